```python
import jax, jax.numpy as jnp
from jax import lax
import numpy as np

D_MODEL = 1024
BATCH = 8
SEQ = 2048
DEPTH = 2
DEC_BATCH = 32
DEC_SEQ = 4
PAST_LEN = 8192
PAGE_SIZE = 128

N_MIXERS = 2
N_RET_LAYERS = (DEPTH + 1) // 2
N_MOBA_LAYERS = DEPTH // 2
RET_HEADS = 4
RET_DK = 256
RET_DV = 512
RET_CHUNK = 128
RET_QK = RET_HEADS * RET_DK
RET_V = RET_HEADS * RET_DV
MOBA_HEADS = 8
MOBA_DH = 128
MOBA_W = MOBA_HEADS * MOBA_DH
MOBA_BLOCK = 256
MOBA_TOPK = 3
MOBA_QBLK = 16
EPS = 1e-6
NEG = -1e30

kernel_name = "retnet_moba_hybrid_step"


def rmsnorm(x, g):
    xf = x.astype(jnp.float32)
    y = xf * lax.rsqrt(jnp.mean(xf * xf, axis=-1, keepdims=True) + EPS)
    return (y * g.astype(jnp.float32)).astype(x.dtype)


def head_layernorm(x, g):
    xf = x.astype(jnp.float32)
    mu = jnp.mean(xf, axis=-1, keepdims=True)
    xc = xf - mu
    y = xc * lax.rsqrt(jnp.mean(xc * xc, axis=-1, keepdims=True) + EPS)
    return (y * g.astype(jnp.float32)).astype(x.dtype)


def adaln(c, w, b):
    mod = jax.nn.silu(c) @ w + b
    shift, scale, gate = jnp.split(mod, 3, axis=-1)
    return shift[:, None, :], scale[:, None, :], gate[:, None, :]


def ret_log_gamma():
    h = jnp.arange(RET_HEADS, dtype=jnp.float32)
    return jnp.log1p(-jnp.exp2(-5.0 - h))


def alibi_slopes():
    h = jnp.arange(MOBA_HEADS, dtype=jnp.float32)
    return jnp.exp2(-8.0 * (h + 1.0) / MOBA_HEADS)


def retention(q, k, v, s0):
    B, H, T, DK = q.shape
    DV = v.shape[-1]
    C = RET_CHUNK if T % RET_CHUNK == 0 else T
    n = T // C
    log_g = ret_log_gamma()
    i = jnp.arange(C, dtype=jnp.float32)
    diff = i[:, None] - i[None, :]
    decay_mat = jnp.where(diff >= 0, jnp.exp(log_g[:, None, None] * jnp.maximum(diff, 0.0)), 0.0)
    q_decay = jnp.exp(log_g[:, None] * (i + 1.0))[..., None]
    k_decay = jnp.exp(log_g[:, None] * (C - 1.0 - i))[..., None]
    chunk_decay = jnp.exp(log_g * C)[:, None, None]

    def to_chunks(a):
        return a.astype(jnp.float32).reshape(B, H, n, C, a.shape[-1]).transpose(2, 0, 1, 3, 4)

    def step(s, inp):
        qi, ki, vi = inp
        scores = jnp.einsum('bhik,bhjk->bhij', qi, ki) * decay_mat
        inner = jnp.einsum('bhij,bhjv->bhiv', scores, vi)
        cross = jnp.einsum('bhik,bhkv->bhiv', qi * q_decay, s)
        s_new = chunk_decay * s + jnp.einsum('bhjk,bhjv->bhkv', ki * k_decay, vi)
        return s_new, inner + cross

    s_fin, out = lax.scan(step, s0.astype(jnp.float32), (to_chunks(q), to_chunks(k), to_chunks(v)))
    out = out.transpose(1, 2, 0, 3, 4).reshape(B, H, T, DV)
    return out.astype(v.dtype), s_fin


def retention_mixer(h, s0, w_in, gn_g, w_out):
    B, T, _ = h.shape
    proj = h @ w_in
    q, k, v, g = jnp.split(proj, [RET_QK, 2 * RET_QK, 2 * RET_QK + RET_V], axis=-1)
    q = q.reshape(B, T, RET_HEADS, RET_DK).transpose(0, 2, 1, 3)
    k = (k * (RET_DK ** -0.5)).reshape(B, T, RET_HEADS, RET_DK).transpose(0, 2, 1, 3)
    v = v.reshape(B, T, RET_HEADS, RET_DV).transpose(0, 2, 1, 3)
    o, s_fin = retention(q, k, v, s0)
    o = head_layernorm(o.transpose(0, 2, 1, 3), gn_g).reshape(B, T, RET_V)
    return (o * jax.nn.silu(g)) @ w_out, s_fin


def moba_attend(q, k_all, v_all, q_pos):
    B, H, Tq, DH = q.shape
    L = k_all.shape[1]
    nb = L // MOBA_BLOCK
    kb = k_all.reshape(B, nb, MOBA_BLOCK, H, DH)
    vb = v_all.reshape(B, nb, MOBA_BLOCK, H, DH)
    k_mean = jnp.mean(kb.astype(jnp.float32), axis=2)
    topk = min(MOBA_TOPK, nb)
    slopes = alibi_slopes()[None, :, None, None, None]
    qb = MOBA_QBLK if Tq % MOBA_QBLK == 0 else Tq
    nqb = Tq // qb
    q_blocks = q.reshape(B, H, nqb, qb, DH).transpose(2, 0, 1, 3, 4)
    pos_blocks = q_pos.reshape(nqb, qb)
    blk_ids = jnp.arange(nb, dtype=jnp.int32)
    offs = jnp.arange(MOBA_BLOCK, dtype=jnp.int32)
    bi = jnp.arange(B)[:, None, None, None]
    hi = jnp.arange(H)[None, :, None, None]
    scale = DH ** -0.5

    def one(args):
        qi, pi = args
        qf = qi.astype(jnp.float32)
        own = pi // MOBA_BLOCK
        gate_scores = jnp.einsum('bhqd,bnhd->bhqn', qf, k_mean)
        past = blk_ids[None, :] < own[:, None]
        gate_scores = jnp.where(past, gate_scores, NEG)
        _, sel = lax.top_k(gate_scores, topk)
        sel_valid = sel < own[:, None]
        own_b = jnp.broadcast_to(own[:, None], (B, H, qb, 1))
        idx = jnp.concatenate([sel, own_b], axis=-1)
        valid = jnp.concatenate([sel_valid, jnp.ones((B, H, qb, 1), dtype=bool)], axis=-1)
        kg = kb[bi, idx, :, hi].astype(jnp.float32)
        vg = vb[bi, idx, :, hi].astype(jnp.float32)
        kpos = idx[..., None] * MOBA_BLOCK + offs
        dist = pi[:, None, None] - kpos
        s = jnp.einsum('bhqd,bhqnkd->bhqnk', qf, kg) * scale - slopes * dist.astype(jnp.float32)
        s = jnp.where(valid[..., None] & (dist >= 0), s, NEG)
        nsel = idx.shape[-1]
        p = jax.nn.softmax(s.reshape(B, H, qb, nsel * MOBA_BLOCK), axis=-1).reshape(s.shape)
        return jnp.einsum('bhqnk,bhqnkd->bhqd', p, vg)

    out = lax.map(one, (q_blocks, pos_blocks))
    return out.transpose(1, 2, 0, 3, 4).reshape(B, H, Tq, DH)


def moba_mixer(h, k_past, v_past, pos0, w_in, qn_g, kn_g, w_out):
    B, T, _ = h.shape
    proj = h @ w_in
    q, k, v, g = jnp.split(proj, 4, axis=-1)
    q = rmsnorm(q.reshape(B, T, MOBA_HEADS, MOBA_DH), qn_g)
    k = rmsnorm(k.reshape(B, T, MOBA_HEADS, MOBA_DH), kn_g)
    v = v.reshape(B, T, MOBA_HEADS, MOBA_DH)
    past_len = 0 if k_past is None else k_past.shape[1]
    L = past_len + T
    pad = (-L) % MOBA_BLOCK
    zpad = jnp.zeros((B, pad, MOBA_HEADS, MOBA_DH), k.dtype)
    if k_past is None:
        k_all = jnp.concatenate([k, zpad], axis=1)
        v_all = jnp.concatenate([v, zpad], axis=1)
    else:
        k_all = jnp.concatenate([k_past.astype(k.dtype), k, zpad], axis=1)
        v_all = jnp.concatenate([v_past.astype(v.dtype), v, zpad], axis=1)
    q_pos = pos0 + jnp.arange(T, dtype=jnp.int32)
    o = moba_attend(q.transpose(0, 2, 1, 3), k_all, v_all, q_pos)
    o = o.astype(h.dtype).transpose(0, 2, 1, 3).reshape(B, T, MOBA_W)
    return (o * jax.nn.silu(g)) @ w_out, k, v


def setup_inputs(seed: int = 0) -> dict:
    key = jax.random.key(seed)
    ks = jax.random.split(key, 20)
    n_pages = PAST_LEN // PAGE_SIZE
    used = DEC_BATCH * n_pages
    n_pool = used + max(1, used // 4)
    f32 = jnp.float32
    D = D_MODEL
    nrm = lambda k, shape, s: jax.random.normal(k, shape, f32) * s
    page_table = jax.random.permutation(ks[0], n_pool)[:used].reshape(DEC_BATCH, n_pages).astype(jnp.int32)
    return {
        "x_prompt": nrm(ks[1], (BATCH, SEQ, D), 1.0),
        "x_sample": nrm(ks[2], (DEC_BATCH, DEC_SEQ, D), 1.0),
        "c_prompt": nrm(ks[3], (BATCH, D), 1.0),
        "c_sample": nrm(ks[4], (DEC_BATCH, D), 1.0),
        "state_ret": nrm(ks[5], (N_RET_LAYERS, DEC_BATCH, RET_HEADS, RET_DK, RET_DV), 0.1),
        "cache_k": nrm(ks[6], (N_MOBA_LAYERS, n_pool, PAGE_SIZE, MOBA_HEADS, MOBA_DH), 1.0),
        "cache_v": nrm(ks[7], (N_MOBA_LAYERS, n_pool, PAGE_SIZE, MOBA_HEADS, MOBA_DH), 1.0),
        "page_table": page_table,
        "norm_g": 1.0 + nrm(ks[8], (DEPTH, D), 0.02),
        "w_ada": nrm(ks[9], (DEPTH, D, 3 * D), 0.5 * D ** -0.5),
        "b_ada": nrm(ks[10], (DEPTH, 3 * D), 0.02),
        "w_ret_in": nrm(ks[11], (N_RET_LAYERS, D, 2 * RET_QK + 2 * RET_V), D ** -0.5),
        "ret_gn_g": 1.0 + nrm(ks[12], (N_RET_LAYERS, RET_HEADS, RET_DV), 0.02),
        "w_ret_out": nrm(ks[13], (N_RET_LAYERS, RET_V, D), RET_V ** -0.5),
        "w_moba_in": nrm(ks[14], (N_MOBA_LAYERS, D, 4 * MOBA_W), D ** -0.5),
        "moba_q_g": 1.0 + nrm(ks[15], (N_MOBA_LAYERS, MOBA_DH), 0.02),
        "moba_k_g": 1.0 + nrm(ks[16], (N_MOBA_LAYERS, MOBA_DH), 0.02),
        "w_moba_out": nrm(ks[17], (N_MOBA_LAYERS, MOBA_W, D), MOBA_W ** -0.5),
    }


def reference(x_prompt, x_sample, c_prompt, c_sample, state_ret, cache_k, cache_v, page_table,
              norm_g, w_ada, b_ada, w_ret_in, ret_gn_g, w_ret_out,
              w_moba_in, moba_q_g, moba_k_g, w_moba_out):
    yp, ys = x_prompt, x_sample
    Bp = x_prompt.shape[0]
    Bs, Ts, _ = x_sample.shape
    ret_p, ret_s, kp_l, vp_l, ks_l, vs_l = [], [], [], [], [], []
    for layer in range(DEPTH):
        j = layer // N_MIXERS
        sh_p, sc_p, gt_p = adaln(c_prompt, w_ada[layer], b_ada[layer])
        sh_s, sc_s, gt_s = adaln(c_sample, w_ada[layer], b_ada[layer])
        hp = rmsnorm(yp, norm_g[layer]) * (1.0 + sc_p) + sh_p
        hs = rmsnorm(ys, norm_g[layer]) * (1.0 + sc_s) + sh_s
        if layer % N_MIXERS == 0:
            s0 = jnp.zeros((Bp, RET_HEADS, RET_DK, RET_DV), jnp.float32)
            op, sp = retention_mixer(hp, s0, w_ret_in[j], ret_gn_g[j], w_ret_out[j])
            os_, ss = retention_mixer(hs, state_ret[j], w_ret_in[j], ret_gn_g[j], w_ret_out[j])
            ret_p.append(sp.astype(x_prompt.dtype))
            ret_s.append(ss.astype(x_sample.dtype))
        else:
            past_len = page_table.shape[1] * PAGE_SIZE
            k_past = cache_k[j][page_table].reshape(Bs, past_len, MOBA_HEADS, MOBA_DH)
            v_past = cache_v[j][page_table].reshape(Bs, past_len, MOBA_HEADS, MOBA_DH)
            op, kpn, vpn = moba_mixer(hp, None, None, 0, w_moba_in[j], moba_q_g[j], moba_k_g[j], w_moba_out[j])
            os_, ksn, vsn = moba_mixer(hs, k_past, v_past, past_len, w_moba_in[j], moba_q_g[j], moba_k_g[j], w_moba_out[j])
            kp_l.append(kpn)
            vp_l.append(vpn)
            ks_l.append(ksn)
            vs_l.append(vsn)
        yp = yp + gt_p * op
        ys = ys + gt_s * os_
    return (yp, ys, jnp.stack(ret_p), jnp.stack(ret_s), jnp.stack(kp_l), jnp.stack(vp_l), jnp.stack(ks_l), jnp.stack(vs_l))
```

```python
import functools
import math

import jax
import jax.numpy as jnp
from jax import lax
from jax.experimental import pallas as pl
from jax.experimental.pallas import tpu as pltpu

F32 = jnp.float32
BF16 = jnp.bfloat16
I32 = jnp.int32

RET_HEADS = 4
RET_DK = 256
RET_DV = 512
RET_CHUNK = 128
RET_QK = RET_HEADS * RET_DK
RET_V = RET_HEADS * RET_DV
MOBA_HEADS = 8
MOBA_DH = 128
MOBA_W = MOBA_HEADS * MOBA_DH
MOBA_BLOCK = 256
MOBA_TOPK = 3
PAGE_SIZE = 128
EPS = 1e-6
NEG = -1e30
BELOW_NEG = -3e38

V7X_VMEM_BYTES = 64 * 1024 * 1024
NT_DIMS = (((1,), (1,)), ((), ()))


def _params(semantics, vmem_mb):
    assert vmem_mb * 1024 * 1024 < V7X_VMEM_BYTES
    return pltpu.CompilerParams(dimension_semantics=semantics, vmem_limit_bytes=vmem_mb * 1024 * 1024)


def _silu(x):
    return x * jax.nn.sigmoid(x)


def _norm_mod(x, g, sh, sc):
    ms = jnp.mean(x * x, axis=-1, keepdims=True)
    y = x * lax.rsqrt(ms + EPS) * g
    return y * (1.0 + sc) + sh


def _head_rms(x, g):
    ms = jnp.mean(x * x, axis=-1, keepdims=True)
    return x * lax.rsqrt(ms + EPS) * g


def _ret_log_gamma(h):
    return math.log1p(-(2.0 ** (-5.0 - h)))


def _ada_kernel(c_ref, w_ref, b_ref, o_ref):
    s = _silu(c_ref[...]).astype(BF16)
    o_ref[...] = jnp.dot(s, w_ref[...].astype(BF16), preferred_element_type=F32) + b_ref[...]


def _ada(c_all, w_ada, b_ada):
    depth, d, n = w_ada.shape
    m = c_all.shape[0]
    tn = 512
    return pl.pallas_call(
        _ada_kernel,
        grid=(depth, n // tn),
        in_specs=[
            pl.BlockSpec((m, d), lambda l, j: (0, 0)),
            pl.BlockSpec((None, d, tn), lambda l, j: (l, 0, j)),
            pl.BlockSpec((None, 1, tn), lambda l, j: (l, 0, j)),
        ],
        out_specs=pl.BlockSpec((None, m, tn), lambda l, j: (l, 0, j)),
        out_shape=jax.ShapeDtypeStruct((depth, m, n), F32),
        compiler_params=_params(("arbitrary", "arbitrary"), 32),
        name="ada",
    )(c_all, w_ada, b_ada.reshape(depth, 1, n))


def _mod_spec(mod, tm):
    d = mod.shape[-1]
    if mod.shape[1] == 1:
        return pl.BlockSpec((None, 1, d), lambda g, i: (g, 0, 0))
    return pl.BlockSpec((None, tm, d), lambda g, i: (g, i, 0))


def _ret_proj_kernel(x_ref, g_ref, sh_ref, sc_ref, w_ref, o_ref, *, col_chunk):
    h = _norm_mod(x_ref[...], g_ref[...], sh_ref[...], sc_ref[...]).astype(BF16)
    n = w_ref.shape[1]
    for c in range(n // col_chunk):
        cs = slice(c * col_chunk, (c + 1) * col_chunk)
        o_ref[:, cs] = jnp.dot(h, w_ref[:, cs], preferred_element_type=F32).astype(o_ref.dtype)


def _ret_proj(x, norm_g, sh, sc, w_bf16, out_dtype, tm):
    g, r, d = x.shape
    n = w_bf16.shape[1]
    return pl.pallas_call(
        functools.partial(_ret_proj_kernel, col_chunk=512),
        grid=(g, r // tm),
        in_specs=[
            pl.BlockSpec((None, tm, d), lambda gi, i: (gi, i, 0)),
            pl.BlockSpec((1, d), lambda gi, i: (0, 0)),
            _mod_spec(sh, tm),
            _mod_spec(sc, tm),
            pl.BlockSpec((d, n), lambda gi, i: (0, 0)),
        ],
        out_specs=pl.BlockSpec((None, tm, n), lambda gi, i: (gi, i, 0)),
        out_shape=jax.ShapeDtypeStruct((g, r, n), out_dtype),
        compiler_params=_params(("arbitrary", "arbitrary"), 56),
        name="ret_proj",
    )(x, norm_g.reshape(1, d), sh, sc, w_bf16)


def _out_proj_kernel(a_ref, w_ref, x_ref, gt_ref, o_ref):
    o_ref[...] = x_ref[...] + gt_ref[...] * jnp.dot(a_ref[...], w_ref[...], preferred_element_type=F32)


def _out_proj(a, w_bf16, x, gate, tm):
    g, r, k = a.shape
    d = w_bf16.shape[1]
    return pl.pallas_call(
        _out_proj_kernel,
        grid=(g, r // tm),
        in_specs=[
            pl.BlockSpec((None, tm, k), lambda gi, i: (gi, i, 0)),
            pl.BlockSpec((k, d), lambda gi, i: (0, 0)),
            pl.BlockSpec((None, tm, d), lambda gi, i: (gi, i, 0)),
            _mod_spec(gate, tm),
        ],
        out_specs=pl.BlockSpec((None, tm, d), lambda gi, i: (gi, i, 0)),
        out_shape=jax.ShapeDtypeStruct((g, r, d), F32),
        compiler_params=_params(("arbitrary", "arbitrary"), 40),
        name="out_proj",
    )(a, w_bf16, x, gate)


def _ret_prompt_kernel(p_ref, gn_ref, og_ref, s_ref, *, chunk, n_chunks):
    @pl.when(pl.program_id(1) == 0)
    def _():
        s_ref[...] = jnp.zeros_like(s_ref)

    ii = lax.broadcasted_iota(I32, (chunk, chunk), 0)
    jj = lax.broadcasted_iota(I32, (chunk, chunk), 1)
    diff = (ii - jj).astype(F32)
    pos = lax.broadcasted_iota(I32, (chunk, 1), 0).astype(F32)
    for h in range(RET_HEADS):
        lg = _ret_log_gamma(h)
        decay = jnp.where(diff >= 0, jnp.exp(lg * jnp.maximum(diff, 0.0)), 0.0)
        q_decay = jnp.exp(lg * (pos + 1.0))
        k_decay = jnp.exp(lg * (chunk - 1.0 - pos))
        chunk_decay = math.exp(lg * chunk)
        for c in range(n_chunks):
            rows = slice(c * chunk, (c + 1) * chunk)
            q = p_ref[rows, h * RET_DK:(h + 1) * RET_DK]
            k = p_ref[rows, RET_QK + h * RET_DK:RET_QK + (h + 1) * RET_DK].astype(F32) * (RET_DK ** -0.5)
            v = p_ref[rows, 2 * RET_QK + h * RET_DV:2 * RET_QK + (h + 1) * RET_DV]
            gate = p_ref[rows, 2 * RET_QK + RET_V + h * RET_DV:2 * RET_QK + RET_V + (h + 1) * RET_DV].astype(F32)
            state = s_ref[h]
            scores = lax.dot_general(q, k.astype(BF16), NT_DIMS, preferred_element_type=F32) * decay
            inner = jnp.dot(scores.astype(BF16), v, preferred_element_type=F32)
            cross = jnp.dot((q.astype(F32) * q_decay).astype(BF16), state.astype(BF16), preferred_element_type=F32)
            kv = lax.dot_general((k * k_decay).astype(BF16), v, (((0,), (0,)), ((), ())), preferred_element_type=F32)
            s_ref[h] = chunk_decay * state + kv
            o = inner + cross
            oc = o - jnp.mean(o, axis=-1, keepdims=True)
            y = oc * lax.rsqrt(jnp.mean(oc * oc, axis=-1, keepdims=True) + EPS) * gn_ref[h:h + 1, :]
            og_ref[rows, h * RET_DV:(h + 1) * RET_DV] = (y * _silu(gate)).astype(og_ref.dtype)


def _ret_prompt(proj, gn_g, rows_per_step):
    b, t, n = proj.shape
    assert t % RET_CHUNK == 0 and rows_per_step % RET_CHUNK == 0
    return pl.pallas_call(
        functools.partial(_ret_prompt_kernel, chunk=RET_CHUNK, n_chunks=rows_per_step // RET_CHUNK),
        grid=(b, t // rows_per_step),
        in_specs=[
            pl.BlockSpec((None, rows_per_step, n), lambda bi, i: (bi, i, 0)),
            pl.BlockSpec((RET_HEADS, RET_DV), lambda bi, i: (0, 0)),
        ],
        out_specs=[
            pl.BlockSpec((None, rows_per_step, RET_V), lambda bi, i: (bi, i, 0)),
            pl.BlockSpec((None, RET_HEADS, RET_DK, RET_DV), lambda bi, i: (bi, 0, 0, 0)),
        ],
        out_shape=[
            jax.ShapeDtypeStruct((b, t, RET_V), BF16),
            jax.ShapeDtypeStruct((b, RET_HEADS, RET_DK, RET_DV), F32),
        ],
        compiler_params=_params(("arbitrary", "arbitrary"), 40),
        name="ret_prompt",
    )(proj, gn_g)


def _ret_sample_kernel(p_ref, kt_ref, s0_ref, gn_ref, og_ref, s_ref, *, t):
    ii = lax.broadcasted_iota(I32, (t, t), 0)
    jj = lax.broadcasted_iota(I32, (t, t), 1)
    diff = (ii - jj).astype(F32)
    pos = lax.broadcasted_iota(I32, (t, 1), 0).astype(F32)
    for h in range(RET_HEADS):
        lg = _ret_log_gamma(h)
        decay = jnp.where(diff >= 0, jnp.exp(lg * jnp.maximum(diff, 0.0)), 0.0)
        q_decay = jnp.exp(lg * (pos + 1.0))
        chunk_decay = math.exp(lg * t)
        q = p_ref[:, h * RET_DK:(h + 1) * RET_DK]
        k = p_ref[:, RET_QK + h * RET_DK:RET_QK + (h + 1) * RET_DK] * (RET_DK ** -0.5)
        v = p_ref[:, 2 * RET_QK + h * RET_DV:2 * RET_QK + (h + 1) * RET_DV]
        gate = p_ref[:, 2 * RET_QK + RET_V + h * RET_DV:2 * RET_QK + RET_V + (h + 1) * RET_DV]
        kt = kt_ref[h * RET_DK:(h + 1) * RET_DK, :] * (RET_DK ** -0.5)
        state = s0_ref[h]
        o = jnp.dot(q * q_decay, state, preferred_element_type=F32)
        new_state = chunk_decay * state
        for j in range(t):
            s_j = jnp.sum(q * k[j:j + 1, :], axis=-1, keepdims=True) * decay[:, j:j + 1]
            o = o + s_j * v[j:j + 1, :]
            new_state = new_state + (kt[:, j:j + 1] * math.exp(lg * (t - 1.0 - j))) * v[j:j + 1, :]
        s_ref[h] = new_state
        oc = o - jnp.mean(o, axis=-1, keepdims=True)
        y = oc * lax.rsqrt(jnp.mean(oc * oc, axis=-1, keepdims=True) + EPS) * gn_ref[h:h + 1, :]
        og_ref[:, h * RET_DV:(h + 1) * RET_DV] = (y * _silu(gate)).astype(og_ref.dtype)


def _ret_sample(proj, k_t, state0, gn_g):
    b, t, n = proj.shape
    return pl.pallas_call(
        functools.partial(_ret_sample_kernel, t=t),
        grid=(b,),
        in_specs=[
            pl.BlockSpec((None, t, n), lambda bi: (bi, 0, 0)),
            pl.BlockSpec((None, RET_QK, t), lambda bi: (bi, 0, 0)),
            pl.BlockSpec((None, RET_HEADS, RET_DK, RET_DV), lambda bi: (bi, 0, 0, 0)),
            pl.BlockSpec((RET_HEADS, RET_DV), lambda bi: (0, 0)),
        ],
        out_specs=[
            pl.BlockSpec((None, t, RET_V), lambda bi: (bi, 0, 0)),
            pl.BlockSpec((None, RET_HEADS, RET_DK, RET_DV), lambda bi: (bi, 0, 0, 0)),
        ],
        out_shape=[
            jax.ShapeDtypeStruct((b, t, RET_V), F32),
            jax.ShapeDtypeStruct((b, RET_HEADS, RET_DK, RET_DV), F32),
        ],
        compiler_params=_params(("arbitrary",), 40),
        name="ret_sample",
    )(proj, k_t, state0, gn_g)


def _moba_proj_kernel(x_ref, g_ref, sh_ref, sc_ref, w_ref, qg_ref, kg_ref, q_ref, k_ref, v_ref, gt_ref):
    h = _norm_mod(x_ref[...], g_ref[...], sh_ref[...], sc_ref[...]).astype(BF16)
    w = MOBA_W
    q = jnp.dot(h, w_ref[:, 0:w], preferred_element_type=F32)
    k = jnp.dot(h, w_ref[:, w:2 * w], preferred_element_type=F32)
    for hd in range(MOBA_HEADS):
        cs = slice(hd * MOBA_DH, (hd + 1) * MOBA_DH)
        q_ref[:, cs] = _head_rms(q[:, cs], qg_ref[...]).astype(q_ref.dtype)
        k_ref[:, cs] = _head_rms(k[:, cs], kg_ref[...])
    v_ref[...] = jnp.dot(h, w_ref[:, 2 * w:3 * w], preferred_element_type=F32)
    gt_ref[...] = jnp.dot(h, w_ref[:, 3 * w:4 * w], preferred_element_type=F32).astype(gt_ref.dtype)


def _moba_proj(x, norm_g, sh, sc, w_bf16, q_g, k_g, act_dtype, tm):
    g, r, d = x.shape
    n = w_bf16.shape[1]
    row_spec = pl.BlockSpec((None, tm, MOBA_W), lambda gi, i: (gi, i, 0))
    return pl.pallas_call(
        _moba_proj_kernel,
        grid=(g, r // tm),
        in_specs=[
            pl.BlockSpec((None, tm, d), lambda gi, i: (gi, i, 0)),
            pl.BlockSpec((1, d), lambda gi, i: (0, 0)),
            _mod_spec(sh, tm),
            _mod_spec(sc, tm),
            pl.BlockSpec((d, n), lambda gi, i: (0, 0)),
            pl.BlockSpec((1, MOBA_DH), lambda gi, i: (0, 0)),
            pl.BlockSpec((1, MOBA_DH), lambda gi, i: (0, 0)),
        ],
        out_specs=[row_spec, row_spec, row_spec, row_spec],
        out_shape=[
            jax.ShapeDtypeStruct((g, r, MOBA_W), act_dtype),
            jax.ShapeDtypeStruct((g, r, MOBA_W), F32),
            jax.ShapeDtypeStruct((g, r, MOBA_W), F32),
            jax.ShapeDtypeStruct((g, r, MOBA_W), act_dtype),
        ],
        compiler_params=_params(("arbitrary", "arbitrary"), 48),
        name="moba_proj",
    )(x, norm_g.reshape(1, d), sh, sc, w_bf16, q_g.reshape(1, MOBA_DH), k_g.reshape(1, MOBA_DH))


def _alibi_slope(head):
    hv = jnp.full((1, 1), head, I32).astype(F32)
    return jnp.exp2(-8.0 * (hv + 1.0) / MOBA_HEADS)


def _moba_prompt_kernel(q_ref, k_ref, v_ref, g_ref, o_ref, kb_scr, vt_scr, km_scr, pen_scr, *, n_blocks):
    blk = MOBA_BLOCK
    head = pl.program_id(1)
    t = pl.program_id(2)

    @pl.when(t == 0)
    def _():
        for n in range(n_blocks):
            kn = k_ref[n * blk:(n + 1) * blk, :]
            kb_scr[n * blk:(n + 1) * blk, :] = kn.astype(BF16)
            vt_scr[n] = v_ref[n * blk:(n + 1) * blk, :].T.astype(BF16)
            mean = jnp.mean(kn, axis=0, keepdims=True)
            hi = mean.astype(BF16).astype(F32)
            mid = (mean - hi).astype(BF16).astype(F32)
            lo = (mean - hi - mid).astype(BF16).astype(F32)
            km_scr[n:n + 1, :] = hi
            km_scr[n_blocks + n:n_blocks + n + 1, :] = mid
            km_scr[2 * n_blocks + n:2 * n_blocks + n + 1, :] = lo
        km_scr[3 * n_blocks:, :] = jnp.zeros((km_scr.shape[0] - 3 * n_blocks, MOBA_DH), F32)

    q = q_ref[...]
    scale = MOBA_DH ** -0.5
    slope = _alibi_slope(head)

    parts = lax.dot_general(km_scr[...].astype(BF16), q, NT_DIMS, preferred_element_type=F32)
    gs = parts[0:n_blocks] + parts[n_blocks:2 * n_blocks] + parts[2 * n_blocks:3 * n_blocks]
    nid = lax.broadcasted_iota(I32, (n_blocks, blk), 0)
    past = nid < t
    gs = jnp.where(past, gs, NEG)
    sel = jnp.zeros((n_blocks, blk), F32)
    for _ in range(min(MOBA_TOPK, n_blocks)):
        mx = jnp.max(gs, axis=0, keepdims=True)
        idx = jnp.min(jnp.where(gs == mx, nid, n_blocks), axis=0, keepdims=True)
        pick = nid == idx
        sel = jnp.where(pick & past, 1.0, sel)
        gs = jnp.where(pick, BELOW_NEG, gs)
    pen_scr[...] = sel

    ksub = lax.broadcasted_iota(I32, (blk, blk), 0)
    qlane = lax.broadcasted_iota(I32, (blk, blk), 1)
    delta = qlane - ksub

    k_own = kb_scr[pl.ds(pl.multiple_of(t * blk, blk), blk), :]
    s = lax.dot_general(k_own, q, NT_DIMS, preferred_element_type=F32) * scale
    s = jnp.where(delta >= 0, s - slope * delta.astype(F32), NEG)
    m0 = jnp.max(s, axis=0, keepdims=True)
    p = jnp.exp(s - m0)
    l0 = jnp.sum(p, axis=0, keepdims=True)
    acc0 = jnp.dot(vt_scr[t], p.astype(BF16), preferred_element_type=F32)

    def past_block(n, carry):
        m, l, acc = carry
        kn = kb_scr[pl.ds(pl.multiple_of(n * blk, blk), blk), :]
        s = lax.dot_general(kn, q, NT_DIMS, preferred_element_type=F32) * scale
        dist = (delta + (t - n) * blk).astype(F32)
        chosen = pen_scr[pl.ds(n, 1), :] > 0.5
        s = jnp.where(chosen, s - slope * dist, NEG)
        m_new = jnp.maximum(m, jnp.max(s, axis=0, keepdims=True))
        alpha = jnp.exp(m - m_new)
        p = jnp.exp(s - m_new)
        l_new = alpha * l + jnp.sum(p, axis=0, keepdims=True)
        acc_new = alpha * acc + jnp.dot(vt_scr[n], p.astype(BF16), preferred_element_type=F32)
        return m_new, l_new, acc_new

    _, l, acc = lax.fori_loop(0, t, past_block, (m0, l0, acc0))
    o = (acc / l).T
    o_ref[...] = (o * _silu(g_ref[...].astype(F32))).astype(o_ref.dtype)


def _moba_prompt(q, k, v, gate):
    b, t, w = q.shape
    assert t % MOBA_BLOCK == 0
    n_blocks = t // MOBA_BLOCK
    km_rows = -(-3 * n_blocks // 16) * 16
    qspec = pl.BlockSpec((None, MOBA_BLOCK, MOBA_DH), lambda bi, h, i: (bi, i, h))
    kvspec = pl.BlockSpec((None, t, MOBA_DH), lambda bi, h, i: (bi, 0, h))
    return pl.pallas_call(
        functools.partial(_moba_prompt_kernel, n_blocks=n_blocks),
        grid=(b, MOBA_HEADS, n_blocks),
        in_specs=[qspec, kvspec, kvspec, qspec],
        out_specs=qspec,
        out_shape=jax.ShapeDtypeStruct((b, t, w), BF16),
        scratch_shapes=[
            pltpu.VMEM((t, MOBA_DH), BF16),
            pltpu.VMEM((n_blocks, MOBA_DH, MOBA_BLOCK), BF16),
            pltpu.VMEM((km_rows, MOBA_DH), F32),
            pltpu.VMEM((n_blocks, MOBA_BLOCK), F32),
        ],
        compiler_params=_params(("arbitrary", "arbitrary", "arbitrary"), 40),
        name="moba_prompt",
    )(q, k, v, gate)


def _kmean_kernel(pt_ref, *refs, blocks_per_step, pages_per_block):
    del pt_ref
    o_ref = refs[-1]
    for j in range(blocks_per_step):
        tot = None
        for p in range(pages_per_block):
            part = jnp.sum(refs[j * pages_per_block + p][...], axis=0)
            tot = part if tot is None else tot + part
        o_ref[j] = tot / float(pages_per_block * PAGE_SIZE)


def _kmean_pages(cache_pages, page_table):
    b, n_pages = page_table.shape
    hd = cache_pages.shape[2:]
    ppb = MOBA_BLOCK // PAGE_SIZE
    n_blocks = n_pages // ppb
    bps = 8
    assert n_pages % ppb == 0 and n_blocks % bps == 0

    def page_spec(p):
        return pl.BlockSpec((None, PAGE_SIZE, *hd), lambda bi, j, pt: (pt[bi, j * bps * ppb + p], 0, 0, 0))

    return pl.pallas_call(
        functools.partial(_kmean_kernel, blocks_per_step=bps, pages_per_block=ppb),
        grid_spec=pltpu.PrefetchScalarGridSpec(
            num_scalar_prefetch=1,
            grid=(b, n_blocks // bps),
            in_specs=[page_spec(p) for p in range(bps * ppb)],
            out_specs=pl.BlockSpec((None, bps, *hd), lambda bi, j, pt: (bi, j, 0, 0)),
        ),
        out_shape=jax.ShapeDtypeStruct((b, n_blocks, *hd), F32),
        compiler_params=_params(("arbitrary", "arbitrary"), 40),
        name="kmean_pages",
    )(page_table, *([cache_pages] * (bps * ppb)))


def _moba_select_kernel(q_ref, km_ref, sel_ref, *, t, topk):
    n_blocks = km_ref.shape[0]
    rows = sel_ref.shape[0]
    rid = lax.broadcasted_iota(I32, (rows, 128), 0)
    lid = lax.broadcasted_iota(I32, (rows, 128), 1)
    nid = lax.broadcasted_iota(I32, (n_blocks, 1), 0)
    out = jnp.zeros((rows, 128), I32)
    for h in range(MOBA_HEADS):
        cs = slice(h * MOBA_DH, (h + 1) * MOBA_DH)
        km = km_ref[:, h, :]
        for i in range(t):
            gs = jnp.sum(km * q_ref[i:i + 1, cs], axis=-1, keepdims=True)
            for r in range(topk):
                mx = jnp.max(gs, axis=0, keepdims=True)
                idx = jnp.min(jnp.where(gs == mx, nid, n_blocks), axis=0, keepdims=True)
                out = jnp.where((rid == h * t + i) & (lid == r), idx, out)
                gs = jnp.where(nid == idx, BELOW_NEG, gs)
    sel_ref[...] = out


def _moba_select(q, kmean):
    b, t, w = q.shape
    n_blocks = kmean.shape[1]
    rows = MOBA_HEADS * t
    return pl.pallas_call(
        functools.partial(_moba_select_kernel, t=t, topk=MOBA_TOPK),
        grid=(b,),
        in_specs=[
            pl.BlockSpec((None, t, w), lambda bi: (bi, 0, 0)),
            pl.BlockSpec((None, n_blocks, MOBA_HEADS, MOBA_DH), lambda bi: (bi, 0, 0, 0)),
        ],
        out_specs=pl.BlockSpec((None, rows, 128), lambda bi: (bi, 0, 0)),
        out_shape=jax.ShapeDtypeStruct((b, rows, 128), I32),
        compiler_params=_params(("arbitrary",), 32),
        name="moba_select",
    )(q, kmean)


def _moba_sample_kernel(pt_ref, sel_ref, q_ref, kn_ref, vn_ref, g_ref, ck_ref, cv_ref, o_ref, kbuf, vbuf, sems,
                        *, t, past_len, topk, pages_per_block):
    b = pl.program_id(0)
    head = pl.program_id(1)
    n_b = pl.num_programs(0)
    n_h = pl.num_programs(1)
    step = b * n_h + head

    def page_copies(bi, hi, slot):
        copies = []
        for i in range(t):
            for s in range(topk):
                blk_id = sel_ref[bi, (hi * t + i) * topk + s]
                for p in range(pages_per_block):
                    page = pt_ref[bi, blk_id * pages_per_block + p]
                    j = (i * topk + s) * pages_per_block + p
                    copies.append(pltpu.make_async_copy(ck_ref.at[page, :, hi, :], kbuf.at[slot, j], sems.at[0, slot]))
                    copies.append(pltpu.make_async_copy(cv_ref.at[page, :, hi, :], vbuf.at[slot, j], sems.at[1, slot]))
        return copies

    @pl.when(step == 0)
    def _():
        for c in page_copies(b, head, 0):
            c.start()

    @pl.when(step + 1 < n_b * n_h)
    def _():
        wrap = head + 1 == n_h
        for c in page_copies(jnp.where(wrap, b + 1, b), jnp.where(wrap, 0, head + 1), (step + 1) % 2):
            c.start()

    slot = step % 2
    for c in page_copies(b, head, slot):
        c.wait()

    scale = MOBA_DH ** -0.5
    slope = _alibi_slope(head)
    q = q_ref[...]
    kn = kn_ref[...]
    vn = vn_ref[...]
    gate = g_ref[...]
    off = lax.broadcasted_iota(I32, (PAGE_SIZE, 1), 0)
    jn = lax.broadcasted_iota(I32, (t, 1), 0)
    for i in range(t):
        qi = q[i:i + 1, :]
        d_own = i - jn
        s_own = jnp.sum(kn * qi, axis=-1, keepdims=True) * scale - slope * d_own.astype(F32)
        cols = [jnp.where(d_own >= 0, s_own, NEG)]
        for s in range(topk):
            blk_id = sel_ref[b, (head * t + i) * topk + s]
            for p in range(pages_per_block):
                kp = kbuf[slot, (i * topk + s) * pages_per_block + p]
                dist = (past_len + i) - (blk_id * MOBA_BLOCK + p * PAGE_SIZE + off)
                sc = jnp.sum(kp * qi, axis=-1, keepdims=True) * scale - slope * dist.astype(F32)
                cols.append(jnp.where(dist >= 0, sc, NEG))
        m = cols[0].max(axis=0, keepdims=True)
        for c in cols[1:]:
            m = jnp.maximum(m, c.max(axis=0, keepdims=True))
        p_own = jnp.exp(cols[0] - m)
        l = jnp.sum(p_own, axis=0, keepdims=True)
        o = jnp.sum(p_own * vn, axis=0, keepdims=True)
        for jj, c in enumerate(cols[1:]):
            pc = jnp.exp(c - m)
            l = l + jnp.sum(pc, axis=0, keepdims=True)
            o = o + jnp.sum(pc * vbuf[slot, i * topk * pages_per_block + jj], axis=0, keepdims=True)
        o_ref[i:i + 1, :] = (o / l) * _silu(gate[i:i + 1, :])


def _moba_sample(q, k_new, v_new, gate, cache_k_pages, cache_v_pages, page_table, sel, past_len):
    b, t, w = q.shape
    ppb = MOBA_BLOCK // PAGE_SIZE
    topk = MOBA_TOPK
    row_spec = pl.BlockSpec((None, t, MOBA_DH), lambda bi, h, pt, sl: (bi, 0, h))
    hbm_spec = pl.BlockSpec(memory_space=pl.ANY)
    n_sel = t * topk * ppb
    return pl.pallas_call(
        functools.partial(_moba_sample_kernel, t=t, past_len=past_len, topk=topk, pages_per_block=ppb),
        grid_spec=pltpu.PrefetchScalarGridSpec(
            num_scalar_prefetch=2,
            grid=(b, MOBA_HEADS),
            in_specs=[row_spec, row_spec, row_spec, row_spec, hbm_spec, hbm_spec],
            out_specs=row_spec,
            scratch_shapes=[
                pltpu.VMEM((2, n_sel, PAGE_SIZE, MOBA_DH), F32),
                pltpu.VMEM((2, n_sel, PAGE_SIZE, MOBA_DH), F32),
                pltpu.SemaphoreType.DMA((2, 2)),
            ],
        ),
        out_shape=jax.ShapeDtypeStruct((b, t, w), F32),
        compiler_params=_params(("arbitrary", "arbitrary"), 40),
        name="moba_sample",
    )(page_table, sel, q, k_new, v_new, gate, cache_k_pages, cache_v_pages)


def kernel(x_prompt, x_sample, c_prompt, c_sample, state_ret, cache_k, cache_v, page_table, norm_g, w_ada, b_ada,
           w_ret_in, ret_gn_g, w_ret_out, w_moba_in, moba_q_g, moba_k_g, w_moba_out):
    bp, tp, d = x_prompt.shape
    bs, ts, _ = x_sample.shape
    n_pages = page_table.shape[1]
    past_len = n_pages * PAGE_SIZE
    assert past_len % MOBA_BLOCK == 0 and ts <= MOBA_BLOCK and past_len // MOBA_BLOCK >= MOBA_TOPK
    assert w_ada.shape[0] == 2 and w_ret_in.shape[0] == 1 and w_moba_in.shape[0] == 1

    n_c = bp + bs
    c_all = jnp.concatenate([c_prompt, c_sample, jnp.zeros((-n_c % 8, d), F32)], axis=0)
    mod = _ada(c_all, w_ada, b_ada)

    def group_mod(layer):
        mp = mod[layer, :bp].reshape(bp, 1, 3 * d)
        ms = jnp.repeat(mod[layer, bp:n_c], ts, axis=0).reshape(1, bs * ts, 3 * d)
        return [(m[..., :d], m[..., d:2 * d], m[..., 2 * d:]) for m in (mp, ms)]

    xs = x_sample.reshape(1, bs * ts, d)
    w_ret_in_b = w_ret_in[0].astype(BF16)
    w_ret_out_b = w_ret_out[0].astype(BF16)
    w_moba_in_b = w_moba_in[0].astype(BF16)
    w_moba_out_b = w_moba_out[0].astype(BF16)

    (sh_p, sc_p, gt_p), (sh_s, sc_s, gt_s) = group_mod(0)
    proj_p = _ret_proj(x_prompt, norm_g[0], sh_p, sc_p, w_ret_in_b, BF16, tm=256)
    og_p, ret_p = _ret_prompt(proj_p, ret_gn_g[0], rows_per_step=256)
    y_p = _out_proj(og_p, w_ret_out_b, x_prompt, gt_p, tm=512)

    proj_s = _ret_proj(xs, norm_g[0], sh_s, sc_s, w_ret_in_b, F32, tm=bs * ts).reshape(bs, ts, -1)
    k_t = jnp.swapaxes(proj_s[:, :, RET_QK:2 * RET_QK], 1, 2)
    og_s, ret_s = _ret_sample(proj_s, k_t, state_ret[0], ret_gn_g[0])
    y_s = _out_proj(og_s.reshape(1, bs * ts, RET_V).astype(BF16), w_ret_out_b, xs, gt_s, tm=bs * ts)

    (sh_p, sc_p, gt_p), (sh_s, sc_s, gt_s) = group_mod(1)
    q_p, k_p, v_p, g_p = _moba_proj(y_p, norm_g[1], sh_p, sc_p, w_moba_in_b, moba_q_g[0], moba_k_g[0], BF16, tm=256)
    oa_p = _moba_prompt(q_p, k_p, v_p, g_p)
    y_p = _out_proj(oa_p, w_moba_out_b, y_p, gt_p, tm=512)

    q_s, k_s, v_s, g_s = _moba_proj(y_s, norm_g[1], sh_s, sc_s, w_moba_in_b, moba_q_g[0], moba_k_g[0], F32,
                                    tm=bs * ts)
    q_s, k_s, v_s, g_s = (a.reshape(bs, ts, MOBA_W) for a in (q_s, k_s, v_s, g_s))
    ck = cache_k[0]
    cv = cache_v[0]
    kmean = _kmean_pages(ck, page_table)
    sel = _moba_select(q_s, kmean)[:, :, :MOBA_TOPK].reshape(bs, MOBA_HEADS * ts * MOBA_TOPK)
    oa_s = _moba_sample(q_s, k_s, v_s, g_s, ck, cv, page_table, sel, past_len)
    y_s = _out_proj(oa_s.reshape(1, bs * ts, MOBA_W).astype(BF16), w_moba_out_b, y_s, gt_s, tm=bs * ts)

    hd = (MOBA_HEADS, MOBA_DH)
    return (y_p, y_s.reshape(bs, ts, d), ret_p[None], ret_s[None],
            k_p.reshape(1, bp, tp, *hd), v_p.reshape(1, bp, tp, *hd),
            k_s.reshape(1, bs, ts, *hd), v_s.reshape(1, bs, ts, *hd))
```

```python
import functools
import math

import jax
import jax.numpy as jnp
from jax import lax
from jax.experimental import pallas as pl
from jax.experimental.pallas import tpu as pltpu

F32 = jnp.float32
BF16 = jnp.bfloat16
I32 = jnp.int32

RET_HEADS = 4
RET_DK = 256
RET_DV = 512
RET_CHUNK = 128
RET_QK = RET_HEADS * RET_DK
RET_V = RET_HEADS * RET_DV
MOBA_HEADS = 8
MOBA_DH = 128
MOBA_W = MOBA_HEADS * MOBA_DH
MOBA_BLOCK = 256
MOBA_TOPK = 3
PAGE_SIZE = 128
EPS = 1e-6
NEG = -1e30
BELOW_NEG = -3e38

V7X_VMEM_BYTES = 64 * 1024 * 1024
NT_DIMS = (((1,), (1,)), ((), ()))


def _params(semantics, vmem_mb):
    assert vmem_mb * 1024 * 1024 < V7X_VMEM_BYTES
    return pltpu.CompilerParams(dimension_semantics=semantics, vmem_limit_bytes=vmem_mb * 1024 * 1024)


def _silu(x):
    return x * jax.nn.sigmoid(x)


def _norm_mod(x, g, sh, sc):
    ms = jnp.mean(x * x, axis=-1, keepdims=True)
    y = x * lax.rsqrt(ms + EPS) * g
    return y * (1.0 + sc) + sh


def _head_rms(x, g):
    ms = jnp.mean(x * x, axis=-1, keepdims=True)
    return x * lax.rsqrt(ms + EPS) * g


def _ret_log_gamma(h):
    return math.log1p(-(2.0 ** (-5.0 - h)))


def _ada_kernel(c_ref, w_ref, b_ref, o_ref):
    s = _silu(c_ref[...]).astype(BF16)
    o_ref[...] = jnp.dot(s, w_ref[...].astype(BF16), preferred_element_type=F32) + b_ref[...]


def _ada(c_all, w_ada, b_ada):
    depth, d, n = w_ada.shape
    m = c_all.shape[0]
    tn = 512
    return pl.pallas_call(
        _ada_kernel,
        grid=(depth, n // tn),
        in_specs=[
            pl.BlockSpec((m, d), lambda l, j: (0, 0)),
            pl.BlockSpec((None, d, tn), lambda l, j: (l, 0, j)),
            pl.BlockSpec((None, 1, tn), lambda l, j: (l, 0, j)),
        ],
        out_specs=pl.BlockSpec((None, m, tn), lambda l, j: (l, 0, j)),
        out_shape=jax.ShapeDtypeStruct((depth, m, n), F32),
        compiler_params=_params(("arbitrary", "arbitrary"), 32),
        name="ada",
    )(c_all, w_ada, b_ada.reshape(depth, 1, n))


def _mod_spec(mod, tm):
    d = mod.shape[-1]
    if mod.shape[1] == 1:
        return pl.BlockSpec((None, 1, d), lambda g, i: (g, 0, 0))
    return pl.BlockSpec((None, tm, d), lambda g, i: (g, i, 0))


def _ret_proj_kernel(x_ref, g_ref, sh_ref, sc_ref, w_ref, o_ref, *, col_chunk):
    h = _norm_mod(x_ref[...], g_ref[...], sh_ref[...], sc_ref[...]).astype(BF16)
    n = w_ref.shape[1]
    for c in range(n // col_chunk):
        cs = slice(c * col_chunk, (c + 1) * col_chunk)
        o_ref[:, cs] = jnp.dot(h, w_ref[:, cs], preferred_element_type=F32).astype(o_ref.dtype)


def _ret_proj(x, norm_g, sh, sc, w_bf16, out_dtype, tm):
    g, r, d = x.shape
    n = w_bf16.shape[1]
    return pl.pallas_call(
        functools.partial(_ret_proj_kernel, col_chunk=512),
        grid=(g, r // tm),
        in_specs=[
            pl.BlockSpec((None, tm, d), lambda gi, i: (gi, i, 0)),
            pl.BlockSpec((1, d), lambda gi, i: (0, 0)),
            _mod_spec(sh, tm),
            _mod_spec(sc, tm),
            pl.BlockSpec((d, n), lambda gi, i: (0, 0)),
        ],
        out_specs=pl.BlockSpec((None, tm, n), lambda gi, i: (gi, i, 0)),
        out_shape=jax.ShapeDtypeStruct((g, r, n), out_dtype),
        compiler_params=_params(("arbitrary", "arbitrary"), 56),
        name="ret_proj",
    )(x, norm_g.reshape(1, d), sh, sc, w_bf16)


def _out_proj_kernel(a_ref, w_ref, x_ref, gt_ref, o_ref):
    o_ref[...] = x_ref[...] + gt_ref[...] * jnp.dot(a_ref[...], w_ref[...], preferred_element_type=F32)


def _out_proj(a, w_bf16, x, gate, tm):
    g, r, k = a.shape
    d = w_bf16.shape[1]
    return pl.pallas_call(
        _out_proj_kernel,
        grid=(g, r // tm),
        in_specs=[
            pl.BlockSpec((None, tm, k), lambda gi, i: (gi, i, 0)),
            pl.BlockSpec((k, d), lambda gi, i: (0, 0)),
            pl.BlockSpec((None, tm, d), lambda gi, i: (gi, i, 0)),
            _mod_spec(gate, tm),
        ],
        out_specs=pl.BlockSpec((None, tm, d), lambda gi, i: (gi, i, 0)),
        out_shape=jax.ShapeDtypeStruct((g, r, d), F32),
        compiler_params=_params(("arbitrary", "arbitrary"), 40),
        name="out_proj",
    )(a, w_bf16, x, gate)


def _ret_prompt_kernel(p_ref, gn_ref, og_ref, s_ref, *, chunk, n_chunks):
    @pl.when(pl.program_id(1) == 0)
    def _():
        s_ref[...] = jnp.zeros_like(s_ref)

    ii = lax.broadcasted_iota(I32, (chunk, chunk), 0)
    jj = lax.broadcasted_iota(I32, (chunk, chunk), 1)
    diff = (ii - jj).astype(F32)
    pos = lax.broadcasted_iota(I32, (chunk, 1), 0).astype(F32)
    for h in range(RET_HEADS):
        lg = _ret_log_gamma(h)
        decay = jnp.where(diff >= 0, jnp.exp(lg * jnp.maximum(diff, 0.0)), 0.0)
        q_decay = jnp.exp(lg * (pos + 1.0))
        k_decay = jnp.exp(lg * (chunk - 1.0 - pos))
        chunk_decay = math.exp(lg * chunk)
        for c in range(n_chunks):
            rows = slice(c * chunk, (c + 1) * chunk)
            q = p_ref[rows, h * RET_DK:(h + 1) * RET_DK]
            k = p_ref[rows, RET_QK + h * RET_DK:RET_QK + (h + 1) * RET_DK].astype(F32) * (RET_DK ** -0.5)
            v = p_ref[rows, 2 * RET_QK + h * RET_DV:2 * RET_QK + (h + 1) * RET_DV]
            gate = p_ref[rows, 2 * RET_QK + RET_V + h * RET_DV:2 * RET_QK + RET_V + (h + 1) * RET_DV].astype(F32)
            state = s_ref[h]
            scores = lax.dot_general(q, k.astype(BF16), NT_DIMS, preferred_element_type=F32) * decay
            inner = jnp.dot(scores.astype(BF16), v, preferred_element_type=F32)
            cross = jnp.dot((q.astype(F32) * q_decay).astype(BF16), state.astype(BF16), preferred_element_type=F32)
            kv = lax.dot_general((k * k_decay).astype(BF16), v, (((0,), (0,)), ((), ())), preferred_element_type=F32)
            s_ref[h] = chunk_decay * state + kv
            o = inner + cross
            oc = o - jnp.mean(o, axis=-1, keepdims=True)
            y = oc * lax.rsqrt(jnp.mean(oc * oc, axis=-1, keepdims=True) + EPS) * gn_ref[h:h + 1, :]
            og_ref[rows, h * RET_DV:(h + 1) * RET_DV] = (y * _silu(gate)).astype(og_ref.dtype)


def _ret_prompt(proj, gn_g, rows_per_step):
    b, t, n = proj.shape
    assert t % RET_CHUNK == 0 and rows_per_step % RET_CHUNK == 0
    return pl.pallas_call(
        functools.partial(_ret_prompt_kernel, chunk=RET_CHUNK, n_chunks=rows_per_step // RET_CHUNK),
        grid=(b, t // rows_per_step),
        in_specs=[
            pl.BlockSpec((None, rows_per_step, n), lambda bi, i: (bi, i, 0)),
            pl.BlockSpec((RET_HEADS, RET_DV), lambda bi, i: (0, 0)),
        ],
        out_specs=[
            pl.BlockSpec((None, rows_per_step, RET_V), lambda bi, i: (bi, i, 0)),
            pl.BlockSpec((None, RET_HEADS, RET_DK, RET_DV), lambda bi, i: (bi, 0, 0, 0)),
        ],
        out_shape=[
            jax.ShapeDtypeStruct((b, t, RET_V), BF16),
            jax.ShapeDtypeStruct((b, RET_HEADS, RET_DK, RET_DV), F32),
        ],
        compiler_params=_params(("arbitrary", "arbitrary"), 40),
        name="ret_prompt",
    )(proj, gn_g)


def _ret_sample_kernel(p_ref, s0_ref, gn_ref, og_ref, s_ref, *, t):
    ii = lax.broadcasted_iota(I32, (t, t), 0)
    jj = lax.broadcasted_iota(I32, (t, t), 1)
    diff = (ii - jj).astype(F32)
    pos = lax.broadcasted_iota(I32, (t, 1), 0).astype(F32)
    pad = -t % 8
    for h in range(RET_HEADS):
        lg = _ret_log_gamma(h)
        decay = jnp.where(diff >= 0, jnp.exp(lg * jnp.maximum(diff, 0.0)), 0.0)
        q_decay = jnp.exp(lg * (pos + 1.0))
        k_decay = jnp.exp(lg * (t - 1.0 - pos))
        chunk_decay = math.exp(lg * t)
        q = p_ref[:, h * RET_DK:(h + 1) * RET_DK]
        k = p_ref[:, RET_QK + h * RET_DK:RET_QK + (h + 1) * RET_DK] * (RET_DK ** -0.5)
        v = p_ref[:, 2 * RET_QK + h * RET_DV:2 * RET_QK + (h + 1) * RET_DV]
        gate = p_ref[:, 2 * RET_QK + RET_V + h * RET_DV:2 * RET_QK + RET_V + (h + 1) * RET_DV]
        state = s0_ref[h]
        o = jnp.dot(q * q_decay, state, preferred_element_type=F32)
        for j in range(t):
            s_j = jnp.sum(q * k[j:j + 1, :], axis=-1, keepdims=True) * decay[:, j:j + 1]
            o = o + s_j * v[j:j + 1, :]
        kd = jnp.concatenate([k * k_decay, jnp.zeros((pad, RET_DK), F32)], axis=0)
        vp = jnp.concatenate([v, jnp.zeros((pad, RET_DV), F32)], axis=0)
        kv = lax.dot_general(kd, vp, (((0,), (0,)), ((), ())), preferred_element_type=F32)
        s_ref[h] = chunk_decay * state + kv
        oc = o - jnp.mean(o, axis=-1, keepdims=True)
        y = oc * lax.rsqrt(jnp.mean(oc * oc, axis=-1, keepdims=True) + EPS) * gn_ref[h:h + 1, :]
        og_ref[:, h * RET_DV:(h + 1) * RET_DV] = (y * _silu(gate)).astype(og_ref.dtype)


def _ret_sample(proj, state0, gn_g):
    b, t, n = proj.shape
    return pl.pallas_call(
        functools.partial(_ret_sample_kernel, t=t),
        grid=(b,),
        in_specs=[
            pl.BlockSpec((None, t, n), lambda bi: (bi, 0, 0)),
            pl.BlockSpec((None, RET_HEADS, RET_DK, RET_DV), lambda bi: (bi, 0, 0, 0)),
            pl.BlockSpec((RET_HEADS, RET_DV), lambda bi: (0, 0)),
        ],
        out_specs=[
            pl.BlockSpec((None, t, RET_V), lambda bi: (bi, 0, 0)),
            pl.BlockSpec((None, RET_HEADS, RET_DK, RET_DV), lambda bi: (bi, 0, 0, 0)),
        ],
        out_shape=[
            jax.ShapeDtypeStruct((b, t, RET_V), F32),
            jax.ShapeDtypeStruct((b, RET_HEADS, RET_DK, RET_DV), F32),
        ],
        compiler_params=_params(("arbitrary",), 40),
        name="ret_sample",
    )(proj, state0, gn_g)


def _moba_proj_kernel(x_ref, g_ref, sh_ref, sc_ref, w_ref, qg_ref, kg_ref, q_ref, k_ref, v_ref, gt_ref):
    h = _norm_mod(x_ref[...], g_ref[...], sh_ref[...], sc_ref[...]).astype(BF16)
    w = MOBA_W
    q = jnp.dot(h, w_ref[:, 0:w], preferred_element_type=F32)
    k = jnp.dot(h, w_ref[:, w:2 * w], preferred_element_type=F32)
    for hd in range(MOBA_HEADS):
        cs = slice(hd * MOBA_DH, (hd + 1) * MOBA_DH)
        q_ref[:, cs] = (_head_rms(q[:, cs], qg_ref[...]) * (MOBA_DH ** -0.5)).astype(q_ref.dtype)
        k_ref[:, cs] = _head_rms(k[:, cs], kg_ref[...])
    v_ref[...] = jnp.dot(h, w_ref[:, 2 * w:3 * w], preferred_element_type=F32)
    gt_ref[...] = jnp.dot(h, w_ref[:, 3 * w:4 * w], preferred_element_type=F32).astype(gt_ref.dtype)


def _moba_proj(x, norm_g, sh, sc, w_bf16, q_g, k_g, act_dtype, tm):
    g, r, d = x.shape
    n = w_bf16.shape[1]
    row_spec = pl.BlockSpec((None, tm, MOBA_W), lambda gi, i: (gi, i, 0))
    return pl.pallas_call(
        _moba_proj_kernel,
        grid=(g, r // tm),
        in_specs=[
            pl.BlockSpec((None, tm, d), lambda gi, i: (gi, i, 0)),
            pl.BlockSpec((1, d), lambda gi, i: (0, 0)),
            _mod_spec(sh, tm),
            _mod_spec(sc, tm),
            pl.BlockSpec((d, n), lambda gi, i: (0, 0)),
            pl.BlockSpec((1, MOBA_DH), lambda gi, i: (0, 0)),
            pl.BlockSpec((1, MOBA_DH), lambda gi, i: (0, 0)),
        ],
        out_specs=[row_spec, row_spec, row_spec, row_spec],
        out_shape=[
            jax.ShapeDtypeStruct((g, r, MOBA_W), act_dtype),
            jax.ShapeDtypeStruct((g, r, MOBA_W), F32),
            jax.ShapeDtypeStruct((g, r, MOBA_W), F32),
            jax.ShapeDtypeStruct((g, r, MOBA_W), act_dtype),
        ],
        compiler_params=_params(("arbitrary", "arbitrary"), 48),
        name="moba_proj",
    )(x, norm_g.reshape(1, d), sh, sc, w_bf16, q_g.reshape(1, MOBA_DH), k_g.reshape(1, MOBA_DH))


def _alibi_slope(head):
    hv = jnp.full((1, 1), head, I32).astype(F32)
    return jnp.exp2(-8.0 * (hv + 1.0) / MOBA_HEADS)


AUG_PEN0 = 8


def _moba_prompt_kernel(q_ref, k_ref, v_ref, g_ref, o_ref, ka_scr, vt_scr, qa_scr, km_scr, *, n_blocks):
    blk = MOBA_BLOCK
    dh = MOBA_DH
    t_len = n_blocks * blk
    slope = _alibi_slope(pl.program_id(1))
    pen_rows = km_scr.shape[0] // 4

    lane = lax.broadcasted_iota(I32, (blk, dh), 1)
    k_off = lax.broadcasted_iota(I32, (blk, dh), 0).astype(F32)
    km_scr[...] = jnp.zeros_like(km_scr)
    for n in range(n_blocks):
        rows = slice(n * blk, (n + 1) * blk)
        kn = k_ref[rows, :]
        ka_scr[rows, 0:dh] = kn.astype(BF16)
        aug = jnp.where(lane < 2, 1.0,
                        jnp.where(lane == 2, slope * float(blk * n),
                                  jnp.where(lane == 3, slope * k_off,
                                            jnp.where(lane == AUG_PEN0 + n, 1.0, 0.0))))
        ka_scr[rows, dh:2 * dh] = aug.astype(BF16)
        vt_scr[:, rows] = v_ref[rows, :].T.astype(BF16)
        mean = jnp.mean(kn, axis=0, keepdims=True)
        hi = mean.astype(BF16).astype(F32)
        mid = (mean - hi).astype(BF16).astype(F32)
        lo = (mean - hi - mid).astype(BF16).astype(F32)
        km_scr[n:n + 1, :] = hi
        km_scr[pen_rows + n:pen_rows + n + 1, :] = mid
        km_scr[2 * pen_rows + n:2 * pen_rows + n + 1, :] = lo
        qa_scr[0:dh, rows] = q_ref[rows, :].astype(F32).T.astype(BF16)

    parts = jnp.dot(km_scr[...].astype(BF16), qa_scr[0:dh, :], preferred_element_type=F32)
    gs = parts[0:pen_rows] + parts[pen_rows:2 * pen_rows] + parts[2 * pen_rows:3 * pen_rows]
    nid = lax.broadcasted_iota(I32, (pen_rows, t_len), 0)
    q_pos = lax.broadcasted_iota(I32, (pen_rows, t_len), 1)
    own = lax.shift_right_logical(q_pos, int(math.log2(blk)))
    past = nid < own
    attended = nid == own
    gs = jnp.where(past, gs, NEG)
    for _ in range(min(MOBA_TOPK, n_blocks)):
        mx = jnp.max(gs, axis=0, keepdims=True)
        idx = jnp.min(jnp.where(gs == mx, nid, pen_rows), axis=0, keepdims=True)
        pick = nid == idx
        attended = attended | (pick & past)
        gs = jnp.where(pick, BELOW_NEG, gs)
    pen = jnp.where(attended, 0.0, NEG)
    r8 = lax.broadcasted_iota(I32, (8, t_len), 0)
    q8 = lax.broadcasted_iota(I32, (8, t_len), 1)
    own8 = lax.shift_right_logical(q8, int(math.log2(blk)))
    bias = jnp.where(r8 == 0, -slope * (own8 * blk).astype(F32),
                     jnp.where(r8 == 1, -slope * (q8 - own8 * blk).astype(F32), jnp.where(r8 < 4, 1.0, 0.0)))
    extra = jnp.concatenate([bias, pen, jnp.zeros((dh - 8 - pen_rows, t_len), F32)], axis=0)
    qa_scr[dh:2 * dh, :] = extra.astype(BF16)

    def query_block(own):
        n_keys = (own + 1) * blk
        cols = slice(own * blk, n_keys)
        s = jnp.dot(ka_scr[0:n_keys, :], qa_scr[:, cols], preferred_element_type=F32)
        k_idx = lax.broadcasted_iota(I32, (blk, blk), 0)
        q_idx = lax.broadcasted_iota(I32, (blk, blk), 1)
        s_own = jnp.where(q_idx >= k_idx, s[own * blk:n_keys], NEG)
        m = jnp.max(s_own, axis=0, keepdims=True)
        if own > 0:
            s_past = s[0:own * blk]
            m = jnp.maximum(m, jnp.max(s_past, axis=0, keepdims=True))
        p = jnp.exp(s_own - m)
        l = jnp.sum(p, axis=0, keepdims=True)
        p = p.astype(BF16)
        if own > 0:
            p_past = jnp.exp(s_past - m)
            l = l + jnp.sum(p_past, axis=0, keepdims=True)
            p = jnp.concatenate([p_past.astype(BF16), p], axis=0)
        acc = jnp.dot(vt_scr[:, 0:n_keys], p, preferred_element_type=F32)
        o = (acc / l).T
        o_ref[cols, :] = (o * _silu(g_ref[cols, :].astype(F32))).astype(o_ref.dtype)

    for own in range(n_blocks):
        query_block(own)


def _moba_prompt(q, k, v, gate):
    b, t, w = q.shape
    assert t % MOBA_BLOCK == 0
    n_blocks = t // MOBA_BLOCK
    pen_rows = -(-n_blocks // 8) * 8
    assert AUG_PEN0 + pen_rows <= MOBA_DH
    assert MOBA_BLOCK & (MOBA_BLOCK - 1) == 0
    spec = pl.BlockSpec((None, t, MOBA_DH), lambda bi, h: (bi, 0, h))
    return pl.pallas_call(
        functools.partial(_moba_prompt_kernel, n_blocks=n_blocks),
        grid=(b, MOBA_HEADS),
        in_specs=[spec, spec, spec, spec],
        out_specs=spec,
        out_shape=jax.ShapeDtypeStruct((b, t, w), BF16),
        scratch_shapes=[
            pltpu.VMEM((t, 2 * MOBA_DH), BF16),
            pltpu.VMEM((MOBA_DH, t), BF16),
            pltpu.VMEM((2 * MOBA_DH, t), BF16),
            pltpu.VMEM((4 * pen_rows, MOBA_DH), F32),
        ],
        compiler_params=_params(("arbitrary", "arbitrary"), 48),
        name="moba_prompt",
    )(q, k, v, gate)


def _kmean_kernel(pt_ref, *refs, blocks_per_step, pages_per_block):
    del pt_ref
    o_ref = refs[-1]
    for j in range(blocks_per_step):
        tot = None
        for p in range(pages_per_block):
            part = jnp.sum(refs[j * pages_per_block + p][...], axis=0)
            tot = part if tot is None else tot + part
        o_ref[j] = tot / float(pages_per_block * PAGE_SIZE)


def _kmean_pages(cache_pages, page_table):
    b, n_pages = page_table.shape
    hd = cache_pages.shape[2:]
    ppb = MOBA_BLOCK // PAGE_SIZE
    n_blocks = n_pages // ppb
    bps = 8
    assert n_pages % ppb == 0 and n_blocks % bps == 0

    def page_spec(p):
        return pl.BlockSpec((None, PAGE_SIZE, *hd), lambda bi, j, pt: (pt[bi, j * bps * ppb + p], 0, 0, 0))

    return pl.pallas_call(
        functools.partial(_kmean_kernel, blocks_per_step=bps, pages_per_block=ppb),
        grid_spec=pltpu.PrefetchScalarGridSpec(
            num_scalar_prefetch=1,
            grid=(b, n_blocks // bps),
            in_specs=[page_spec(p) for p in range(bps * ppb)],
            out_specs=pl.BlockSpec((None, bps, *hd), lambda bi, j, pt: (bi, j, 0, 0)),
        ),
        out_shape=jax.ShapeDtypeStruct((b, n_blocks, *hd), F32),
        compiler_params=_params(("arbitrary", "arbitrary"), 40),
        name="kmean_pages",
    )(page_table, *([cache_pages] * (bps * ppb)))


def _moba_select_kernel(q_ref, km_ref, sel_ref, *, t, topk):
    n_blocks = km_ref.shape[0]
    rows = sel_ref.shape[0]
    rid = lax.broadcasted_iota(I32, (rows, 128), 0)
    lid = lax.broadcasted_iota(I32, (rows, 128), 1)
    nid = lax.broadcasted_iota(I32, (n_blocks, 1), 0)
    out = jnp.zeros((rows, 128), I32)
    for h in range(MOBA_HEADS):
        cs = slice(h * MOBA_DH, (h + 1) * MOBA_DH)
        km = km_ref[:, h, :]
        for i in range(t):
            gs = jnp.sum(km * q_ref[i:i + 1, cs], axis=-1, keepdims=True)
            for r in range(topk):
                mx = jnp.max(gs, axis=0, keepdims=True)
                idx = jnp.min(jnp.where(gs == mx, nid, n_blocks), axis=0, keepdims=True)
                out = jnp.where((rid == h * t + i) & (lid == r), idx, out)
                gs = jnp.where(nid == idx, BELOW_NEG, gs)
    sel_ref[...] = out


def _moba_select(q, kmean):
    b, t, w = q.shape
    n_blocks = kmean.shape[1]
    rows = MOBA_HEADS * t
    return pl.pallas_call(
        functools.partial(_moba_select_kernel, t=t, topk=MOBA_TOPK),
        grid=(b,),
        in_specs=[
            pl.BlockSpec((None, t, w), lambda bi: (bi, 0, 0)),
            pl.BlockSpec((None, n_blocks, MOBA_HEADS, MOBA_DH), lambda bi: (bi, 0, 0, 0)),
        ],
        out_specs=pl.BlockSpec((None, rows, 128), lambda bi: (bi, 0, 0)),
        out_shape=jax.ShapeDtypeStruct((b, rows, 128), I32),
        compiler_params=_params(("arbitrary",), 32),
        name="moba_select",
    )(q, kmean)


def _moba_sample_kernel(pt_ref, sel_ref, q_ref, kn_ref, vn_ref, g_ref, ck_ref, cv_ref, o_ref, kbuf, vbuf, sems,
                        *, t, past_len, topk, pages_per_block):
    b = pl.program_id(0)
    head = pl.program_id(1)
    n_b = pl.num_programs(0)
    n_h = pl.num_programs(1)
    step = b * n_h + head

    def page_copies(bi, hi, slot):
        copies = []
        for i in range(t):
            for s in range(topk):
                blk_id = sel_ref[bi, (hi * t + i) * topk + s]
                for p in range(pages_per_block):
                    page = pt_ref[bi, blk_id * pages_per_block + p]
                    j = (i * topk + s) * pages_per_block + p
                    copies.append(pltpu.make_async_copy(ck_ref.at[page, :, hi, :], kbuf.at[slot, j], sems.at[0, slot]))
                    copies.append(pltpu.make_async_copy(cv_ref.at[page, :, hi, :], vbuf.at[slot, j], sems.at[1, slot]))
        return copies

    @pl.when(step == 0)
    def _():
        for c in page_copies(b, head, 0):
            c.start()

    @pl.when(step + 1 < n_b * n_h)
    def _():
        wrap = head + 1 == n_h
        for c in page_copies(jnp.where(wrap, b + 1, b), jnp.where(wrap, 0, head + 1), (step + 1) % 2):
            c.start()

    slot = step % 2
    for c in page_copies(b, head, slot):
        c.wait()

    slope = _alibi_slope(head)
    q = q_ref[...]
    kn = kn_ref[...]
    vn = vn_ref[...]
    gate = g_ref[...]
    off = lax.broadcasted_iota(I32, (1, PAGE_SIZE), 1)
    jn = lax.broadcasted_iota(I32, (t, 1), 0)
    pages_per_query = topk * pages_per_block
    for i in range(t):
        qi = q[i:i + 1, :]
        d_own = i - jn
        s_own = jnp.sum(kn * qi, axis=-1, keepdims=True) - slope * d_own.astype(F32)
        s_own = jnp.where(d_own >= 0, s_own, NEG)
        k_sel = kbuf[slot, i * pages_per_query:(i + 1) * pages_per_query].reshape(pages_per_query * PAGE_SIZE, MOBA_DH)
        v_sel = vbuf[slot, i * pages_per_query:(i + 1) * pages_per_query].reshape(pages_per_query * PAGE_SIZE, MOBA_DH)
        dist = jnp.concatenate(
            [(past_len + i) - (sel_ref[b, (head * t + i) * topk + s] * MOBA_BLOCK + p * PAGE_SIZE + off)
             for s in range(topk) for p in range(pages_per_block)], axis=1)
        s_sel = lax.dot_general(qi, k_sel, NT_DIMS, preferred_element_type=F32) - slope * dist.astype(F32)
        s_sel = jnp.where(dist >= 0, s_sel, NEG)
        m = jnp.maximum(jnp.max(s_own, axis=0, keepdims=True), jnp.max(s_sel, axis=1, keepdims=True))
        p_own = jnp.exp(s_own - m)
        p_sel = jnp.exp(s_sel - m)
        l = jnp.sum(p_own, axis=0, keepdims=True) + jnp.sum(p_sel, axis=1, keepdims=True)
        o = jnp.sum(p_own * vn, axis=0, keepdims=True) + jnp.dot(p_sel, v_sel, preferred_element_type=F32)
        o_ref[i:i + 1, :] = (o / l) * _silu(gate[i:i + 1, :])


def _moba_sample(q, k_new, v_new, gate, cache_k_pages, cache_v_pages, page_table, sel, past_len):
    b, t, w = q.shape
    ppb = MOBA_BLOCK // PAGE_SIZE
    topk = MOBA_TOPK
    row_spec = pl.BlockSpec((None, t, MOBA_DH), lambda bi, h, pt, sl: (bi, 0, h))
    hbm_spec = pl.BlockSpec(memory_space=pl.ANY)
    n_sel = t * topk * ppb
    return pl.pallas_call(
        functools.partial(_moba_sample_kernel, t=t, past_len=past_len, topk=topk, pages_per_block=ppb),
        grid_spec=pltpu.PrefetchScalarGridSpec(
            num_scalar_prefetch=2,
            grid=(b, MOBA_HEADS),
            in_specs=[row_spec, row_spec, row_spec, row_spec, hbm_spec, hbm_spec],
            out_specs=row_spec,
            scratch_shapes=[
                pltpu.VMEM((2, n_sel, PAGE_SIZE, MOBA_DH), F32),
                pltpu.VMEM((2, n_sel, PAGE_SIZE, MOBA_DH), F32),
                pltpu.SemaphoreType.DMA((2, 2)),
            ],
        ),
        out_shape=jax.ShapeDtypeStruct((b, t, w), F32),
        compiler_params=_params(("arbitrary", "arbitrary"), 40),
        name="moba_sample",
    )(page_table, sel, q, k_new, v_new, gate, cache_k_pages, cache_v_pages)


def kernel(x_prompt, x_sample, c_prompt, c_sample, state_ret, cache_k, cache_v, page_table, norm_g, w_ada, b_ada,
           w_ret_in, ret_gn_g, w_ret_out, w_moba_in, moba_q_g, moba_k_g, w_moba_out):
    bp, tp, d = x_prompt.shape
    bs, ts, _ = x_sample.shape
    n_pages = page_table.shape[1]
    past_len = n_pages * PAGE_SIZE
    assert past_len % MOBA_BLOCK == 0 and ts <= MOBA_BLOCK and past_len // MOBA_BLOCK >= MOBA_TOPK
    assert w_ada.shape[0] == 2 and w_ret_in.shape[0] == 1 and w_moba_in.shape[0] == 1

    n_c = bp + bs
    c_all = jnp.concatenate([c_prompt, c_sample, jnp.zeros((-n_c % 8, d), F32)], axis=0)
    mod = _ada(c_all, w_ada, b_ada)

    def group_mod(layer):
        mp = mod[layer, :bp].reshape(bp, 1, 3 * d)
        ms = jnp.repeat(mod[layer, bp:n_c], ts, axis=0).reshape(1, bs * ts, 3 * d)
        return [(m[..., :d], m[..., d:2 * d], m[..., 2 * d:]) for m in (mp, ms)]

    xs = x_sample.reshape(1, bs * ts, d)
    w_ret_in_b = w_ret_in[0].astype(BF16)
    w_ret_out_b = w_ret_out[0].astype(BF16)
    w_moba_in_b = w_moba_in[0].astype(BF16)
    w_moba_out_b = w_moba_out[0].astype(BF16)

    (sh_p, sc_p, gt_p), (sh_s, sc_s, gt_s) = group_mod(0)
    proj_p = _ret_proj(x_prompt, norm_g[0], sh_p, sc_p, w_ret_in_b, BF16, tm=256)
    og_p, ret_p = _ret_prompt(proj_p, ret_gn_g[0], rows_per_step=256)
    y_p = _out_proj(og_p, w_ret_out_b, x_prompt, gt_p, tm=512)

    proj_s = _ret_proj(xs, norm_g[0], sh_s, sc_s, w_ret_in_b, F32, tm=bs * ts).reshape(bs, ts, -1)
    og_s, ret_s = _ret_sample(proj_s, state_ret[0], ret_gn_g[0])
    y_s = _out_proj(og_s.reshape(1, bs * ts, RET_V).astype(BF16), w_ret_out_b, xs, gt_s, tm=bs * ts)

    (sh_p, sc_p, gt_p), (sh_s, sc_s, gt_s) = group_mod(1)
    q_p, k_p, v_p, g_p = _moba_proj(y_p, norm_g[1], sh_p, sc_p, w_moba_in_b, moba_q_g[0], moba_k_g[0], BF16, tm=256)
    oa_p = _moba_prompt(q_p, k_p, v_p, g_p)
    y_p = _out_proj(oa_p, w_moba_out_b, y_p, gt_p, tm=512)

    q_s, k_s, v_s, g_s = _moba_proj(y_s, norm_g[1], sh_s, sc_s, w_moba_in_b, moba_q_g[0], moba_k_g[0], F32,
                                    tm=bs * ts)
    q_s, k_s, v_s, g_s = (a.reshape(bs, ts, MOBA_W) for a in (q_s, k_s, v_s, g_s))
    ck = cache_k[0]
    cv = cache_v[0]
    kmean = _kmean_pages(ck, page_table)
    sel = _moba_select(q_s, kmean)[:, :, :MOBA_TOPK].reshape(bs, MOBA_HEADS * ts * MOBA_TOPK)
    oa_s = _moba_sample(q_s, k_s, v_s, g_s, ck, cv, page_table, sel, past_len)
    y_s = _out_proj(oa_s.reshape(1, bs * ts, MOBA_W).astype(BF16), w_moba_out_b, y_s, gt_s, tm=bs * ts)

    hd = (MOBA_HEADS, MOBA_DH)
    return (y_p, y_s.reshape(bs, ts, d), ret_p[None], ret_s[None],
            k_p.reshape(1, bp, tp, *hd), v_p.reshape(1, bp, tp, *hd),
            k_s.reshape(1, bs, ts, *hd), v_s.reshape(1, bs, ts, *hd))
```

```python
import functools
import math

import jax
import jax.numpy as jnp
from jax import lax
from jax.experimental import pallas as pl
from jax.experimental.pallas import tpu as pltpu

F32 = jnp.float32
BF16 = jnp.bfloat16
I32 = jnp.int32

RET_HEADS = 4
RET_DK = 256
RET_DV = 512
RET_CHUNK = 128
RET_QK = RET_HEADS * RET_DK
RET_V = RET_HEADS * RET_DV
MOBA_HEADS = 8
MOBA_DH = 128
MOBA_W = MOBA_HEADS * MOBA_DH
MOBA_BLOCK = 256
MOBA_TOPK = 3
PAGE_SIZE = 128
EPS = 1e-6
NEG = -1e30
BELOW_NEG = -3e38

V7X_VMEM_BYTES = 64 * 1024 * 1024
NT_DIMS = (((1,), (1,)), ((), ()))


def _params(semantics, vmem_mb):
    assert vmem_mb * 1024 * 1024 < V7X_VMEM_BYTES
    return pltpu.CompilerParams(dimension_semantics=semantics, vmem_limit_bytes=vmem_mb * 1024 * 1024)


def _silu(x):
    return x * jax.nn.sigmoid(x)


def _norm_mod(x, g, sh, sc):
    ms = jnp.mean(x * x, axis=-1, keepdims=True)
    y = x * lax.rsqrt(ms + EPS) * g
    return y * (1.0 + sc) + sh


def _head_rms(x, g):
    ms = jnp.mean(x * x, axis=-1, keepdims=True)
    return x * lax.rsqrt(ms + EPS) * g


def _ret_log_gamma(h):
    return math.log1p(-(2.0 ** (-5.0 - h)))


def _ada_kernel(c_ref, w_ref, b_ref, o_ref):
    s = _silu(c_ref[...]).astype(BF16)
    o_ref[...] = jnp.dot(s, w_ref[...].astype(BF16), preferred_element_type=F32) + b_ref[...]


def _ada(c_all, w_ada, b_ada):
    depth, d, n = w_ada.shape
    m = c_all.shape[0]
    tn = 512
    return pl.pallas_call(
        _ada_kernel,
        grid=(depth, n // tn),
        in_specs=[
            pl.BlockSpec((m, d), lambda l, j: (0, 0)),
            pl.BlockSpec((None, d, tn), lambda l, j: (l, 0, j)),
            pl.BlockSpec((None, 1, tn), lambda l, j: (l, 0, j)),
        ],
        out_specs=pl.BlockSpec((None, m, tn), lambda l, j: (l, 0, j)),
        out_shape=jax.ShapeDtypeStruct((depth, m, n), F32),
        compiler_params=_params(("arbitrary", "arbitrary"), 32),
        name="ada",
    )(c_all, w_ada, b_ada.reshape(depth, 1, n))


def _mod_spec(mod, tm):
    d = mod.shape[-1]
    if mod.shape[1] == 1:
        return pl.BlockSpec((None, 1, d), lambda g, i, *_: (g, 0, 0))
    return pl.BlockSpec((None, tm, d), lambda g, i, *_: (g, i, 0))


def _weight_spec(shape, index_map, resident):
    if resident:
        return pl.BlockSpec(shape, index_map, pipeline_mode=pl.Buffered(1))
    return pl.BlockSpec(shape, index_map)


def _cast_weight(w_ref, wb_scr, col_chunk):
    for c in range(w_ref.shape[1] // col_chunk):
        cs = slice(c * col_chunk, (c + 1) * col_chunk)
        wb_scr[:, cs] = w_ref[:, cs].astype(BF16)


def _ret_proj_kernel(x_ref, g_ref, sh_ref, sc_ref, w_ref, o_ref, wb_scr, h_scr, *, col_chunk, resident):
    if resident:
        @pl.when((pl.program_id(0) == 0) & (pl.program_id(1) == 0))
        def _():
            _cast_weight(w_ref, wb_scr, col_chunk)
    else:
        _cast_weight(w_ref, wb_scr, col_chunk)

    @pl.when(pl.program_id(2) == 0)
    def _():
        h_scr[...] = _norm_mod(x_ref[...], g_ref[...], sh_ref[...], sc_ref[...]).astype(BF16)

    h = h_scr[...]
    for c in range(w_ref.shape[1] // col_chunk):
        cs = slice(c * col_chunk, (c + 1) * col_chunk)
        o_ref[:, cs] = jnp.dot(h, wb_scr[:, cs], preferred_element_type=F32).astype(o_ref.dtype)


def _ret_proj(x, norm_g, sh, sc, w, out_dtype, tm, tn):
    g, r, d = x.shape
    n = w.shape[1]
    resident = tn == n
    return pl.pallas_call(
        functools.partial(_ret_proj_kernel, col_chunk=512, resident=resident),
        grid=(g, r // tm, n // tn),
        in_specs=[
            pl.BlockSpec((None, tm, d), lambda gi, i, j: (gi, i, 0)),
            pl.BlockSpec((1, d), lambda gi, i, j: (0, 0)),
            _mod_spec(sh, tm),
            _mod_spec(sc, tm),
            _weight_spec((d, tn), lambda gi, i, j: (0, j), resident),
        ],
        out_specs=pl.BlockSpec((None, tm, tn), lambda gi, i, j: (gi, i, j)),
        out_shape=jax.ShapeDtypeStruct((g, r, n), out_dtype),
        scratch_shapes=[pltpu.VMEM((d, tn), BF16), pltpu.VMEM((tm, d), BF16)],
        compiler_params=_params(("arbitrary", "arbitrary", "arbitrary"), 58),
        name="ret_proj",
    )(x, norm_g.reshape(1, d), sh, sc, w)


def _out_proj_kernel(a_ref, w_ref, x_ref, gt_ref, o_ref, wb_scr):
    @pl.when((pl.program_id(0) == 0) & (pl.program_id(1) == 0))
    def _():
        _cast_weight(w_ref, wb_scr, 512)

    o_ref[...] = x_ref[...] + gt_ref[...] * jnp.dot(a_ref[...], wb_scr[...], preferred_element_type=F32)


def _out_proj(a, w, x, gate, tm):
    g, r, k = a.shape
    d = w.shape[1]
    return pl.pallas_call(
        _out_proj_kernel,
        grid=(g, r // tm),
        in_specs=[
            pl.BlockSpec((None, tm, k), lambda gi, i: (gi, i, 0)),
            _weight_spec((k, d), lambda gi, i: (0, 0), True),
            pl.BlockSpec((None, tm, d), lambda gi, i: (gi, i, 0)),
            _mod_spec(gate, tm),
        ],
        out_specs=pl.BlockSpec((None, tm, d), lambda gi, i: (gi, i, 0)),
        out_shape=jax.ShapeDtypeStruct((g, r, d), F32),
        scratch_shapes=[pltpu.VMEM((k, d), BF16)],
        compiler_params=_params(("arbitrary", "arbitrary"), 48),
        name="out_proj",
    )(a, w, x, gate)


def _ret_prompt_kernel(p_ref, gn_ref, og_ref, s_ref, *, chunk, n_chunks):
    @pl.when(pl.program_id(1) == 0)
    def _():
        s_ref[...] = jnp.zeros_like(s_ref)

    ii = lax.broadcasted_iota(I32, (chunk, chunk), 0)
    jj = lax.broadcasted_iota(I32, (chunk, chunk), 1)
    diff = (ii - jj).astype(F32)
    pos = lax.broadcasted_iota(I32, (chunk, 1), 0).astype(F32)
    for h in range(RET_HEADS):
        lg = _ret_log_gamma(h)
        decay = jnp.where(diff >= 0, jnp.exp(lg * jnp.maximum(diff, 0.0)), 0.0)
        q_decay = jnp.exp(lg * (pos + 1.0))
        k_decay = jnp.exp(lg * (chunk - 1.0 - pos))
        chunk_decay = math.exp(lg * chunk)
        for c in range(n_chunks):
            rows = slice(c * chunk, (c + 1) * chunk)
            q = p_ref[rows, h * RET_DK:(h + 1) * RET_DK]
            k = p_ref[rows, RET_QK + h * RET_DK:RET_QK + (h + 1) * RET_DK].astype(F32) * (RET_DK ** -0.5)
            v = p_ref[rows, 2 * RET_QK + h * RET_DV:2 * RET_QK + (h + 1) * RET_DV]
            gate = p_ref[rows, 2 * RET_QK + RET_V + h * RET_DV:2 * RET_QK + RET_V + (h + 1) * RET_DV].astype(F32)
            state = s_ref[h]
            scores = lax.dot_general(q, k.astype(BF16), NT_DIMS, preferred_element_type=F32) * decay
            inner = jnp.dot(scores.astype(BF16), v, preferred_element_type=F32)
            cross = jnp.dot((q.astype(F32) * q_decay).astype(BF16), state.astype(BF16), preferred_element_type=F32)
            kv = lax.dot_general((k * k_decay).astype(BF16), v, (((0,), (0,)), ((), ())), preferred_element_type=F32)
            s_ref[h] = chunk_decay * state + kv
            o = inner + cross
            oc = o - jnp.mean(o, axis=-1, keepdims=True)
            y = oc * lax.rsqrt(jnp.mean(oc * oc, axis=-1, keepdims=True) + EPS) * gn_ref[h:h + 1, :]
            og_ref[rows, h * RET_DV:(h + 1) * RET_DV] = (y * _silu(gate)).astype(og_ref.dtype)


def _ret_prompt(proj, gn_g, rows_per_step):
    b, t, n = proj.shape
    assert t % RET_CHUNK == 0 and rows_per_step % RET_CHUNK == 0
    return pl.pallas_call(
        functools.partial(_ret_prompt_kernel, chunk=RET_CHUNK, n_chunks=rows_per_step // RET_CHUNK),
        grid=(b, t // rows_per_step),
        in_specs=[
            pl.BlockSpec((None, rows_per_step, n), lambda bi, i: (bi, i, 0)),
            pl.BlockSpec((RET_HEADS, RET_DV), lambda bi, i: (0, 0)),
        ],
        out_specs=[
            pl.BlockSpec((None, rows_per_step, RET_V), lambda bi, i: (bi, i, 0)),
            pl.BlockSpec((None, RET_HEADS, RET_DK, RET_DV), lambda bi, i: (bi, 0, 0, 0)),
        ],
        out_shape=[
            jax.ShapeDtypeStruct((b, t, RET_V), BF16),
            jax.ShapeDtypeStruct((b, RET_HEADS, RET_DK, RET_DV), F32),
        ],
        compiler_params=_params(("arbitrary", "arbitrary"), 40),
        name="ret_prompt",
    )(proj, gn_g)


def _ret_sample_kernel(p_ref, s0_ref, gn_ref, og_ref, s_ref, *, t):
    ii = lax.broadcasted_iota(I32, (t, t), 0)
    jj = lax.broadcasted_iota(I32, (t, t), 1)
    diff = (ii - jj).astype(F32)
    pos = lax.broadcasted_iota(I32, (t, 1), 0).astype(F32)
    pad = -t % 8
    for h in range(RET_HEADS):
        lg = _ret_log_gamma(h)
        decay = jnp.where(diff >= 0, jnp.exp(lg * jnp.maximum(diff, 0.0)), 0.0)
        q_decay = jnp.exp(lg * (pos + 1.0))
        k_decay = jnp.exp(lg * (t - 1.0 - pos))
        chunk_decay = math.exp(lg * t)
        q = p_ref[:, h * RET_DK:(h + 1) * RET_DK]
        k = p_ref[:, RET_QK + h * RET_DK:RET_QK + (h + 1) * RET_DK] * (RET_DK ** -0.5)
        v = p_ref[:, 2 * RET_QK + h * RET_DV:2 * RET_QK + (h + 1) * RET_DV]
        gate = p_ref[:, 2 * RET_QK + RET_V + h * RET_DV:2 * RET_QK + RET_V + (h + 1) * RET_DV]
        state = s0_ref[h]
        o = jnp.dot(q * q_decay, state, preferred_element_type=F32)
        for j in range(t):
            s_j = jnp.sum(q * k[j:j + 1, :], axis=-1, keepdims=True) * decay[:, j:j + 1]
            o = o + s_j * v[j:j + 1, :]
        kd = jnp.concatenate([k * k_decay, jnp.zeros((pad, RET_DK), F32)], axis=0)
        vp = jnp.concatenate([v, jnp.zeros((pad, RET_DV), F32)], axis=0)
        kv = lax.dot_general(kd, vp, (((0,), (0,)), ((), ())), preferred_element_type=F32)
        s_ref[h] = chunk_decay * state + kv
        oc = o - jnp.mean(o, axis=-1, keepdims=True)
        y = oc * lax.rsqrt(jnp.mean(oc * oc, axis=-1, keepdims=True) + EPS) * gn_ref[h:h + 1, :]
        og_ref[:, h * RET_DV:(h + 1) * RET_DV] = (y * _silu(gate)).astype(og_ref.dtype)


def _ret_sample(proj, state0, gn_g):
    b, t, n = proj.shape
    return pl.pallas_call(
        functools.partial(_ret_sample_kernel, t=t),
        grid=(b,),
        in_specs=[
            pl.BlockSpec((None, t, n), lambda bi: (bi, 0, 0)),
            pl.BlockSpec((None, RET_HEADS, RET_DK, RET_DV), lambda bi: (bi, 0, 0, 0)),
            pl.BlockSpec((RET_HEADS, RET_DV), lambda bi: (0, 0)),
        ],
        out_specs=[
            pl.BlockSpec((None, t, RET_V), lambda bi: (bi, 0, 0)),
            pl.BlockSpec((None, RET_HEADS, RET_DK, RET_DV), lambda bi: (bi, 0, 0, 0)),
        ],
        out_shape=[
            jax.ShapeDtypeStruct((b, t, RET_V), F32),
            jax.ShapeDtypeStruct((b, RET_HEADS, RET_DK, RET_DV), F32),
        ],
        compiler_params=_params(("arbitrary",), 40),
        name="ret_sample",
    )(proj, state0, gn_g)


def _moba_proj_kernel(x_ref, g_ref, sh_ref, sc_ref, w_ref, qg_ref, kg_ref, q_ref, k_ref, v_ref, gt_ref, wb_scr):
    @pl.when((pl.program_id(0) == 0) & (pl.program_id(1) == 0))
    def _():
        _cast_weight(w_ref, wb_scr, 512)

    h = _norm_mod(x_ref[...], g_ref[...], sh_ref[...], sc_ref[...]).astype(BF16)
    w = MOBA_W
    q = jnp.dot(h, wb_scr[:, 0:w], preferred_element_type=F32)
    k = jnp.dot(h, wb_scr[:, w:2 * w], preferred_element_type=F32)
    for hd in range(MOBA_HEADS):
        cs = slice(hd * MOBA_DH, (hd + 1) * MOBA_DH)
        q_ref[:, cs] = (_head_rms(q[:, cs], qg_ref[...]) * (MOBA_DH ** -0.5)).astype(q_ref.dtype)
        k_ref[:, cs] = _head_rms(k[:, cs], kg_ref[...])
    v_ref[...] = jnp.dot(h, wb_scr[:, 2 * w:3 * w], preferred_element_type=F32)
    gt_ref[...] = jnp.dot(h, wb_scr[:, 3 * w:4 * w], preferred_element_type=F32).astype(gt_ref.dtype)


def _moba_proj(x, norm_g, sh, sc, w, q_g, k_g, act_dtype, tm):
    g, r, d = x.shape
    n = w.shape[1]
    row_spec = pl.BlockSpec((None, tm, MOBA_W), lambda gi, i: (gi, i, 0))
    return pl.pallas_call(
        _moba_proj_kernel,
        grid=(g, r // tm),
        in_specs=[
            pl.BlockSpec((None, tm, d), lambda gi, i: (gi, i, 0)),
            pl.BlockSpec((1, d), lambda gi, i: (0, 0)),
            _mod_spec(sh, tm),
            _mod_spec(sc, tm),
            _weight_spec((d, n), lambda gi, i: (0, 0), True),
            pl.BlockSpec((1, MOBA_DH), lambda gi, i: (0, 0)),
            pl.BlockSpec((1, MOBA_DH), lambda gi, i: (0, 0)),
        ],
        out_specs=[row_spec, row_spec, row_spec, row_spec],
        out_shape=[
            jax.ShapeDtypeStruct((g, r, MOBA_W), act_dtype),
            jax.ShapeDtypeStruct((g, r, MOBA_W), F32),
            jax.ShapeDtypeStruct((g, r, MOBA_W), F32),
            jax.ShapeDtypeStruct((g, r, MOBA_W), act_dtype),
        ],
        scratch_shapes=[pltpu.VMEM((d, n), BF16)],
        compiler_params=_params(("arbitrary", "arbitrary"), 56),
        name="moba_proj",
    )(x, norm_g.reshape(1, d), sh, sc, w, q_g.reshape(1, MOBA_DH), k_g.reshape(1, MOBA_DH))


def _alibi_slope(head):
    hv = jnp.full((1, 1), head, I32).astype(F32)
    return jnp.exp2(-8.0 * (hv + 1.0) / MOBA_HEADS)


AUG_PEN0 = 8


def _moba_prompt_kernel(q_ref, k_ref, v_ref, g_ref, o_ref, ka_scr, vt_scr, qa_scr, km_scr, *, n_blocks):
    blk = MOBA_BLOCK
    dh = MOBA_DH
    t_len = n_blocks * blk
    slope = _alibi_slope(pl.program_id(1))
    pen_rows = km_scr.shape[0] // 4

    lane = lax.broadcasted_iota(I32, (blk, dh), 1)
    k_off = lax.broadcasted_iota(I32, (blk, dh), 0).astype(F32)
    km_scr[...] = jnp.zeros_like(km_scr)
    for n in range(n_blocks):
        rows = slice(n * blk, (n + 1) * blk)
        kn = k_ref[rows, :]
        ka_scr[rows, 0:dh] = kn.astype(BF16)
        aug = jnp.where(lane < 2, 1.0,
                        jnp.where(lane == 2, slope * float(blk * n),
                                  jnp.where(lane == 3, slope * k_off,
                                            jnp.where(lane == AUG_PEN0 + n, 1.0, 0.0))))
        ka_scr[rows, dh:2 * dh] = aug.astype(BF16)
        vt_scr[:, rows] = v_ref[rows, :].T.astype(BF16)
        mean = jnp.mean(kn, axis=0, keepdims=True)
        hi = mean.astype(BF16).astype(F32)
        mid = (mean - hi).astype(BF16).astype(F32)
        lo = (mean - hi - mid).astype(BF16).astype(F32)
        km_scr[n:n + 1, :] = hi
        km_scr[pen_rows + n:pen_rows + n + 1, :] = mid
        km_scr[2 * pen_rows + n:2 * pen_rows + n + 1, :] = lo
        qa_scr[0:dh, rows] = q_ref[rows, :].astype(F32).T.astype(BF16)

    parts = jnp.dot(km_scr[...].astype(BF16), qa_scr[0:dh, :], preferred_element_type=F32)
    gs = parts[0:pen_rows] + parts[pen_rows:2 * pen_rows] + parts[2 * pen_rows:3 * pen_rows]
    nid = lax.broadcasted_iota(I32, (pen_rows, t_len), 0)
    q_pos = lax.broadcasted_iota(I32, (pen_rows, t_len), 1)
    own = lax.shift_right_logical(q_pos, int(math.log2(blk)))
    past = nid < own
    attended = nid == own
    gs = jnp.where(past, gs, NEG)
    for _ in range(min(MOBA_TOPK, n_blocks)):
        mx = jnp.max(gs, axis=0, keepdims=True)
        idx = jnp.min(jnp.where(gs == mx, nid, pen_rows), axis=0, keepdims=True)
        pick = nid == idx
        attended = attended | (pick & past)
        gs = jnp.where(pick, BELOW_NEG, gs)
    pen = jnp.where(attended, 0.0, NEG)
    r8 = lax.broadcasted_iota(I32, (8, t_len), 0)
    q8 = lax.broadcasted_iota(I32, (8, t_len), 1)
    own8 = lax.shift_right_logical(q8, int(math.log2(blk)))
    bias = jnp.where(r8 == 0, -slope * (own8 * blk).astype(F32),
                     jnp.where(r8 == 1, -slope * (q8 - own8 * blk).astype(F32), jnp.where(r8 < 4, 1.0, 0.0)))
    extra = jnp.concatenate([bias, pen, jnp.zeros((dh - 8 - pen_rows, t_len), F32)], axis=0)
    qa_scr[dh:2 * dh, :] = extra.astype(BF16)

    def query_block(own):
        n_keys = (own + 1) * blk
        cols = slice(own * blk, n_keys)
        s = jnp.dot(ka_scr[0:n_keys, :], qa_scr[:, cols], preferred_element_type=F32)
        k_idx = lax.broadcasted_iota(I32, (blk, blk), 0)
        q_idx = lax.broadcasted_iota(I32, (blk, blk), 1)
        s_own = jnp.where(q_idx >= k_idx, s[own * blk:n_keys], NEG)
        m = jnp.max(s_own, axis=0, keepdims=True)
        if own > 0:
            s_past = s[0:own * blk]
            m = jnp.maximum(m, jnp.max(s_past, axis=0, keepdims=True))
        p = jnp.exp(s_own - m)
        l = jnp.sum(p, axis=0, keepdims=True)
        p = p.astype(BF16)
        if own > 0:
            p_past = jnp.exp(s_past - m)
            l = l + jnp.sum(p_past, axis=0, keepdims=True)
            p = jnp.concatenate([p_past.astype(BF16), p], axis=0)
        acc = jnp.dot(vt_scr[:, 0:n_keys], p, preferred_element_type=F32)
        o = (acc / l).T
        o_ref[cols, :] = (o * _silu(g_ref[cols, :].astype(F32))).astype(o_ref.dtype)

    for own in range(n_blocks):
        query_block(own)


def _moba_prompt(q, k, v, gate):
    b, t, w = q.shape
    assert t % MOBA_BLOCK == 0
    n_blocks = t // MOBA_BLOCK
    pen_rows = -(-n_blocks // 8) * 8
    assert AUG_PEN0 + pen_rows <= MOBA_DH
    assert MOBA_BLOCK & (MOBA_BLOCK - 1) == 0
    spec = pl.BlockSpec((None, t, MOBA_DH), lambda bi, h: (bi, 0, h))
    return pl.pallas_call(
        functools.partial(_moba_prompt_kernel, n_blocks=n_blocks),
        grid=(b, MOBA_HEADS),
        in_specs=[spec, spec, spec, spec],
        out_specs=spec,
        out_shape=jax.ShapeDtypeStruct((b, t, w), BF16),
        scratch_shapes=[
            pltpu.VMEM((t, 2 * MOBA_DH), BF16),
            pltpu.VMEM((MOBA_DH, t), BF16),
            pltpu.VMEM((2 * MOBA_DH, t), BF16),
            pltpu.VMEM((4 * pen_rows, MOBA_DH), F32),
        ],
        compiler_params=_params(("arbitrary", "arbitrary"), 48),
        name="moba_prompt",
    )(q, k, v, gate)


def _kmean_kernel(pt_ref, *refs, blocks_per_step, pages_per_block):
    del pt_ref
    o_ref = refs[-1]
    for j in range(blocks_per_step):
        tot = None
        for p in range(pages_per_block):
            part = jnp.sum(refs[j * pages_per_block + p][...], axis=0)
            tot = part if tot is None else tot + part
        o_ref[j] = tot / float(pages_per_block * PAGE_SIZE)


def _kmean_pages(cache_pages, page_table):
    b, n_pages = page_table.shape
    hd = cache_pages.shape[2:]
    ppb = MOBA_BLOCK // PAGE_SIZE
    n_blocks = n_pages // ppb
    bps = 8
    assert n_pages % ppb == 0 and n_blocks % bps == 0

    def page_spec(p):
        return pl.BlockSpec((None, PAGE_SIZE, *hd), lambda bi, j, pt: (pt[bi, j * bps * ppb + p], 0, 0, 0))

    return pl.pallas_call(
        functools.partial(_kmean_kernel, blocks_per_step=bps, pages_per_block=ppb),
        grid_spec=pltpu.PrefetchScalarGridSpec(
            num_scalar_prefetch=1,
            grid=(b, n_blocks // bps),
            in_specs=[page_spec(p) for p in range(bps * ppb)],
            out_specs=pl.BlockSpec((None, bps, *hd), lambda bi, j, pt: (bi, j, 0, 0)),
        ),
        out_shape=jax.ShapeDtypeStruct((b, n_blocks, *hd), F32),
        compiler_params=_params(("arbitrary", "arbitrary"), 40),
        name="kmean_pages",
    )(page_table, *([cache_pages] * (bps * ppb)))


def _moba_select_kernel(q_ref, km_ref, sel_ref, *, t, topk):
    n_blocks = km_ref.shape[0]
    rows = sel_ref.shape[0]
    rid = lax.broadcasted_iota(I32, (rows, 128), 0)
    lid = lax.broadcasted_iota(I32, (rows, 128), 1)
    nid = lax.broadcasted_iota(I32, (n_blocks, 1), 0)
    out = jnp.zeros((rows, 128), I32)
    for h in range(MOBA_HEADS):
        cs = slice(h * MOBA_DH, (h + 1) * MOBA_DH)
        km = km_ref[:, h, :]
        for i in range(t):
            gs = jnp.sum(km * q_ref[i:i + 1, cs], axis=-1, keepdims=True)
            for r in range(topk):
                mx = jnp.max(gs, axis=0, keepdims=True)
                idx = jnp.min(jnp.where(gs == mx, nid, n_blocks), axis=0, keepdims=True)
                out = jnp.where((rid == h * t + i) & (lid == r), idx, out)
                gs = jnp.where(nid == idx, BELOW_NEG, gs)
    sel_ref[...] = out


def _moba_select(q, kmean):
    b, t, w = q.shape
    n_blocks = kmean.shape[1]
    rows = MOBA_HEADS * t
    return pl.pallas_call(
        functools.partial(_moba_select_kernel, t=t, topk=MOBA_TOPK),
        grid=(b,),
        in_specs=[
            pl.BlockSpec((None, t, w), lambda bi: (bi, 0, 0)),
            pl.BlockSpec((None, n_blocks, MOBA_HEADS, MOBA_DH), lambda bi: (bi, 0, 0, 0)),
        ],
        out_specs=pl.BlockSpec((None, rows, 128), lambda bi: (bi, 0, 0)),
        out_shape=jax.ShapeDtypeStruct((b, rows, 128), I32),
        compiler_params=_params(("arbitrary",), 32),
        name="moba_select",
    )(q, kmean)


def _moba_sample_kernel(pt_ref, sel_ref, q_ref, kn_ref, vn_ref, g_ref, ck_ref, cv_ref, o_ref, kbuf, vbuf, sems,
                        *, t, past_len, topk, pages_per_block):
    b = pl.program_id(0)
    head = pl.program_id(1)
    n_b = pl.num_programs(0)
    n_h = pl.num_programs(1)
    step = b * n_h + head

    n_sel = t * topk * pages_per_block
    slot = step % 2

    def slab_copies(page, hi, slot_, j):
        return (pltpu.make_async_copy(ck_ref.at[page, :, hi, :], kbuf.at[slot_, j], sems.at[0, slot_]),
                pltpu.make_async_copy(cv_ref.at[page, :, hi, :], vbuf.at[slot_, j], sems.at[1, slot_]))

    def start_step(bi, hi, slot_):
        for i in range(t):
            for s in range(topk):
                blk_id = sel_ref[bi, (hi * t + i) * topk + s]
                for p in range(pages_per_block):
                    page = pt_ref[bi, blk_id * pages_per_block + p]
                    for c in slab_copies(page, hi, slot_, (i * topk + s) * pages_per_block + p):
                        c.start()

    def wait_slot(slot_):
        for j in range(n_sel):
            for c in slab_copies(0, 0, slot_, j):
                c.wait()

    @pl.when(step == 0)
    def _():
        start_step(b, head, 0)

    next_h = jnp.where(head + 1 == n_h, 0, head + 1)
    next_b = jnp.where(head + 1 == n_h, jnp.where(b + 1 == n_b, 0, b + 1), b)
    start_step(next_b, next_h, 1 - slot)
    wait_slot(slot)

    slope = _alibi_slope(head)
    q = q_ref[...]
    kn = kn_ref[...]
    vn = vn_ref[...]
    gate = g_ref[...]
    keys_per_query = topk * pages_per_block * PAGE_SIZE
    n_keys = t * keys_per_query

    k_all = kbuf[slot].reshape(n_keys, MOBA_DH)
    v_all = vbuf[slot].reshape(n_keys, MOBA_DH)
    off = lax.broadcasted_iota(I32, (1, PAGE_SIZE), 1)
    k_pos = jnp.concatenate(
        [sel_ref[b, (head * t + i) * topk + s] * MOBA_BLOCK + p * PAGE_SIZE + off
         for i in range(t) for s in range(topk) for p in range(pages_per_block)], axis=1)
    row = lax.broadcasted_iota(I32, (t, n_keys), 0)
    col = lax.broadcasted_iota(I32, (t, n_keys), 1)
    owned = (col >= row * keys_per_query) & (col < (row + 1) * keys_per_query)
    dist = (past_len + row) - k_pos
    s_sel = lax.dot_general(q, k_all, NT_DIMS, preferred_element_type=F32) - slope * dist.astype(F32)
    s_sel = jnp.where(owned & (dist >= 0), s_sel, NEG)

    s_new = jnp.concatenate([jnp.sum(q * kn[j:j + 1, :], axis=-1, keepdims=True) for j in range(t)], axis=1)
    d_new = lax.broadcasted_iota(I32, (t, t), 0) - lax.broadcasted_iota(I32, (t, t), 1)
    s_new = jnp.where(d_new >= 0, s_new - slope * d_new.astype(F32), NEG)

    m = jnp.maximum(jnp.max(s_sel, axis=1, keepdims=True), jnp.max(s_new, axis=1, keepdims=True))
    p_sel = jnp.exp(s_sel - m)
    p_new = jnp.exp(s_new - m)
    l = jnp.sum(p_sel, axis=1, keepdims=True) + jnp.sum(p_new, axis=1, keepdims=True)
    o = jnp.dot(p_sel, v_all, preferred_element_type=F32)
    for j in range(t):
        o = o + p_new[:, j:j + 1] * vn[j:j + 1, :]
    o_ref[...] = (o / l) * _silu(gate)

    @pl.when(step == n_b * n_h - 1)
    def _():
        wait_slot(1 - slot)


def _moba_sample(q, k_new, v_new, gate, cache_k_pages, cache_v_pages, page_table, sel, past_len):
    b, t, w = q.shape
    ppb = MOBA_BLOCK // PAGE_SIZE
    topk = MOBA_TOPK
    row_spec = pl.BlockSpec((None, t, MOBA_DH), lambda bi, h, pt, sl: (bi, 0, h))
    hbm_spec = pl.BlockSpec(memory_space=pl.ANY)
    n_sel = t * topk * ppb
    return pl.pallas_call(
        functools.partial(_moba_sample_kernel, t=t, past_len=past_len, topk=topk, pages_per_block=ppb),
        grid_spec=pltpu.PrefetchScalarGridSpec(
            num_scalar_prefetch=2,
            grid=(b, MOBA_HEADS),
            in_specs=[row_spec, row_spec, row_spec, row_spec, hbm_spec, hbm_spec],
            out_specs=row_spec,
            scratch_shapes=[
                pltpu.VMEM((2, n_sel, PAGE_SIZE, MOBA_DH), F32),
                pltpu.VMEM((2, n_sel, PAGE_SIZE, MOBA_DH), F32),
                pltpu.SemaphoreType.DMA((2, 2)),
            ],
        ),
        out_shape=jax.ShapeDtypeStruct((b, t, w), F32),
        compiler_params=_params(("arbitrary", "arbitrary"), 40),
        name="moba_sample",
    )(page_table, sel, q, k_new, v_new, gate, cache_k_pages, cache_v_pages)


def kernel(x_prompt, x_sample, c_prompt, c_sample, state_ret, cache_k, cache_v, page_table, norm_g, w_ada, b_ada,
           w_ret_in, ret_gn_g, w_ret_out, w_moba_in, moba_q_g, moba_k_g, w_moba_out):
    bp, tp, d = x_prompt.shape
    bs, ts, _ = x_sample.shape
    n_pages = page_table.shape[1]
    past_len = n_pages * PAGE_SIZE
    assert past_len % MOBA_BLOCK == 0 and ts <= MOBA_BLOCK and past_len // MOBA_BLOCK >= MOBA_TOPK
    assert w_ada.shape[0] == 2 and w_ret_in.shape[0] == 1 and w_moba_in.shape[0] == 1

    n_c = bp + bs
    c_all = jnp.concatenate([c_prompt, c_sample, jnp.zeros((-n_c % 8, d), F32)], axis=0)
    mod = _ada(c_all, w_ada, b_ada)

    def group_mod(layer):
        mp = mod[layer, :bp].reshape(bp, 1, 3 * d)
        ms = jnp.repeat(mod[layer, bp:n_c], ts, axis=0).reshape(1, bs * ts, 3 * d)
        return [(m[..., :d], m[..., d:2 * d], m[..., 2 * d:]) for m in (mp, ms)]

    xs = x_sample.reshape(1, bs * ts, d)
    n_ret = w_ret_in.shape[-1]

    (sh_p, sc_p, gt_p), (sh_s, sc_s, gt_s) = group_mod(0)
    proj_p = _ret_proj(x_prompt, norm_g[0], sh_p, sc_p, w_ret_in[0], BF16, tm=256, tn=n_ret)
    og_p, ret_p = _ret_prompt(proj_p, ret_gn_g[0], rows_per_step=256)
    y_p = _out_proj(og_p, w_ret_out[0], x_prompt, gt_p, tm=512)

    proj_s = _ret_proj(xs, norm_g[0], sh_s, sc_s, w_ret_in[0], F32, tm=bs * ts, tn=1024).reshape(bs, ts, -1)
    og_s, ret_s = _ret_sample(proj_s, state_ret[0], ret_gn_g[0])
    y_s = _out_proj(og_s.reshape(1, bs * ts, RET_V).astype(BF16), w_ret_out[0], xs, gt_s, tm=bs * ts)

    (sh_p, sc_p, gt_p), (sh_s, sc_s, gt_s) = group_mod(1)
    q_p, k_p, v_p, g_p = _moba_proj(y_p, norm_g[1], sh_p, sc_p, w_moba_in[0], moba_q_g[0], moba_k_g[0], BF16, tm=256)
    oa_p = _moba_prompt(q_p, k_p, v_p, g_p)
    y_p = _out_proj(oa_p, w_moba_out[0], y_p, gt_p, tm=512)

    q_s, k_s, v_s, g_s = _moba_proj(y_s, norm_g[1], sh_s, sc_s, w_moba_in[0], moba_q_g[0], moba_k_g[0], F32,
                                    tm=bs * ts)
    q_s, k_s, v_s, g_s = (a.reshape(bs, ts, MOBA_W) for a in (q_s, k_s, v_s, g_s))
    ck = cache_k[0]
    cv = cache_v[0]
    kmean = _kmean_pages(ck, page_table)
    sel = _moba_select(q_s, kmean)[:, :, :MOBA_TOPK].reshape(bs, MOBA_HEADS * ts * MOBA_TOPK)
    oa_s = _moba_sample(q_s, k_s, v_s, g_s, ck, cv, page_table, sel, past_len)
    y_s = _out_proj(oa_s.reshape(1, bs * ts, MOBA_W).astype(BF16), w_moba_out[0], y_s, gt_s, tm=bs * ts)

    hd = (MOBA_HEADS, MOBA_DH)
    return (y_p, y_s.reshape(bs, ts, d), ret_p[None], ret_s[None],
            k_p.reshape(1, bp, tp, *hd), v_p.reshape(1, bp, tp, *hd),
            k_s.reshape(1, bs, ts, *hd), v_s.reshape(1, bs, ts, *hd))
```

```python
import functools
import math
from typing import NamedTuple

import jax
import jax.numpy as jnp
from jax import lax
from jax.experimental import pallas as pl
from jax.experimental.pallas import tpu as pltpu

F32 = jnp.float32
BF16 = jnp.bfloat16
I32 = jnp.int32

RET_HEADS = 4
RET_DK = 256
RET_DV = 512
RET_CHUNK = 128
RET_QK = RET_HEADS * RET_DK
RET_V = RET_HEADS * RET_DV
MOBA_HEADS = 8
MOBA_DH = 128
MOBA_W = MOBA_HEADS * MOBA_DH
MOBA_BLOCK = 256
MOBA_TOPK = 3
PAGE_SIZE = 128
EPS = 1e-6
NEG = -1e30
BELOW_NEG = -3e38

V7X_VMEM_BYTES = 64 * 1024 * 1024
NT_DIMS = (((1,), (1,)), ((), ()))


def _params(semantics, vmem_mb):
    assert vmem_mb * 1024 * 1024 < V7X_VMEM_BYTES
    return pltpu.CompilerParams(dimension_semantics=semantics, vmem_limit_bytes=vmem_mb * 1024 * 1024)


def _silu(x):
    return x * jax.nn.sigmoid(x)


def _norm_mod(x, g, sh, sc):
    ms = jnp.mean(x * x, axis=-1, keepdims=True)
    y = x * lax.rsqrt(ms + EPS) * g
    return y * (1.0 + sc) + sh


def _head_rms(x, g):
    ms = jnp.mean(x * x, axis=-1, keepdims=True)
    return x * lax.rsqrt(ms + EPS) * g


def _ret_log_gamma(h):
    return math.log1p(-(2.0 ** (-5.0 - h)))


def _ada_kernel(c_ref, w_ref, b_ref, o_ref):
    s = _silu(c_ref[...]).astype(BF16)
    o_ref[...] = jnp.dot(s, w_ref[...].astype(BF16), preferred_element_type=F32) + b_ref[...]


def _ada(c_all, w_ada, b_ada):
    depth, d, n = w_ada.shape
    m = c_all.shape[0]
    tn = 512
    return pl.pallas_call(
        _ada_kernel,
        grid=(depth, n // tn),
        in_specs=[
            pl.BlockSpec((m, d), lambda l, j: (0, 0)),
            pl.BlockSpec((None, d, tn), lambda l, j: (l, 0, j)),
            pl.BlockSpec((None, 1, tn), lambda l, j: (l, 0, j)),
        ],
        out_specs=pl.BlockSpec((None, m, tn), lambda l, j: (l, 0, j)),
        out_shape=jax.ShapeDtypeStruct((depth, m, n), F32),
        compiler_params=_params(("arbitrary", "arbitrary"), 32),
        name="ada",
    )(c_all, w_ada, b_ada.reshape(depth, 1, n))


def _mod_spec(mod, tm):
    d = mod.shape[-1]
    if mod.shape[1] == 1:
        return pl.BlockSpec((None, 1, d), lambda g, i, *_: (g, 0, 0))
    return pl.BlockSpec((None, tm, d), lambda g, i, *_: (g, i, 0))


def _weight_spec(shape, index_map, resident):
    if resident:
        return pl.BlockSpec(shape, index_map, pipeline_mode=pl.Buffered(1))
    return pl.BlockSpec(shape, index_map)


def _cast_weight(w_ref, wb_scr, col_chunk):
    for c in range(w_ref.shape[1] // col_chunk):
        cs = slice(c * col_chunk, (c + 1) * col_chunk)
        wb_scr[:, cs] = w_ref[:, cs].astype(BF16)


KM_GROUP_PAGES = 4
KM_RING_DEPTH = 3
KM_GROUP_BLOCKS = KM_GROUP_PAGES * PAGE_SIZE // MOBA_BLOCK


class _KmStream(NamedTuple):
    group0: int
    groups_per_step: int
    n_steps: int
    n_pages: int


def _km_scratch(cache):
    return [pltpu.VMEM((KM_RING_DEPTH, KM_GROUP_PAGES, *cache.shape[1:]), F32), pltpu.SemaphoreType.DMA((KM_RING_DEPTH,))]


def _km_out(stream, cache, steps_per_group):
    hd = cache.shape[2:]
    n_blocks = stream.n_steps * stream.groups_per_step * KM_GROUP_BLOCKS
    spec = pl.BlockSpec((stream.groups_per_step * KM_GROUP_BLOCKS, *hd),
                        lambda gi, i: (gi * steps_per_group + i, 0, 0))
    return spec, jax.ShapeDtypeStruct((n_blocks, *hd), F32)


def _km_copy(ck_ref, ring, sems, page, slot, p):
    return pltpu.make_async_copy(ck_ref.at[page], ring.at[slot, p], sems.at[slot])


def _km_start(stream, pt_ref, ck_ref, ring, sems, local, slot):
    first_page = (stream.group0 + local) * KM_GROUP_PAGES
    b = first_page // stream.n_pages
    pg = first_page % stream.n_pages
    for p in range(KM_GROUP_PAGES):
        _km_copy(ck_ref, ring, sems, pt_ref[b, pg + p], slot, p).start()


def _km_wait(ck_ref, ring, sems, slot):
    for p in range(KM_GROUP_PAGES):
        _km_copy(ck_ref, ring, sems, 0, slot, p).wait()


def _run_with_km_stream(stream, km_refs, chunks):
    if stream is None:
        for chunk in chunks:
            chunk()
        return
    pt_ref, ck_ref, km_ref, ring, sems = km_refs
    step = pl.program_id(0) * pl.num_programs(1) + pl.program_id(1)
    gps = stream.groups_per_step
    total = stream.n_steps * gps
    lead = KM_RING_DEPTH - 1
    ppb = MOBA_BLOCK // PAGE_SIZE

    @pl.when(step == 0)
    def _():
        for local in range(lead):
            _km_start(stream, pt_ref, ck_ref, ring, sems, local, local % KM_RING_DEPTH)

    per = -(-len(chunks) // gps)
    for g in range(gps):
        local = step * gps + g
        ahead = local + lead
        _km_start(stream, pt_ref, ck_ref, ring, sems, jnp.where(ahead >= total, ahead - total, ahead),
                  ahead % KM_RING_DEPTH)
        slot = local % KM_RING_DEPTH
        _km_wait(ck_ref, ring, sems, slot)
        for blk in range(KM_GROUP_BLOCKS):
            tot = jnp.sum(ring[slot, blk * ppb], axis=0)
            for p in range(1, ppb):
                tot = tot + jnp.sum(ring[slot, blk * ppb + p], axis=0)
            km_ref[g * KM_GROUP_BLOCKS + blk] = tot / float(MOBA_BLOCK)
        for chunk in chunks[g * per:(g + 1) * per]:
            chunk()
    for chunk in chunks[gps * per:]:
        chunk()

    @pl.when(step == stream.n_steps - 1)
    def _():
        for k in range(lead):
            _km_wait(ck_ref, ring, sems, (total + k) % KM_RING_DEPTH)


def _ret_proj_kernel(*refs, col_chunk, stream):
    x_ref, g_ref, sh_ref, sc_ref, w_ref = refs[:5]
    if stream is None:
        o_ref, wb_scr = refs[5:]
        km_refs = None
    else:
        pt_ref, ck_ref, o_ref, km_ref, wb_scr, ring, sems = refs[5:]
        km_refs = (pt_ref, ck_ref, km_ref, ring, sems)

    @pl.when((pl.program_id(0) == 0) & (pl.program_id(1) == 0))
    def _():
        _cast_weight(w_ref, wb_scr, col_chunk)

    h = _norm_mod(x_ref[...], g_ref[...], sh_ref[...], sc_ref[...]).astype(BF16)

    def column_chunk(c):
        cs = slice(c * col_chunk, (c + 1) * col_chunk)
        o_ref[:, cs] = jnp.dot(h, wb_scr[:, cs], preferred_element_type=F32).astype(o_ref.dtype)

    _run_with_km_stream(stream, km_refs,
                        [functools.partial(column_chunk, c) for c in range(w_ref.shape[1] // col_chunk)])


def _ret_proj(x, norm_g, sh, sc, w, out_dtype, tm, km=None):
    g, r, d = x.shape
    n = w.shape[1]
    in_specs = [
        pl.BlockSpec((None, tm, d), lambda gi, i: (gi, i, 0)),
        pl.BlockSpec((1, d), lambda gi, i: (0, 0)),
        _mod_spec(sh, tm),
        _mod_spec(sc, tm),
        _weight_spec((d, n), lambda gi, i: (0, 0), True),
    ]
    out_specs = [pl.BlockSpec((None, tm, n), lambda gi, i: (gi, i, 0))]
    out_shape = [jax.ShapeDtypeStruct((g, r, n), out_dtype)]
    scratch = [pltpu.VMEM((d, n), BF16)]
    args = [x, norm_g.reshape(1, d), sh, sc, w]
    stream = None
    if km is not None:
        page_table, cache, stream = km
        assert stream.n_steps == g * (r // tm)
        in_specs += [pl.BlockSpec(memory_space=pltpu.SMEM), pl.BlockSpec(memory_space=pl.ANY)]
        km_spec, km_shape = _km_out(stream, cache, r // tm)
        out_specs.append(km_spec)
        out_shape.append(km_shape)
        scratch += _km_scratch(cache)
        args += [page_table, cache]
    outs = pl.pallas_call(
        functools.partial(_ret_proj_kernel, col_chunk=512, stream=stream),
        grid=(g, r // tm),
        in_specs=in_specs,
        out_specs=out_specs,
        out_shape=out_shape,
        scratch_shapes=scratch,
        compiler_params=_params(("arbitrary", "arbitrary"), 58),
        name="ret_proj",
    )(*args)
    return outs if km is not None else outs[0]


def _out_proj_kernel(a_ref, w_ref, x_ref, gt_ref, o_ref, wb_scr):
    @pl.when((pl.program_id(0) == 0) & (pl.program_id(1) == 0))
    def _():
        _cast_weight(w_ref, wb_scr, 512)

    o_ref[...] = x_ref[...] + gt_ref[...] * jnp.dot(a_ref[...], wb_scr[...], preferred_element_type=F32)


def _out_proj(a, w, x, gate, tm):
    g, r, k = a.shape
    d = w.shape[1]
    return pl.pallas_call(
        _out_proj_kernel,
        grid=(g, r // tm),
        in_specs=[
            pl.BlockSpec((None, tm, k), lambda gi, i: (gi, i, 0)),
            _weight_spec((k, d), lambda gi, i: (0, 0), True),
            pl.BlockSpec((None, tm, d), lambda gi, i: (gi, i, 0)),
            _mod_spec(gate, tm),
        ],
        out_specs=pl.BlockSpec((None, tm, d), lambda gi, i: (gi, i, 0)),
        out_shape=jax.ShapeDtypeStruct((g, r, d), F32),
        scratch_shapes=[pltpu.VMEM((k, d), BF16)],
        compiler_params=_params(("arbitrary", "arbitrary"), 48),
        name="out_proj",
    )(a, w, x, gate)


def _ret_prompt_kernel(p_ref, gn_ref, og_ref, s_ref, *, chunk, n_chunks):
    @pl.when(pl.program_id(1) == 0)
    def _():
        s_ref[...] = jnp.zeros_like(s_ref)

    ii = lax.broadcasted_iota(I32, (chunk, chunk), 0)
    jj = lax.broadcasted_iota(I32, (chunk, chunk), 1)
    diff = (ii - jj).astype(F32)
    pos = lax.broadcasted_iota(I32, (chunk, 1), 0).astype(F32)
    for h in range(RET_HEADS):
        lg = _ret_log_gamma(h)
        decay = jnp.where(diff >= 0, jnp.exp(lg * jnp.maximum(diff, 0.0)), 0.0)
        q_decay = jnp.exp(lg * (pos + 1.0))
        k_decay = jnp.exp(lg * (chunk - 1.0 - pos))
        chunk_decay = math.exp(lg * chunk)
        for c in range(n_chunks):
            rows = slice(c * chunk, (c + 1) * chunk)
            q = p_ref[rows, h * RET_DK:(h + 1) * RET_DK]
            k = p_ref[rows, RET_QK + h * RET_DK:RET_QK + (h + 1) * RET_DK].astype(F32) * (RET_DK ** -0.5)
            v = p_ref[rows, 2 * RET_QK + h * RET_DV:2 * RET_QK + (h + 1) * RET_DV]
            gate = p_ref[rows, 2 * RET_QK + RET_V + h * RET_DV:2 * RET_QK + RET_V + (h + 1) * RET_DV].astype(F32)
            state = s_ref[h]
            scores = lax.dot_general(q, k.astype(BF16), NT_DIMS, preferred_element_type=F32) * decay
            inner = jnp.dot(scores.astype(BF16), v, preferred_element_type=F32)
            cross = jnp.dot((q.astype(F32) * q_decay).astype(BF16), state.astype(BF16), preferred_element_type=F32)
            kv = lax.dot_general((k * k_decay).astype(BF16), v, (((0,), (0,)), ((), ())), preferred_element_type=F32)
            s_ref[h] = chunk_decay * state + kv
            o = inner + cross
            oc = o - jnp.mean(o, axis=-1, keepdims=True)
            y = oc * lax.rsqrt(jnp.mean(oc * oc, axis=-1, keepdims=True) + EPS) * gn_ref[h:h + 1, :]
            og_ref[rows, h * RET_DV:(h + 1) * RET_DV] = (y * _silu(gate)).astype(og_ref.dtype)


def _ret_prompt(proj, gn_g, rows_per_step):
    b, t, n = proj.shape
    assert t % RET_CHUNK == 0 and rows_per_step % RET_CHUNK == 0
    return pl.pallas_call(
        functools.partial(_ret_prompt_kernel, chunk=RET_CHUNK, n_chunks=rows_per_step // RET_CHUNK),
        grid=(b, t // rows_per_step),
        in_specs=[
            pl.BlockSpec((None, rows_per_step, n), lambda bi, i: (bi, i, 0)),
            pl.BlockSpec((RET_HEADS, RET_DV), lambda bi, i: (0, 0)),
        ],
        out_specs=[
            pl.BlockSpec((None, rows_per_step, RET_V), lambda bi, i: (bi, i, 0)),
            pl.BlockSpec((None, RET_HEADS, RET_DK, RET_DV), lambda bi, i: (bi, 0, 0, 0)),
        ],
        out_shape=[
            jax.ShapeDtypeStruct((b, t, RET_V), BF16),
            jax.ShapeDtypeStruct((b, RET_HEADS, RET_DK, RET_DV), F32),
        ],
        compiler_params=_params(("arbitrary", "arbitrary"), 40),
        name="ret_prompt",
    )(proj, gn_g)


def _ret_sample_kernel(p_ref, s0_ref, gn_ref, og_ref, s_ref, *, t):
    ii = lax.broadcasted_iota(I32, (t, t), 0)
    jj = lax.broadcasted_iota(I32, (t, t), 1)
    diff = (ii - jj).astype(F32)
    pos = lax.broadcasted_iota(I32, (t, 1), 0).astype(F32)
    pad = -t % 8
    for h in range(RET_HEADS):
        lg = _ret_log_gamma(h)
        decay = jnp.where(diff >= 0, jnp.exp(lg * jnp.maximum(diff, 0.0)), 0.0)
        q_decay = jnp.exp(lg * (pos + 1.0))
        k_decay = jnp.exp(lg * (t - 1.0 - pos))
        chunk_decay = math.exp(lg * t)
        q = p_ref[:, h * RET_DK:(h + 1) * RET_DK]
        k = p_ref[:, RET_QK + h * RET_DK:RET_QK + (h + 1) * RET_DK] * (RET_DK ** -0.5)
        v = p_ref[:, 2 * RET_QK + h * RET_DV:2 * RET_QK + (h + 1) * RET_DV]
        gate = p_ref[:, 2 * RET_QK + RET_V + h * RET_DV:2 * RET_QK + RET_V + (h + 1) * RET_DV]
        state = s0_ref[h]
        o = jnp.dot(q * q_decay, state, preferred_element_type=F32)
        for j in range(t):
            s_j = jnp.sum(q * k[j:j + 1, :], axis=-1, keepdims=True) * decay[:, j:j + 1]
            o = o + s_j * v[j:j + 1, :]
        kd = jnp.concatenate([k * k_decay, jnp.zeros((pad, RET_DK), F32)], axis=0)
        vp = jnp.concatenate([v, jnp.zeros((pad, RET_DV), F32)], axis=0)
        kv = lax.dot_general(kd, vp, (((0,), (0,)), ((), ())), preferred_element_type=F32)
        s_ref[h] = chunk_decay * state + kv
        oc = o - jnp.mean(o, axis=-1, keepdims=True)
        y = oc * lax.rsqrt(jnp.mean(oc * oc, axis=-1, keepdims=True) + EPS) * gn_ref[h:h + 1, :]
        og_ref[:, h * RET_DV:(h + 1) * RET_DV] = (y * _silu(gate)).astype(og_ref.dtype)


def _ret_sample(proj, state0, gn_g):
    b, t, n = proj.shape
    return pl.pallas_call(
        functools.partial(_ret_sample_kernel, t=t),
        grid=(b,),
        in_specs=[
            pl.BlockSpec((None, t, n), lambda bi: (bi, 0, 0)),
            pl.BlockSpec((None, RET_HEADS, RET_DK, RET_DV), lambda bi: (bi, 0, 0, 0)),
            pl.BlockSpec((RET_HEADS, RET_DV), lambda bi: (0, 0)),
        ],
        out_specs=[
            pl.BlockSpec((None, t, RET_V), lambda bi: (bi, 0, 0)),
            pl.BlockSpec((None, RET_HEADS, RET_DK, RET_DV), lambda bi: (bi, 0, 0, 0)),
        ],
        out_shape=[
            jax.ShapeDtypeStruct((b, t, RET_V), F32),
            jax.ShapeDtypeStruct((b, RET_HEADS, RET_DK, RET_DV), F32),
        ],
        compiler_params=_params(("arbitrary",), 40),
        name="ret_sample",
    )(proj, state0, gn_g)


def _moba_proj_kernel(*refs, stream):
    x_ref, g_ref, sh_ref, sc_ref, w_ref, qg_ref, kg_ref = refs[:7]
    if stream is None:
        q_ref, k_ref, v_ref, gt_ref, wb_scr = refs[7:]
        km_refs = None
    else:
        pt_ref, ck_ref, q_ref, k_ref, v_ref, gt_ref, km_ref, wb_scr, ring, sems = refs[7:]
        km_refs = (pt_ref, ck_ref, km_ref, ring, sems)

    @pl.when((pl.program_id(0) == 0) & (pl.program_id(1) == 0))
    def _():
        _cast_weight(w_ref, wb_scr, 512)

    h = _norm_mod(x_ref[...], g_ref[...], sh_ref[...], sc_ref[...]).astype(BF16)
    w = MOBA_W

    def q_part():
        q = jnp.dot(h, wb_scr[:, 0:w], preferred_element_type=F32)
        for hd in range(MOBA_HEADS):
            cs = slice(hd * MOBA_DH, (hd + 1) * MOBA_DH)
            q_ref[:, cs] = (_head_rms(q[:, cs], qg_ref[...]) * (MOBA_DH ** -0.5)).astype(q_ref.dtype)

    def k_part():
        k = jnp.dot(h, wb_scr[:, w:2 * w], preferred_element_type=F32)
        for hd in range(MOBA_HEADS):
            cs = slice(hd * MOBA_DH, (hd + 1) * MOBA_DH)
            k_ref[:, cs] = _head_rms(k[:, cs], kg_ref[...])

    def v_part():
        v_ref[...] = jnp.dot(h, wb_scr[:, 2 * w:3 * w], preferred_element_type=F32)

    def gate_part():
        gt_ref[...] = jnp.dot(h, wb_scr[:, 3 * w:4 * w], preferred_element_type=F32).astype(gt_ref.dtype)

    _run_with_km_stream(stream, km_refs, [q_part, k_part, v_part, gate_part])


def _moba_proj(x, norm_g, sh, sc, w, q_g, k_g, act_dtype, tm, km=None):
    g, r, d = x.shape
    n = w.shape[1]
    row_spec = pl.BlockSpec((None, tm, MOBA_W), lambda gi, i: (gi, i, 0))
    in_specs = [
        pl.BlockSpec((None, tm, d), lambda gi, i: (gi, i, 0)),
        pl.BlockSpec((1, d), lambda gi, i: (0, 0)),
        _mod_spec(sh, tm),
        _mod_spec(sc, tm),
        _weight_spec((d, n), lambda gi, i: (0, 0), True),
        pl.BlockSpec((1, MOBA_DH), lambda gi, i: (0, 0)),
        pl.BlockSpec((1, MOBA_DH), lambda gi, i: (0, 0)),
    ]
    out_specs = [row_spec, row_spec, row_spec, row_spec]
    out_shape = [
        jax.ShapeDtypeStruct((g, r, MOBA_W), act_dtype),
        jax.ShapeDtypeStruct((g, r, MOBA_W), F32),
        jax.ShapeDtypeStruct((g, r, MOBA_W), F32),
        jax.ShapeDtypeStruct((g, r, MOBA_W), act_dtype),
    ]
    scratch = [pltpu.VMEM((d, n), BF16)]
    args = [x, norm_g.reshape(1, d), sh, sc, w, q_g.reshape(1, MOBA_DH), k_g.reshape(1, MOBA_DH)]
    stream = None
    if km is not None:
        page_table, cache, stream = km
        assert stream.n_steps == g * (r // tm)
        in_specs += [pl.BlockSpec(memory_space=pltpu.SMEM), pl.BlockSpec(memory_space=pl.ANY)]
        km_spec, km_shape = _km_out(stream, cache, r // tm)
        out_specs.append(km_spec)
        out_shape.append(km_shape)
        scratch += _km_scratch(cache)
        args += [page_table, cache]
    return pl.pallas_call(
        functools.partial(_moba_proj_kernel, stream=stream),
        grid=(g, r // tm),
        in_specs=in_specs,
        out_specs=out_specs,
        out_shape=out_shape,
        scratch_shapes=scratch,
        compiler_params=_params(("arbitrary", "arbitrary"), 56),
        name="moba_proj",
    )(*args)


def _alibi_slope(head):
    hv = jnp.full((1, 1), head, I32).astype(F32)
    return jnp.exp2(-8.0 * (hv + 1.0) / MOBA_HEADS)


AUG_PEN0 = 8


def _moba_prompt_kernel(q_ref, k_ref, v_ref, g_ref, o_ref, ka_scr, vt_scr, qa_scr, km_scr, *, n_blocks):
    blk = MOBA_BLOCK
    dh = MOBA_DH
    t_len = n_blocks * blk
    slope = _alibi_slope(pl.program_id(1))
    pen_rows = km_scr.shape[0] // 4

    lane = lax.broadcasted_iota(I32, (blk, dh), 1)
    k_off = lax.broadcasted_iota(I32, (blk, dh), 0).astype(F32)
    km_scr[...] = jnp.zeros_like(km_scr)
    for n in range(n_blocks):
        rows = slice(n * blk, (n + 1) * blk)
        kn = k_ref[rows, :]
        ka_scr[rows, 0:dh] = kn.astype(BF16)
        aug = jnp.where(lane < 2, 1.0,
                        jnp.where(lane == 2, slope * float(blk * n),
                                  jnp.where(lane == 3, slope * k_off,
                                            jnp.where(lane == AUG_PEN0 + n, 1.0, 0.0))))
        ka_scr[rows, dh:2 * dh] = aug.astype(BF16)
        vt_scr[:, rows] = v_ref[rows, :].T.astype(BF16)
        mean = jnp.mean(kn, axis=0, keepdims=True)
        hi = mean.astype(BF16).astype(F32)
        mid = (mean - hi).astype(BF16).astype(F32)
        lo = (mean - hi - mid).astype(BF16).astype(F32)
        km_scr[n:n + 1, :] = hi
        km_scr[pen_rows + n:pen_rows + n + 1, :] = mid
        km_scr[2 * pen_rows + n:2 * pen_rows + n + 1, :] = lo
        qa_scr[0:dh, rows] = q_ref[rows, :].astype(F32).T.astype(BF16)

    parts = jnp.dot(km_scr[...].astype(BF16), qa_scr[0:dh, :], preferred_element_type=F32)
    gs = parts[0:pen_rows] + parts[pen_rows:2 * pen_rows] + parts[2 * pen_rows:3 * pen_rows]
    nid = lax.broadcasted_iota(I32, (pen_rows, t_len), 0)
    q_pos = lax.broadcasted_iota(I32, (pen_rows, t_len), 1)
    own = lax.shift_right_logical(q_pos, int(math.log2(blk)))
    past = nid < own
    attended = nid == own
    gs = jnp.where(past, gs, NEG)
    for _ in range(min(MOBA_TOPK, n_blocks)):
        mx = jnp.max(gs, axis=0, keepdims=True)
        idx = jnp.min(jnp.where(gs == mx, nid, pen_rows), axis=0, keepdims=True)
        pick = nid == idx
        attended = attended | (pick & past)
        gs = jnp.where(pick, BELOW_NEG, gs)
    pen = jnp.where(attended, 0.0, NEG)
    r8 = lax.broadcasted_iota(I32, (8, t_len), 0)
    q8 = lax.broadcasted_iota(I32, (8, t_len), 1)
    own8 = lax.shift_right_logical(q8, int(math.log2(blk)))
    bias = jnp.where(r8 == 0, -slope * (own8 * blk).astype(F32),
                     jnp.where(r8 == 1, -slope * (q8 - own8 * blk).astype(F32), jnp.where(r8 < 4, 1.0, 0.0)))
    extra = jnp.concatenate([bias, pen, jnp.zeros((dh - 8 - pen_rows, t_len), F32)], axis=0)
    qa_scr[dh:2 * dh, :] = extra.astype(BF16)

    def query_block(own):
        n_keys = (own + 1) * blk
        cols = slice(own * blk, n_keys)
        s = jnp.dot(ka_scr[0:n_keys, :], qa_scr[:, cols], preferred_element_type=F32)
        k_idx = lax.broadcasted_iota(I32, (blk, blk), 0)
        q_idx = lax.broadcasted_iota(I32, (blk, blk), 1)
        s_own = jnp.where(q_idx >= k_idx, s[own * blk:n_keys], NEG)
        m = jnp.max(s_own, axis=0, keepdims=True)
        if own > 0:
            s_past = s[0:own * blk]
            m = jnp.maximum(m, jnp.max(s_past, axis=0, keepdims=True))
        p = jnp.exp(s_own - m)
        l = jnp.sum(p, axis=0, keepdims=True)
        p = p.astype(BF16)
        if own > 0:
            p_past = jnp.exp(s_past - m)
            l = l + jnp.sum(p_past, axis=0, keepdims=True)
            p = jnp.concatenate([p_past.astype(BF16), p], axis=0)
        acc = jnp.dot(vt_scr[:, 0:n_keys], p, preferred_element_type=F32)
        o = (acc / l).T
        o_ref[cols, :] = (o * _silu(g_ref[cols, :].astype(F32))).astype(o_ref.dtype)

    for own in range(n_blocks):
        query_block(own)


def _moba_prompt(q, k, v, gate):
    b, t, w = q.shape
    assert t % MOBA_BLOCK == 0
    n_blocks = t // MOBA_BLOCK
    pen_rows = -(-n_blocks // 8) * 8
    assert AUG_PEN0 + pen_rows <= MOBA_DH
    assert MOBA_BLOCK & (MOBA_BLOCK - 1) == 0
    spec = pl.BlockSpec((None, t, MOBA_DH), lambda bi, h: (bi, 0, h))
    return pl.pallas_call(
        functools.partial(_moba_prompt_kernel, n_blocks=n_blocks),
        grid=(b, MOBA_HEADS),
        in_specs=[spec, spec, spec, spec],
        out_specs=spec,
        out_shape=jax.ShapeDtypeStruct((b, t, w), BF16),
        scratch_shapes=[
            pltpu.VMEM((t, 2 * MOBA_DH), BF16),
            pltpu.VMEM((MOBA_DH, t), BF16),
            pltpu.VMEM((2 * MOBA_DH, t), BF16),
            pltpu.VMEM((4 * pen_rows, MOBA_DH), F32),
        ],
        compiler_params=_params(("arbitrary", "arbitrary"), 48),
        name="moba_prompt",
    )(q, k, v, gate)


def _moba_select_kernel(q_ref, km_ref, sel_ref, *, t, topk):
    n_blocks = km_ref.shape[0]
    rows = sel_ref.shape[0]
    rid = lax.broadcasted_iota(I32, (rows, 128), 0)
    lid = lax.broadcasted_iota(I32, (rows, 128), 1)
    nid = lax.broadcasted_iota(I32, (n_blocks, 1), 0)
    out = jnp.zeros((rows, 128), I32)
    for h in range(MOBA_HEADS):
        cs = slice(h * MOBA_DH, (h + 1) * MOBA_DH)
        km = km_ref[:, h, :]
        for i in range(t):
            gs = jnp.sum(km * q_ref[i:i + 1, cs], axis=-1, keepdims=True)
            for r in range(topk):
                mx = jnp.max(gs, axis=0, keepdims=True)
                idx = jnp.min(jnp.where(gs == mx, nid, n_blocks), axis=0, keepdims=True)
                out = jnp.where((rid == h * t + i) & (lid == r), idx, out)
                gs = jnp.where(nid == idx, BELOW_NEG, gs)
    sel_ref[...] = out


def _moba_select(q, kmean):
    b, t, w = q.shape
    n_blocks = kmean.shape[1]
    rows = MOBA_HEADS * t
    return pl.pallas_call(
        functools.partial(_moba_select_kernel, t=t, topk=MOBA_TOPK),
        grid=(b,),
        in_specs=[
            pl.BlockSpec((None, t, w), lambda bi: (bi, 0, 0)),
            pl.BlockSpec((None, n_blocks, MOBA_HEADS, MOBA_DH), lambda bi: (bi, 0, 0, 0)),
        ],
        out_specs=pl.BlockSpec((None, rows, 128), lambda bi: (bi, 0, 0)),
        out_shape=jax.ShapeDtypeStruct((b, rows, 128), I32),
        compiler_params=_params(("arbitrary",), 32),
        name="moba_select",
    )(q, kmean)


def _moba_sample_kernel(pt_ref, sel_ref, q_ref, kn_ref, vn_ref, g_ref, ck_ref, cv_ref, o_ref, kbuf, vbuf, sems,
                        *, t, past_len, topk, pages_per_block):
    b = pl.program_id(0)
    head = pl.program_id(1)
    n_b = pl.num_programs(0)
    n_h = pl.num_programs(1)
    step = b * n_h + head

    n_sel = t * topk * pages_per_block
    slot = step % 2

    def slab_copies(page, hi, slot_, j):
        return (pltpu.make_async_copy(ck_ref.at[page, :, hi, :], kbuf.at[slot_, j], sems.at[0, slot_]),
                pltpu.make_async_copy(cv_ref.at[page, :, hi, :], vbuf.at[slot_, j], sems.at[1, slot_]))

    def start_step(bi, hi, slot_):
        for i in range(t):
            for s in range(topk):
                blk_id = sel_ref[bi, (hi * t + i) * topk + s]
                for p in range(pages_per_block):
                    page = pt_ref[bi, blk_id * pages_per_block + p]
                    for c in slab_copies(page, hi, slot_, (i * topk + s) * pages_per_block + p):
                        c.start()

    def wait_slot(slot_):
        for j in range(n_sel):
            for c in slab_copies(0, 0, slot_, j):
                c.wait()

    @pl.when(step == 0)
    def _():
        start_step(b, head, 0)

    next_h = jnp.where(head + 1 == n_h, 0, head + 1)
    next_b = jnp.where(head + 1 == n_h, jnp.where(b + 1 == n_b, 0, b + 1), b)
    start_step(next_b, next_h, 1 - slot)
    wait_slot(slot)

    slope = _alibi_slope(head)
    q = q_ref[...]
    kn = kn_ref[...]
    vn = vn_ref[...]
    gate = g_ref[...]
    keys_per_query = topk * pages_per_block * PAGE_SIZE
    n_keys = t * keys_per_query

    k_all = kbuf[slot].reshape(n_keys, MOBA_DH)
    v_all = vbuf[slot].reshape(n_keys, MOBA_DH)
    off = lax.broadcasted_iota(I32, (1, PAGE_SIZE), 1)
    k_pos = jnp.concatenate(
        [sel_ref[b, (head * t + i) * topk + s] * MOBA_BLOCK + p * PAGE_SIZE + off
         for i in range(t) for s in range(topk) for p in range(pages_per_block)], axis=1)
    row = lax.broadcasted_iota(I32, (t, n_keys), 0)
    col = lax.broadcasted_iota(I32, (t, n_keys), 1)
    owned = (col >= row * keys_per_query) & (col < (row + 1) * keys_per_query)
    dist = (past_len + row) - k_pos
    s_sel = lax.dot_general(q, k_all, NT_DIMS, preferred_element_type=F32) - slope * dist.astype(F32)
    s_sel = jnp.where(owned & (dist >= 0), s_sel, NEG)

    s_new = jnp.concatenate([jnp.sum(q * kn[j:j + 1, :], axis=-1, keepdims=True) for j in range(t)], axis=1)
    d_new = lax.broadcasted_iota(I32, (t, t), 0) - lax.broadcasted_iota(I32, (t, t), 1)
    s_new = jnp.where(d_new >= 0, s_new - slope * d_new.astype(F32), NEG)

    m = jnp.maximum(jnp.max(s_sel, axis=1, keepdims=True), jnp.max(s_new, axis=1, keepdims=True))
    p_sel = jnp.exp(s_sel - m)
    p_new = jnp.exp(s_new - m)
    l = jnp.sum(p_sel, axis=1, keepdims=True) + jnp.sum(p_new, axis=1, keepdims=True)
    o = jnp.dot(p_sel, v_all, preferred_element_type=F32)
    for j in range(t):
        o = o + p_new[:, j:j + 1] * vn[j:j + 1, :]
    o_ref[...] = (o / l) * _silu(gate)

    @pl.when(step == n_b * n_h - 1)
    def _():
        wait_slot(1 - slot)


def _moba_sample(q, k_new, v_new, gate, cache_k_pages, cache_v_pages, page_table, sel, past_len):
    b, t, w = q.shape
    ppb = MOBA_BLOCK // PAGE_SIZE
    topk = MOBA_TOPK
    row_spec = pl.BlockSpec((None, t, MOBA_DH), lambda bi, h, pt, sl: (bi, 0, h))
    hbm_spec = pl.BlockSpec(memory_space=pl.ANY)
    n_sel = t * topk * ppb
    return pl.pallas_call(
        functools.partial(_moba_sample_kernel, t=t, past_len=past_len, topk=topk, pages_per_block=ppb),
        grid_spec=pltpu.PrefetchScalarGridSpec(
            num_scalar_prefetch=2,
            grid=(b, MOBA_HEADS),
            in_specs=[row_spec, row_spec, row_spec, row_spec, hbm_spec, hbm_spec],
            out_specs=row_spec,
            scratch_shapes=[
                pltpu.VMEM((2, n_sel, PAGE_SIZE, MOBA_DH), F32),
                pltpu.VMEM((2, n_sel, PAGE_SIZE, MOBA_DH), F32),
                pltpu.SemaphoreType.DMA((2, 2)),
            ],
        ),
        out_shape=jax.ShapeDtypeStruct((b, t, w), F32),
        compiler_params=_params(("arbitrary", "arbitrary"), 40),
        name="moba_sample",
    )(page_table, sel, q, k_new, v_new, gate, cache_k_pages, cache_v_pages)


def kernel(x_prompt, x_sample, c_prompt, c_sample, state_ret, cache_k, cache_v, page_table, norm_g, w_ada, b_ada,
           w_ret_in, ret_gn_g, w_ret_out, w_moba_in, moba_q_g, moba_k_g, w_moba_out):
    bp, tp, d = x_prompt.shape
    bs, ts, _ = x_sample.shape
    n_pages = page_table.shape[1]
    past_len = n_pages * PAGE_SIZE
    assert past_len % MOBA_BLOCK == 0 and ts <= MOBA_BLOCK and past_len // MOBA_BLOCK >= MOBA_TOPK
    assert w_ada.shape[0] == 2 and w_ret_in.shape[0] == 1 and w_moba_in.shape[0] == 1

    n_c = bp + bs
    c_all = jnp.concatenate([c_prompt, c_sample, jnp.zeros((-n_c % 8, d), F32)], axis=0)
    mod = _ada(c_all, w_ada, b_ada)

    def group_mod(layer):
        mp = mod[layer, :bp].reshape(bp, 1, 3 * d)
        ms = jnp.repeat(mod[layer, bp:n_c], ts, axis=0).reshape(1, bs * ts, 3 * d)
        return [(m[..., :d], m[..., d:2 * d], m[..., 2 * d:]) for m in (mp, ms)]

    xs = x_sample.reshape(1, bs * ts, d)
    ck = cache_k[0]
    cv = cache_v[0]

    tm_proj = 256
    n_steps = bp * (tp // tm_proj)
    n_groups = bs * n_pages // KM_GROUP_PAGES
    assert n_pages % KM_GROUP_PAGES == 0 and n_groups % n_steps == 0 and n_groups // n_steps >= 2
    gps = n_groups // n_steps
    gps_ret = min(gps - 1, -(-gps * 5 // 8))
    stream_ret = _KmStream(0, gps_ret, n_steps, n_pages)
    stream_moba = _KmStream(gps_ret * n_steps, gps - gps_ret, n_steps, n_pages)

    (sh_p, sc_p, gt_p), (sh_s, sc_s, gt_s) = group_mod(0)
    proj_p, km_a = _ret_proj(x_prompt, norm_g[0], sh_p, sc_p, w_ret_in[0], BF16, tm=tm_proj,
                             km=(page_table, ck, stream_ret))
    og_p, ret_p = _ret_prompt(proj_p, ret_gn_g[0], rows_per_step=256)
    y_p = _out_proj(og_p, w_ret_out[0], x_prompt, gt_p, tm=512)

    proj_s = _ret_proj(xs, norm_g[0], sh_s, sc_s, w_ret_in[0], F32, tm=bs * ts).reshape(bs, ts, -1)
    og_s, ret_s = _ret_sample(proj_s, state_ret[0], ret_gn_g[0])
    y_s = _out_proj(og_s.reshape(1, bs * ts, RET_V).astype(BF16), w_ret_out[0], xs, gt_s, tm=bs * ts)

    (sh_p, sc_p, gt_p), (sh_s, sc_s, gt_s) = group_mod(1)
    q_p, k_p, v_p, g_p, km_b = _moba_proj(y_p, norm_g[1], sh_p, sc_p, w_moba_in[0], moba_q_g[0], moba_k_g[0], BF16,
                                          tm=tm_proj, km=(page_table, ck, stream_moba))
    oa_p = _moba_prompt(q_p, k_p, v_p, g_p)
    y_p = _out_proj(oa_p, w_moba_out[0], y_p, gt_p, tm=512)

    q_s, k_s, v_s, g_s = _moba_proj(y_s, norm_g[1], sh_s, sc_s, w_moba_in[0], moba_q_g[0], moba_k_g[0], F32,
                                    tm=bs * ts)
    q_s, k_s, v_s, g_s = (a.reshape(bs, ts, MOBA_W) for a in (q_s, k_s, v_s, g_s))
    kmean = jnp.concatenate([km_a, km_b], axis=0).reshape(bs, past_len // MOBA_BLOCK, MOBA_HEADS, MOBA_DH)
    sel = _moba_select(q_s, kmean)[:, :, :MOBA_TOPK].reshape(bs, MOBA_HEADS * ts * MOBA_TOPK)
    oa_s = _moba_sample(q_s, k_s, v_s, g_s, ck, cv, page_table, sel, past_len)
    y_s = _out_proj(oa_s.reshape(1, bs * ts, MOBA_W).astype(BF16), w_moba_out[0], y_s, gt_s, tm=bs * ts)

    hd = (MOBA_HEADS, MOBA_DH)
    return (y_p, y_s.reshape(bs, ts, d), ret_p[None], ret_s[None],
            k_p.reshape(1, bp, tp, *hd), v_p.reshape(1, bp, tp, *hd),
            k_s.reshape(1, bs, ts, *hd), v_s.reshape(1, bs, ts, *hd))
```

```python
import functools
import math
from typing import NamedTuple

import jax
import jax.numpy as jnp
from jax import lax
from jax.experimental import pallas as pl
from jax.experimental.pallas import tpu as pltpu

F32 = jnp.float32
BF16 = jnp.bfloat16
I32 = jnp.int32

RET_HEADS = 4
RET_DK = 256
RET_DV = 512
RET_CHUNK = 128
RET_QK = RET_HEADS * RET_DK
RET_V = RET_HEADS * RET_DV
MOBA_HEADS = 8
MOBA_DH = 128
MOBA_W = MOBA_HEADS * MOBA_DH
MOBA_BLOCK = 256
MOBA_TOPK = 3
PAGE_SIZE = 128
EPS = 1e-6
NEG = -1e30
BELOW_NEG = -3e38

V7X_VMEM_BYTES = 64 * 1024 * 1024
NT_DIMS = (((1,), (1,)), ((), ()))


def _params(semantics, vmem_mb):
    assert vmem_mb * 1024 * 1024 < V7X_VMEM_BYTES
    return pltpu.CompilerParams(dimension_semantics=semantics, vmem_limit_bytes=vmem_mb * 1024 * 1024)


def _silu(x):
    return x * jax.nn.sigmoid(x)


def _norm_mod(x, g, sh, sc):
    ms = jnp.mean(x * x, axis=-1, keepdims=True)
    y = x * lax.rsqrt(ms + EPS) * g
    return y * (1.0 + sc) + sh


def _head_rms(x, g):
    ms = jnp.mean(x * x, axis=-1, keepdims=True)
    return x * lax.rsqrt(ms + EPS) * g


def _ret_log_gamma(h):
    return math.log1p(-(2.0 ** (-5.0 - h)))


def _ada_kernel(c_ref, w_ref, b_ref, o_ref):
    s = _silu(c_ref[...]).astype(BF16)
    o_ref[...] = jnp.dot(s, w_ref[...].astype(BF16), preferred_element_type=F32) + b_ref[...]


def _ada(c_all, w_ada, b_ada):
    depth, d, n = w_ada.shape
    m = c_all.shape[0]
    tn = 512
    return pl.pallas_call(
        _ada_kernel,
        grid=(depth, n // tn),
        in_specs=[
            pl.BlockSpec((m, d), lambda l, j: (0, 0)),
            pl.BlockSpec((None, d, tn), lambda l, j: (l, 0, j)),
            pl.BlockSpec((None, 1, tn), lambda l, j: (l, 0, j)),
        ],
        out_specs=pl.BlockSpec((None, m, tn), lambda l, j: (l, 0, j)),
        out_shape=jax.ShapeDtypeStruct((depth, m, n), F32),
        compiler_params=_params(("arbitrary", "arbitrary"), 32),
        name="ada",
    )(c_all, w_ada, b_ada.reshape(depth, 1, n))


def _mod_spec(mod, tm):
    d = mod.shape[-1]
    if mod.shape[1] == 1:
        return pl.BlockSpec((None, 1, d), lambda g, i, *_: (g, 0, 0))
    return pl.BlockSpec((None, tm, d), lambda g, i, *_: (g, i, 0))


def _weight_spec(shape, index_map, resident):
    if resident:
        return pl.BlockSpec(shape, index_map, pipeline_mode=pl.Buffered(1))
    return pl.BlockSpec(shape, index_map)


def _cast_weight(w_ref, wb_scr, col_chunk):
    for c in range(w_ref.shape[1] // col_chunk):
        cs = slice(c * col_chunk, (c + 1) * col_chunk)
        wb_scr[:, cs] = w_ref[:, cs].astype(BF16)


PAGES_PER_BLOCK = MOBA_BLOCK // PAGE_SIZE


class _KmStream(NamedTuple):
    page0: int
    pages_per_step: int
    n_steps: int
    n_pages: int


def _km_scratch(stream, cache):
    return [pltpu.VMEM((2, stream.pages_per_step, *cache.shape[1:]), F32), pltpu.SemaphoreType.DMA((2,))]


def _km_out(stream, cache, steps_per_group):
    hd = cache.shape[2:]
    blocks_per_step = stream.pages_per_step // PAGES_PER_BLOCK
    spec = pl.BlockSpec((blocks_per_step, *hd), lambda gi, i: (gi * steps_per_group + i, 0, 0))
    return spec, jax.ShapeDtypeStruct((stream.n_steps * blocks_per_step, *hd), F32)


def _km_copy(ck_ref, buf, sems, page, slot, p):
    return pltpu.make_async_copy(ck_ref.at[page], buf.at[slot, p], sems.at[slot])


def _km_start(stream, pt_ref, ck_ref, buf, sems, step, slot):
    for p in range(stream.pages_per_step):
        flat = stream.page0 + step * stream.pages_per_step + p
        page = pt_ref[flat // stream.n_pages, flat % stream.n_pages]
        _km_copy(ck_ref, buf, sems, page, slot, p).start()


def _km_wait(stream, ck_ref, buf, sems, slot):
    for p in range(stream.pages_per_step):
        _km_copy(ck_ref, buf, sems, 0, slot, p).wait()


def _run_with_km_stream(stream, km_refs, chunks):
    if stream is not None:
        pt_ref, ck_ref, km_ref, buf, sems = km_refs
        step = pl.program_id(0) * pl.num_programs(1) + pl.program_id(1)
        slot = step % 2

        @pl.when(step == 0)
        def _():
            _km_start(stream, pt_ref, ck_ref, buf, sems, step, 0)

        _km_wait(stream, ck_ref, buf, sems, slot)
        _km_start(stream, pt_ref, ck_ref, buf, sems, jnp.where(step + 1 == stream.n_steps, 0, step + 1), 1 - slot)
        for blk in range(stream.pages_per_step // PAGES_PER_BLOCK):
            tot = jnp.sum(buf[slot, blk * PAGES_PER_BLOCK], axis=0)
            for p in range(1, PAGES_PER_BLOCK):
                tot = tot + jnp.sum(buf[slot, blk * PAGES_PER_BLOCK + p], axis=0)
            km_ref[blk] = tot / float(MOBA_BLOCK)

    for chunk in chunks:
        chunk()

    if stream is not None:
        @pl.when(step == stream.n_steps - 1)
        def _():
            _km_wait(stream, ck_ref, buf, sems, 1 - slot)


def _ret_proj_kernel(*refs, col_chunk, stream):
    x_ref, g_ref, sh_ref, sc_ref, w_ref = refs[:5]
    if stream is None:
        (o_ref,) = refs[5:]
        km_refs = None
    else:
        pt_ref, ck_ref, o_ref, km_ref, buf, sems = refs[5:]
        km_refs = (pt_ref, ck_ref, km_ref, buf, sems)

    h = _norm_mod(x_ref[...], g_ref[...], sh_ref[...], sc_ref[...]).astype(BF16)

    def column_chunk(c):
        cs = slice(c * col_chunk, (c + 1) * col_chunk)
        o_ref[:, cs] = jnp.dot(h, w_ref[:, cs], preferred_element_type=F32).astype(o_ref.dtype)

    _run_with_km_stream(stream, km_refs,
                        [functools.partial(column_chunk, c) for c in range(w_ref.shape[1] // col_chunk)])


def _ret_proj(x, norm_g, sh, sc, w, out_dtype, tm, km=None):
    g, r, d = x.shape
    n = w.shape[1]
    assert w.dtype == BF16
    in_specs = [
        pl.BlockSpec((None, tm, d), lambda gi, i: (gi, i, 0)),
        pl.BlockSpec((1, d), lambda gi, i: (0, 0)),
        _mod_spec(sh, tm),
        _mod_spec(sc, tm),
        _weight_spec((d, n), lambda gi, i: (0, 0), True),
    ]
    out_specs = [pl.BlockSpec((None, tm, n), lambda gi, i: (gi, i, 0))]
    out_shape = [jax.ShapeDtypeStruct((g, r, n), out_dtype)]
    scratch = []
    args = [x, norm_g.reshape(1, d), sh, sc, w]
    stream = None
    if km is not None:
        page_table, cache, stream = km
        assert stream.n_steps == g * (r // tm)
        in_specs += [pl.BlockSpec(memory_space=pltpu.SMEM), pl.BlockSpec(memory_space=pl.ANY)]
        km_spec, km_shape = _km_out(stream, cache, r // tm)
        out_specs.append(km_spec)
        out_shape.append(km_shape)
        scratch += _km_scratch(stream, cache)
        args += [page_table, cache]
    outs = pl.pallas_call(
        functools.partial(_ret_proj_kernel, col_chunk=512, stream=stream),
        grid=(g, r // tm),
        in_specs=in_specs,
        out_specs=out_specs,
        out_shape=out_shape,
        scratch_shapes=scratch,
        compiler_params=_params(("arbitrary", "arbitrary"), 58),
        name="ret_proj",
    )(*args)
    return outs if km is not None else outs[0]


def _out_proj_kernel(a_ref, w_ref, x_ref, gt_ref, o_ref, wb_scr):
    @pl.when((pl.program_id(0) == 0) & (pl.program_id(1) == 0))
    def _():
        _cast_weight(w_ref, wb_scr, 512)

    o_ref[...] = x_ref[...] + gt_ref[...] * jnp.dot(a_ref[...], wb_scr[...], preferred_element_type=F32)


def _out_proj(a, w, x, gate, tm):
    g, r, k = a.shape
    d = w.shape[1]
    return pl.pallas_call(
        _out_proj_kernel,
        grid=(g, r // tm),
        in_specs=[
            pl.BlockSpec((None, tm, k), lambda gi, i: (gi, i, 0)),
            _weight_spec((k, d), lambda gi, i: (0, 0), True),
            pl.BlockSpec((None, tm, d), lambda gi, i: (gi, i, 0)),
            _mod_spec(gate, tm),
        ],
        out_specs=pl.BlockSpec((None, tm, d), lambda gi, i: (gi, i, 0)),
        out_shape=jax.ShapeDtypeStruct((g, r, d), F32),
        scratch_shapes=[pltpu.VMEM((k, d), BF16)],
        compiler_params=_params(("arbitrary", "arbitrary"), 48),
        name="out_proj",
    )(a, w, x, gate)


def _ret_prompt_kernel(p_ref, gn_ref, og_ref, s_ref, *, chunk, n_chunks):
    @pl.when(pl.program_id(1) == 0)
    def _():
        s_ref[...] = jnp.zeros_like(s_ref)

    ii = lax.broadcasted_iota(I32, (chunk, chunk), 0)
    jj = lax.broadcasted_iota(I32, (chunk, chunk), 1)
    diff = (ii - jj).astype(F32)
    pos = lax.broadcasted_iota(I32, (chunk, 1), 0).astype(F32)
    for h in range(RET_HEADS):
        lg = _ret_log_gamma(h)
        decay = jnp.where(diff >= 0, jnp.exp(lg * jnp.maximum(diff, 0.0)), 0.0)
        q_decay = jnp.exp(lg * (pos + 1.0))
        k_decay = jnp.exp(lg * (chunk - 1.0 - pos))
        chunk_decay = math.exp(lg * chunk)
        for c in range(n_chunks):
            rows = slice(c * chunk, (c + 1) * chunk)
            q = p_ref[rows, h * RET_DK:(h + 1) * RET_DK]
            k = p_ref[rows, RET_QK + h * RET_DK:RET_QK + (h + 1) * RET_DK].astype(F32) * (RET_DK ** -0.5)
            v = p_ref[rows, 2 * RET_QK + h * RET_DV:2 * RET_QK + (h + 1) * RET_DV]
            gate = p_ref[rows, 2 * RET_QK + RET_V + h * RET_DV:2 * RET_QK + RET_V + (h + 1) * RET_DV].astype(F32)
            state = s_ref[h]
            scores = lax.dot_general(q, k.astype(BF16), NT_DIMS, preferred_element_type=F32) * decay
            inner = jnp.dot(scores.astype(BF16), v, preferred_element_type=F32)
            cross = jnp.dot((q.astype(F32) * q_decay).astype(BF16), state.astype(BF16), preferred_element_type=F32)
            kv = lax.dot_general((k * k_decay).astype(BF16), v, (((0,), (0,)), ((), ())), preferred_element_type=F32)
            s_ref[h] = chunk_decay * state + kv
            o = inner + cross
            oc = o - jnp.mean(o, axis=-1, keepdims=True)
            y = oc * lax.rsqrt(jnp.mean(oc * oc, axis=-1, keepdims=True) + EPS) * gn_ref[h:h + 1, :]
            og_ref[rows, h * RET_DV:(h + 1) * RET_DV] = (y * _silu(gate)).astype(og_ref.dtype)


def _ret_prompt(proj, gn_g, rows_per_step):
    b, t, n = proj.shape
    assert t % RET_CHUNK == 0 and rows_per_step % RET_CHUNK == 0
    return pl.pallas_call(
        functools.partial(_ret_prompt_kernel, chunk=RET_CHUNK, n_chunks=rows_per_step // RET_CHUNK),
        grid=(b, t // rows_per_step),
        in_specs=[
            pl.BlockSpec((None, rows_per_step, n), lambda bi, i: (bi, i, 0)),
            pl.BlockSpec((RET_HEADS, RET_DV), lambda bi, i: (0, 0)),
        ],
        out_specs=[
            pl.BlockSpec((None, rows_per_step, RET_V), lambda bi, i: (bi, i, 0)),
            pl.BlockSpec((None, RET_HEADS, RET_DK, RET_DV), lambda bi, i: (bi, 0, 0, 0)),
        ],
        out_shape=[
            jax.ShapeDtypeStruct((b, t, RET_V), BF16),
            jax.ShapeDtypeStruct((b, RET_HEADS, RET_DK, RET_DV), F32),
        ],
        compiler_params=_params(("arbitrary", "arbitrary"), 40),
        name="ret_prompt",
    )(proj, gn_g)


def _ret_sample_kernel(p_ref, s0_ref, gn_ref, og_ref, s_ref, *, t):
    ii = lax.broadcasted_iota(I32, (t, t), 0)
    jj = lax.broadcasted_iota(I32, (t, t), 1)
    diff = (ii - jj).astype(F32)
    pos = lax.broadcasted_iota(I32, (t, 1), 0).astype(F32)
    pad = -t % 8
    for h in range(RET_HEADS):
        lg = _ret_log_gamma(h)
        decay = jnp.where(diff >= 0, jnp.exp(lg * jnp.maximum(diff, 0.0)), 0.0)
        q_decay = jnp.exp(lg * (pos + 1.0))
        k_decay = jnp.exp(lg * (t - 1.0 - pos))
        chunk_decay = math.exp(lg * t)
        q = p_ref[:, h * RET_DK:(h + 1) * RET_DK]
        k = p_ref[:, RET_QK + h * RET_DK:RET_QK + (h + 1) * RET_DK] * (RET_DK ** -0.5)
        v = p_ref[:, 2 * RET_QK + h * RET_DV:2 * RET_QK + (h + 1) * RET_DV]
        gate = p_ref[:, 2 * RET_QK + RET_V + h * RET_DV:2 * RET_QK + RET_V + (h + 1) * RET_DV]
        state = s0_ref[h]
        o = jnp.dot(q * q_decay, state, preferred_element_type=F32)
        for j in range(t):
            s_j = jnp.sum(q * k[j:j + 1, :], axis=-1, keepdims=True) * decay[:, j:j + 1]
            o = o + s_j * v[j:j + 1, :]
        kd = jnp.concatenate([k * k_decay, jnp.zeros((pad, RET_DK), F32)], axis=0)
        vp = jnp.concatenate([v, jnp.zeros((pad, RET_DV), F32)], axis=0)
        kv = lax.dot_general(kd, vp, (((0,), (0,)), ((), ())), preferred_element_type=F32)
        s_ref[h] = chunk_decay * state + kv
        oc = o - jnp.mean(o, axis=-1, keepdims=True)
        y = oc * lax.rsqrt(jnp.mean(oc * oc, axis=-1, keepdims=True) + EPS) * gn_ref[h:h + 1, :]
        og_ref[:, h * RET_DV:(h + 1) * RET_DV] = (y * _silu(gate)).astype(og_ref.dtype)


def _ret_sample(proj, state0, gn_g):
    b, t, n = proj.shape
    return pl.pallas_call(
        functools.partial(_ret_sample_kernel, t=t),
        grid=(b,),
        in_specs=[
            pl.BlockSpec((None, t, n), lambda bi: (bi, 0, 0)),
            pl.BlockSpec((None, RET_HEADS, RET_DK, RET_DV), lambda bi: (bi, 0, 0, 0)),
            pl.BlockSpec((RET_HEADS, RET_DV), lambda bi: (0, 0)),
        ],
        out_specs=[
            pl.BlockSpec((None, t, RET_V), lambda bi: (bi, 0, 0)),
            pl.BlockSpec((None, RET_HEADS, RET_DK, RET_DV), lambda bi: (bi, 0, 0, 0)),
        ],
        out_shape=[
            jax.ShapeDtypeStruct((b, t, RET_V), F32),
            jax.ShapeDtypeStruct((b, RET_HEADS, RET_DK, RET_DV), F32),
        ],
        compiler_params=_params(("arbitrary",), 40),
        name="ret_sample",
    )(proj, state0, gn_g)


def _moba_proj_kernel(*refs, stream):
    x_ref, g_ref, sh_ref, sc_ref, w_ref, qg_ref, kg_ref = refs[:7]
    if stream is None:
        q_ref, k_ref, v_ref, gt_ref, wb_scr = refs[7:]
        km_refs = None
    else:
        pt_ref, ck_ref, q_ref, k_ref, v_ref, gt_ref, km_ref, wb_scr, ring, sems = refs[7:]
        km_refs = (pt_ref, ck_ref, km_ref, ring, sems)

    @pl.when((pl.program_id(0) == 0) & (pl.program_id(1) == 0))
    def _():
        _cast_weight(w_ref, wb_scr, 512)

    h = _norm_mod(x_ref[...], g_ref[...], sh_ref[...], sc_ref[...]).astype(BF16)
    w = MOBA_W

    def q_part():
        q = jnp.dot(h, wb_scr[:, 0:w], preferred_element_type=F32)
        for hd in range(MOBA_HEADS):
            cs = slice(hd * MOBA_DH, (hd + 1) * MOBA_DH)
            q_ref[:, cs] = (_head_rms(q[:, cs], qg_ref[...]) * (MOBA_DH ** -0.5)).astype(q_ref.dtype)

    def k_part():
        k = jnp.dot(h, wb_scr[:, w:2 * w], preferred_element_type=F32)
        for hd in range(MOBA_HEADS):
            cs = slice(hd * MOBA_DH, (hd + 1) * MOBA_DH)
            k_ref[:, cs] = _head_rms(k[:, cs], kg_ref[...])

    def v_part():
        v_ref[...] = jnp.dot(h, wb_scr[:, 2 * w:3 * w], preferred_element_type=F32)

    def gate_part():
        gt_ref[...] = jnp.dot(h, wb_scr[:, 3 * w:4 * w], preferred_element_type=F32).astype(gt_ref.dtype)

    _run_with_km_stream(stream, km_refs, [q_part, k_part, v_part, gate_part])


def _moba_proj(x, norm_g, sh, sc, w, q_g, k_g, act_dtype, tm, km=None):
    g, r, d = x.shape
    n = w.shape[1]
    row_spec = pl.BlockSpec((None, tm, MOBA_W), lambda gi, i: (gi, i, 0))
    in_specs = [
        pl.BlockSpec((None, tm, d), lambda gi, i: (gi, i, 0)),
        pl.BlockSpec((1, d), lambda gi, i: (0, 0)),
        _mod_spec(sh, tm),
        _mod_spec(sc, tm),
        _weight_spec((d, n), lambda gi, i: (0, 0), True),
        pl.BlockSpec((1, MOBA_DH), lambda gi, i: (0, 0)),
        pl.BlockSpec((1, MOBA_DH), lambda gi, i: (0, 0)),
    ]
    out_specs = [row_spec, row_spec, row_spec, row_spec]
    out_shape = [
        jax.ShapeDtypeStruct((g, r, MOBA_W), act_dtype),
        jax.ShapeDtypeStruct((g, r, MOBA_W), F32),
        jax.ShapeDtypeStruct((g, r, MOBA_W), F32),
        jax.ShapeDtypeStruct((g, r, MOBA_W), act_dtype),
    ]
    scratch = [pltpu.VMEM((d, n), BF16)]
    args = [x, norm_g.reshape(1, d), sh, sc, w, q_g.reshape(1, MOBA_DH), k_g.reshape(1, MOBA_DH)]
    stream = None
    if km is not None:
        page_table, cache, stream = km
        assert stream.n_steps == g * (r // tm)
        in_specs += [pl.BlockSpec(memory_space=pltpu.SMEM), pl.BlockSpec(memory_space=pl.ANY)]
        km_spec, km_shape = _km_out(stream, cache, r // tm)
        out_specs.append(km_spec)
        out_shape.append(km_shape)
        scratch += _km_scratch(stream, cache)
        args += [page_table, cache]
    return pl.pallas_call(
        functools.partial(_moba_proj_kernel, stream=stream),
        grid=(g, r // tm),
        in_specs=in_specs,
        out_specs=out_specs,
        out_shape=out_shape,
        scratch_shapes=scratch,
        compiler_params=_params(("arbitrary", "arbitrary"), 56),
        name="moba_proj",
    )(*args)


def _alibi_slope(head):
    hv = jnp.full((1, 1), head, I32).astype(F32)
    return jnp.exp2(-8.0 * (hv + 1.0) / MOBA_HEADS)


AUG_PEN0 = 8


def _moba_prompt_kernel(q_ref, k_ref, v_ref, g_ref, o_ref, ka_scr, vt_scr, qa_scr, km_scr, *, n_blocks):
    blk = MOBA_BLOCK
    dh = MOBA_DH
    t_len = n_blocks * blk
    slope = _alibi_slope(pl.program_id(1))
    pen_rows = km_scr.shape[0] // 4

    lane = lax.broadcasted_iota(I32, (blk, dh), 1)
    k_off = lax.broadcasted_iota(I32, (blk, dh), 0).astype(F32)
    km_scr[...] = jnp.zeros_like(km_scr)
    for n in range(n_blocks):
        rows = slice(n * blk, (n + 1) * blk)
        kn = k_ref[rows, :]
        ka_scr[rows, 0:dh] = kn.astype(BF16)
        aug = jnp.where(lane < 2, 1.0,
                        jnp.where(lane == 2, slope * float(blk * n),
                                  jnp.where(lane == 3, slope * k_off,
                                            jnp.where(lane == AUG_PEN0 + n, 1.0, 0.0))))
        ka_scr[rows, dh:2 * dh] = aug.astype(BF16)
        vt_scr[:, rows] = v_ref[rows, :].T.astype(BF16)
        mean = jnp.mean(kn, axis=0, keepdims=True)
        hi = mean.astype(BF16).astype(F32)
        mid = (mean - hi).astype(BF16).astype(F32)
        lo = (mean - hi - mid).astype(BF16).astype(F32)
        km_scr[n:n + 1, :] = hi
        km_scr[pen_rows + n:pen_rows + n + 1, :] = mid
        km_scr[2 * pen_rows + n:2 * pen_rows + n + 1, :] = lo
        qa_scr[0:dh, rows] = q_ref[rows, :].astype(F32).T.astype(BF16)

    parts = jnp.dot(km_scr[...].astype(BF16), qa_scr[0:dh, :], preferred_element_type=F32)
    gs = parts[0:pen_rows] + parts[pen_rows:2 * pen_rows] + parts[2 * pen_rows:3 * pen_rows]
    nid = lax.broadcasted_iota(I32, (pen_rows, t_len), 0)
    q_pos = lax.broadcasted_iota(I32, (pen_rows, t_len), 1)
    own = lax.shift_right_logical(q_pos, int(math.log2(blk)))
    past = nid < own
    attended = nid == own
    gs = jnp.where(past, gs, NEG)
    for _ in range(min(MOBA_TOPK, n_blocks)):
        mx = jnp.max(gs, axis=0, keepdims=True)
        idx = jnp.min(jnp.where(gs == mx, nid, pen_rows), axis=0, keepdims=True)
        pick = nid == idx
        attended = attended | (pick & past)
        gs = jnp.where(pick, BELOW_NEG, gs)
    pen = jnp.where(attended, 0.0, NEG)
    r8 = lax.broadcasted_iota(I32, (8, t_len), 0)
    q8 = lax.broadcasted_iota(I32, (8, t_len), 1)
    own8 = lax.shift_right_logical(q8, int(math.log2(blk)))
    bias = jnp.where(r8 == 0, -slope * (own8 * blk).astype(F32),
                     jnp.where(r8 == 1, -slope * (q8 - own8 * blk).astype(F32), jnp.where(r8 < 4, 1.0, 0.0)))
    extra = jnp.concatenate([bias, pen, jnp.zeros((dh - 8 - pen_rows, t_len), F32)], axis=0)
    qa_scr[dh:2 * dh, :] = extra.astype(BF16)

    def query_block(own):
        n_keys = (own + 1) * blk
        cols = slice(own * blk, n_keys)
        s = jnp.dot(ka_scr[0:n_keys, :], qa_scr[:, cols], preferred_element_type=F32)
        k_idx = lax.broadcasted_iota(I32, (blk, blk), 0)
        q_idx = lax.broadcasted_iota(I32, (blk, blk), 1)
        s_own = jnp.where(q_idx >= k_idx, s[own * blk:n_keys], NEG)
        m = jnp.max(s_own, axis=0, keepdims=True)
        if own > 0:
            s_past = s[0:own * blk]
            m = jnp.maximum(m, jnp.max(s_past, axis=0, keepdims=True))
        p = jnp.exp(s_own - m)
        l = jnp.sum(p, axis=0, keepdims=True)
        p = p.astype(BF16)
        if own > 0:
            p_past = jnp.exp(s_past - m)
            l = l + jnp.sum(p_past, axis=0, keepdims=True)
            p = jnp.concatenate([p_past.astype(BF16), p], axis=0)
        acc = jnp.dot(vt_scr[:, 0:n_keys], p, preferred_element_type=F32)
        o = (acc / l).T
        o_ref[cols, :] = (o * _silu(g_ref[cols, :].astype(F32))).astype(o_ref.dtype)

    for own in range(n_blocks):
        query_block(own)


def _moba_prompt(q, k, v, gate):
    b, t, w = q.shape
    assert t % MOBA_BLOCK == 0
    n_blocks = t // MOBA_BLOCK
    pen_rows = -(-n_blocks // 8) * 8
    assert AUG_PEN0 + pen_rows <= MOBA_DH
    assert MOBA_BLOCK & (MOBA_BLOCK - 1) == 0
    spec = pl.BlockSpec((None, t, MOBA_DH), lambda bi, h: (bi, 0, h))
    return pl.pallas_call(
        functools.partial(_moba_prompt_kernel, n_blocks=n_blocks),
        grid=(b, MOBA_HEADS),
        in_specs=[spec, spec, spec, spec],
        out_specs=spec,
        out_shape=jax.ShapeDtypeStruct((b, t, w), BF16),
        scratch_shapes=[
            pltpu.VMEM((t, 2 * MOBA_DH), BF16),
            pltpu.VMEM((MOBA_DH, t), BF16),
            pltpu.VMEM((2 * MOBA_DH, t), BF16),
            pltpu.VMEM((4 * pen_rows, MOBA_DH), F32),
        ],
        compiler_params=_params(("arbitrary", "arbitrary"), 48),
        name="moba_prompt",
    )(q, k, v, gate)


def _moba_select_kernel(q_ref, km_ref, sel_ref, *, t, topk):
    n_blocks = km_ref.shape[0]
    rows = sel_ref.shape[0]
    rid = lax.broadcasted_iota(I32, (rows, 128), 0)
    lid = lax.broadcasted_iota(I32, (rows, 128), 1)
    nid = lax.broadcasted_iota(I32, (n_blocks, 1), 0)
    out = jnp.zeros((rows, 128), I32)
    for h in range(MOBA_HEADS):
        cs = slice(h * MOBA_DH, (h + 1) * MOBA_DH)
        km = km_ref[:, h, :]
        for i in range(t):
            gs = jnp.sum(km * q_ref[i:i + 1, cs], axis=-1, keepdims=True)
            for r in range(topk):
                mx = jnp.max(gs, axis=0, keepdims=True)
                idx = jnp.min(jnp.where(gs == mx, nid, n_blocks), axis=0, keepdims=True)
                out = jnp.where((rid == h * t + i) & (lid == r), idx, out)
                gs = jnp.where(nid == idx, BELOW_NEG, gs)
    sel_ref[...] = out


def _moba_select(q, kmean):
    b, t, w = q.shape
    n_blocks = kmean.shape[1]
    rows = MOBA_HEADS * t
    return pl.pallas_call(
        functools.partial(_moba_select_kernel, t=t, topk=MOBA_TOPK),
        grid=(b,),
        in_specs=[
            pl.BlockSpec((None, t, w), lambda bi: (bi, 0, 0)),
            pl.BlockSpec((None, n_blocks, MOBA_HEADS, MOBA_DH), lambda bi: (bi, 0, 0, 0)),
        ],
        out_specs=pl.BlockSpec((None, rows, 128), lambda bi: (bi, 0, 0)),
        out_shape=jax.ShapeDtypeStruct((b, rows, 128), I32),
        compiler_params=_params(("arbitrary",), 32),
        name="moba_select",
    )(q, kmean)


def _moba_sample_kernel(pt_ref, sel_ref, q_ref, kn_ref, vn_ref, g_ref, ck_ref, cv_ref, o_ref, kbuf, vbuf, sems,
                        *, t, past_len, topk, pages_per_block):
    b = pl.program_id(0)
    head = pl.program_id(1)
    n_b = pl.num_programs(0)
    n_h = pl.num_programs(1)
    step = b * n_h + head

    n_sel = t * topk * pages_per_block
    slot = step % 2

    def slab_copies(page, hi, slot_, j):
        return (pltpu.make_async_copy(ck_ref.at[page, :, hi, :], kbuf.at[slot_, j], sems.at[0, slot_]),
                pltpu.make_async_copy(cv_ref.at[page, :, hi, :], vbuf.at[slot_, j], sems.at[1, slot_]))

    def start_step(bi, hi, slot_):
        for i in range(t):
            for s in range(topk):
                blk_id = sel_ref[bi, (hi * t + i) * topk + s]
                for p in range(pages_per_block):
                    page = pt_ref[bi, blk_id * pages_per_block + p]
                    for c in slab_copies(page, hi, slot_, (i * topk + s) * pages_per_block + p):
                        c.start()

    def wait_slot(slot_):
        for j in range(n_sel):
            for c in slab_copies(0, 0, slot_, j):
                c.wait()

    @pl.when(step == 0)
    def _():
        start_step(b, head, 0)

    next_h = jnp.where(head + 1 == n_h, 0, head + 1)
    next_b = jnp.where(head + 1 == n_h, jnp.where(b + 1 == n_b, 0, b + 1), b)
    start_step(next_b, next_h, 1 - slot)
    wait_slot(slot)

    slope = _alibi_slope(head)
    q = q_ref[...]
    kn = kn_ref[...]
    vn = vn_ref[...]
    gate = g_ref[...]
    keys_per_query = topk * pages_per_block * PAGE_SIZE
    n_keys = t * keys_per_query

    k_all = kbuf[slot].reshape(n_keys, MOBA_DH)
    v_all = vbuf[slot].reshape(n_keys, MOBA_DH)
    off = lax.broadcasted_iota(I32, (1, PAGE_SIZE), 1)
    k_pos = jnp.concatenate(
        [sel_ref[b, (head * t + i) * topk + s] * MOBA_BLOCK + p * PAGE_SIZE + off
         for i in range(t) for s in range(topk) for p in range(pages_per_block)], axis=1)
    row = lax.broadcasted_iota(I32, (t, n_keys), 0)
    col = lax.broadcasted_iota(I32, (t, n_keys), 1)
    owned = (col >= row * keys_per_query) & (col < (row + 1) * keys_per_query)
    dist = (past_len + row) - k_pos
    s_sel = lax.dot_general(q, k_all, NT_DIMS, preferred_element_type=F32) - slope * dist.astype(F32)
    s_sel = jnp.where(owned & (dist >= 0), s_sel, NEG)

    s_new = jnp.concatenate([jnp.sum(q * kn[j:j + 1, :], axis=-1, keepdims=True) for j in range(t)], axis=1)
    d_new = lax.broadcasted_iota(I32, (t, t), 0) - lax.broadcasted_iota(I32, (t, t), 1)
    s_new = jnp.where(d_new >= 0, s_new - slope * d_new.astype(F32), NEG)

    m = jnp.maximum(jnp.max(s_sel, axis=1, keepdims=True), jnp.max(s_new, axis=1, keepdims=True))
    p_sel = jnp.exp(s_sel - m)
    p_new = jnp.exp(s_new - m)
    l = jnp.sum(p_sel, axis=1, keepdims=True) + jnp.sum(p_new, axis=1, keepdims=True)
    o = jnp.dot(p_sel, v_all, preferred_element_type=F32)
    for j in range(t):
        o = o + p_new[:, j:j + 1] * vn[j:j + 1, :]
    o_ref[...] = (o / l) * _silu(gate)

    @pl.when(step == n_b * n_h - 1)
    def _():
        wait_slot(1 - slot)


def _moba_sample(q, k_new, v_new, gate, cache_k_pages, cache_v_pages, page_table, sel, past_len):
    b, t, w = q.shape
    ppb = MOBA_BLOCK // PAGE_SIZE
    topk = MOBA_TOPK
    row_spec = pl.BlockSpec((None, t, MOBA_DH), lambda bi, h, pt, sl: (bi, 0, h))
    hbm_spec = pl.BlockSpec(memory_space=pl.ANY)
    n_sel = t * topk * ppb
    return pl.pallas_call(
        functools.partial(_moba_sample_kernel, t=t, past_len=past_len, topk=topk, pages_per_block=ppb),
        grid_spec=pltpu.PrefetchScalarGridSpec(
            num_scalar_prefetch=2,
            grid=(b, MOBA_HEADS),
            in_specs=[row_spec, row_spec, row_spec, row_spec, hbm_spec, hbm_spec],
            out_specs=row_spec,
            scratch_shapes=[
                pltpu.VMEM((2, n_sel, PAGE_SIZE, MOBA_DH), F32),
                pltpu.VMEM((2, n_sel, PAGE_SIZE, MOBA_DH), F32),
                pltpu.SemaphoreType.DMA((2, 2)),
            ],
        ),
        out_shape=jax.ShapeDtypeStruct((b, t, w), F32),
        compiler_params=_params(("arbitrary", "arbitrary"), 40),
        name="moba_sample",
    )(page_table, sel, q, k_new, v_new, gate, cache_k_pages, cache_v_pages)


def kernel(x_prompt, x_sample, c_prompt, c_sample, state_ret, cache_k, cache_v, page_table, norm_g, w_ada, b_ada,
           w_ret_in, ret_gn_g, w_ret_out, w_moba_in, moba_q_g, moba_k_g, w_moba_out):
    bp, tp, d = x_prompt.shape
    bs, ts, _ = x_sample.shape
    n_pages = page_table.shape[1]
    past_len = n_pages * PAGE_SIZE
    assert past_len % MOBA_BLOCK == 0 and ts <= MOBA_BLOCK and past_len // MOBA_BLOCK >= MOBA_TOPK
    assert w_ada.shape[0] == 2 and w_ret_in.shape[0] == 1 and w_moba_in.shape[0] == 1

    n_c = bp + bs
    c_all = jnp.concatenate([c_prompt, c_sample, jnp.zeros((-n_c % 8, d), F32)], axis=0)
    mod = _ada(c_all, w_ada, b_ada)

    def group_mod(layer):
        mp = mod[layer, :bp].reshape(bp, 1, 3 * d)
        ms = jnp.repeat(mod[layer, bp:n_c], ts, axis=0).reshape(1, bs * ts, 3 * d)
        return [(m[..., :d], m[..., d:2 * d], m[..., 2 * d:]) for m in (mp, ms)]

    xs = x_sample.reshape(1, bs * ts, d)
    ck = cache_k[0]
    cv = cache_v[0]

    tm_proj = 256
    n_steps = bp * (tp // tm_proj)
    n_blocks_past = bs * n_pages // PAGES_PER_BLOCK
    assert n_pages % PAGES_PER_BLOCK == 0 and n_blocks_past % n_steps == 0 and n_blocks_past // n_steps >= 2
    bps = n_blocks_past // n_steps
    bps_ret = min(bps - 1, -(-bps * 5 // 8))
    stream_ret = _KmStream(0, bps_ret * PAGES_PER_BLOCK, n_steps, n_pages)
    stream_moba = _KmStream(bps_ret * PAGES_PER_BLOCK * n_steps, (bps - bps_ret) * PAGES_PER_BLOCK, n_steps, n_pages)
    w_ret_in_b = w_ret_in[0].astype(BF16)

    (sh_p, sc_p, gt_p), (sh_s, sc_s, gt_s) = group_mod(0)
    proj_p, km_a = _ret_proj(x_prompt, norm_g[0], sh_p, sc_p, w_ret_in_b, BF16, tm=tm_proj,
                             km=(page_table, ck, stream_ret))
    og_p, ret_p = _ret_prompt(proj_p, ret_gn_g[0], rows_per_step=256)
    y_p = _out_proj(og_p, w_ret_out[0], x_prompt, gt_p, tm=512)

    proj_s = _ret_proj(xs, norm_g[0], sh_s, sc_s, w_ret_in_b, F32, tm=bs * ts).reshape(bs, ts, -1)
    og_s, ret_s = _ret_sample(proj_s, state_ret[0], ret_gn_g[0])
    y_s = _out_proj(og_s.reshape(1, bs * ts, RET_V).astype(BF16), w_ret_out[0], xs, gt_s, tm=bs * ts)

    (sh_p, sc_p, gt_p), (sh_s, sc_s, gt_s) = group_mod(1)
    q_p, k_p, v_p, g_p, km_b = _moba_proj(y_p, norm_g[1], sh_p, sc_p, w_moba_in[0], moba_q_g[0], moba_k_g[0], BF16,
                                          tm=tm_proj, km=(page_table, ck, stream_moba))
    oa_p = _moba_prompt(q_p, k_p, v_p, g_p)
    y_p = _out_proj(oa_p, w_moba_out[0], y_p, gt_p, tm=512)

    q_s, k_s, v_s, g_s = _moba_proj(y_s, norm_g[1], sh_s, sc_s, w_moba_in[0], moba_q_g[0], moba_k_g[0], F32,
                                    tm=bs * ts)
    q_s, k_s, v_s, g_s = (a.reshape(bs, ts, MOBA_W) for a in (q_s, k_s, v_s, g_s))
    kmean = jnp.concatenate([km_a, km_b], axis=0).reshape(bs, past_len // MOBA_BLOCK, MOBA_HEADS, MOBA_DH)
    sel = _moba_select(q_s, kmean)[:, :, :MOBA_TOPK].reshape(bs, MOBA_HEADS * ts * MOBA_TOPK)
    oa_s = _moba_sample(q_s, k_s, v_s, g_s, ck, cv, page_table, sel, past_len)
    y_s = _out_proj(oa_s.reshape(1, bs * ts, MOBA_W).astype(BF16), w_moba_out[0], y_s, gt_s, tm=bs * ts)

    hd = (MOBA_HEADS, MOBA_DH)
    return (y_p, y_s.reshape(bs, ts, d), ret_p[None], ret_s[None],
            k_p.reshape(1, bp, tp, *hd), v_p.reshape(1, bp, tp, *hd),
            k_s.reshape(1, bs, ts, *hd), v_s.reshape(1, bs, ts, *hd))
```

```python
import functools
import math
from typing import NamedTuple

import jax
import jax.numpy as jnp
from jax import lax
from jax.experimental import pallas as pl
from jax.experimental.pallas import tpu as pltpu

F32 = jnp.float32
BF16 = jnp.bfloat16
I32 = jnp.int32

RET_HEADS = 4
RET_DK = 256
RET_DV = 512
RET_CHUNK = 128
RET_QK = RET_HEADS * RET_DK
RET_V = RET_HEADS * RET_DV
MOBA_HEADS = 8
MOBA_DH = 128
MOBA_W = MOBA_HEADS * MOBA_DH
MOBA_BLOCK = 256
MOBA_TOPK = 3
PAGE_SIZE = 128
EPS = 1e-6
NEG = -1e30
BELOW_NEG = -3e38

V7X_VMEM_BYTES = 64 * 1024 * 1024
NT_DIMS = (((1,), (1,)), ((), ()))


def _params(semantics, vmem_mb):
    assert vmem_mb * 1024 * 1024 < V7X_VMEM_BYTES
    return pltpu.CompilerParams(dimension_semantics=semantics, vmem_limit_bytes=vmem_mb * 1024 * 1024)


def _silu(x):
    return x * jax.nn.sigmoid(x)


def _norm_mod(x, g, sh, sc):
    ms = jnp.mean(x * x, axis=-1, keepdims=True)
    y = x * lax.rsqrt(ms + EPS) * g
    return y * (1.0 + sc) + sh


def _head_rms(x, g):
    ms = jnp.mean(x * x, axis=-1, keepdims=True)
    return x * lax.rsqrt(ms + EPS) * g


def _ret_log_gamma(h):
    return math.log1p(-(2.0 ** (-5.0 - h)))


def _ada_kernel(c_ref, w_ref, b_ref, o_ref):
    s = _silu(c_ref[...]).astype(BF16)
    o_ref[...] = jnp.dot(s, w_ref[...].astype(BF16), preferred_element_type=F32) + b_ref[...]


def _ada(c_all, w_ada, b_ada):
    depth, d, n = w_ada.shape
    m = c_all.shape[0]
    tn = 512
    return pl.pallas_call(
        _ada_kernel,
        grid=(depth, n // tn),
        in_specs=[
            pl.BlockSpec((m, d), lambda l, j: (0, 0)),
            pl.BlockSpec((None, d, tn), lambda l, j: (l, 0, j)),
            pl.BlockSpec((None, 1, tn), lambda l, j: (l, 0, j)),
        ],
        out_specs=pl.BlockSpec((None, m, tn), lambda l, j: (l, 0, j)),
        out_shape=jax.ShapeDtypeStruct((depth, m, n), F32),
        compiler_params=_params(("arbitrary", "arbitrary"), 32),
        name="ada",
    )(c_all, w_ada, b_ada.reshape(depth, 1, n))


def _mod_spec(mod, tm):
    d = mod.shape[-1]
    if mod.shape[1] == 1:
        return pl.BlockSpec((None, 1, d), lambda g, i, *_: (g, 0, 0))
    return pl.BlockSpec((None, tm, d), lambda g, i, *_: (g, i, 0))


def _weight_spec(shape, index_map, resident):
    if resident:
        return pl.BlockSpec(shape, index_map, pipeline_mode=pl.Buffered(1))
    return pl.BlockSpec(shape, index_map)


def _cast_weight(w_ref, wb_scr, col_chunk):
    for c in range(w_ref.shape[1] // col_chunk):
        cs = slice(c * col_chunk, (c + 1) * col_chunk)
        wb_scr[:, cs] = w_ref[:, cs].astype(BF16)


PAGES_PER_BLOCK = MOBA_BLOCK // PAGE_SIZE


class _KmStream(NamedTuple):
    page0: int
    pages_per_step: int
    n_steps: int
    n_pages: int


def _km_scratch(stream, cache):
    return [pltpu.VMEM((2, stream.pages_per_step, *cache.shape[1:]), F32), pltpu.SemaphoreType.DMA((2,))]


def _km_out(stream, cache, steps_per_group):
    hd = cache.shape[2:]
    blocks_per_step = stream.pages_per_step // PAGES_PER_BLOCK
    spec = pl.BlockSpec((blocks_per_step, *hd), lambda gi, i: (gi * steps_per_group + i, 0, 0))
    return spec, jax.ShapeDtypeStruct((stream.n_steps * blocks_per_step, *hd), F32)


def _km_copy(ck_ref, buf, sems, page, slot, p):
    return pltpu.make_async_copy(ck_ref.at[page], buf.at[slot, p], sems.at[slot])


def _km_start(stream, pt_ref, ck_ref, buf, sems, step, slot):
    for p in range(stream.pages_per_step):
        flat = stream.page0 + step * stream.pages_per_step + p
        page = pt_ref[flat // stream.n_pages, flat % stream.n_pages]
        _km_copy(ck_ref, buf, sems, page, slot, p).start()


def _km_wait(stream, ck_ref, buf, sems, slot):
    for p in range(stream.pages_per_step):
        _km_copy(ck_ref, buf, sems, 0, slot, p).wait()


def _run_with_km_stream(stream, km_refs, chunks):
    if stream is not None:
        pt_ref, ck_ref, km_ref, buf, sems = km_refs
        step = pl.program_id(0) * pl.num_programs(1) + pl.program_id(1)
        slot = step % 2

        @pl.when(step == 0)
        def _():
            _km_start(stream, pt_ref, ck_ref, buf, sems, step, 0)

        _km_wait(stream, ck_ref, buf, sems, slot)
        _km_start(stream, pt_ref, ck_ref, buf, sems, jnp.where(step + 1 == stream.n_steps, 0, step + 1), 1 - slot)
        for blk in range(stream.pages_per_step // PAGES_PER_BLOCK):
            tot = jnp.sum(buf[slot, blk * PAGES_PER_BLOCK], axis=0)
            for p in range(1, PAGES_PER_BLOCK):
                tot = tot + jnp.sum(buf[slot, blk * PAGES_PER_BLOCK + p], axis=0)
            km_ref[blk] = tot / float(MOBA_BLOCK)

    for chunk in chunks:
        chunk()

    if stream is not None:
        @pl.when(step == stream.n_steps - 1)
        def _():
            _km_wait(stream, ck_ref, buf, sems, 1 - slot)


def _ret_proj_kernel(*refs, col_chunk, stream):
    x_ref, g_ref, sh_ref, sc_ref, w_ref = refs[:5]
    if stream is None:
        (o_ref,) = refs[5:]
        km_refs = None
    else:
        pt_ref, ck_ref, o_ref, km_ref, buf, sems = refs[5:]
        km_refs = (pt_ref, ck_ref, km_ref, buf, sems)

    h = _norm_mod(x_ref[...], g_ref[...], sh_ref[...], sc_ref[...]).astype(BF16)

    def column_chunk(c):
        cs = slice(c * col_chunk, (c + 1) * col_chunk)
        o_ref[:, cs] = jnp.dot(h, w_ref[:, cs], preferred_element_type=F32).astype(o_ref.dtype)

    _run_with_km_stream(stream, km_refs,
                        [functools.partial(column_chunk, c) for c in range(w_ref.shape[1] // col_chunk)])


def _ret_proj(x, norm_g, sh, sc, w, out_dtype, tm, km=None):
    g, r, d = x.shape
    n = w.shape[1]
    assert w.dtype == BF16
    in_specs = [
        pl.BlockSpec((None, tm, d), lambda gi, i: (gi, i, 0)),
        pl.BlockSpec((1, d), lambda gi, i: (0, 0)),
        _mod_spec(sh, tm),
        _mod_spec(sc, tm),
        _weight_spec((d, n), lambda gi, i: (0, 0), True),
    ]
    out_specs = [pl.BlockSpec((None, tm, n), lambda gi, i: (gi, i, 0))]
    out_shape = [jax.ShapeDtypeStruct((g, r, n), out_dtype)]
    scratch = []
    args = [x, norm_g.reshape(1, d), sh, sc, w]
    stream = None
    if km is not None:
        page_table, cache, stream = km
        assert stream.n_steps == g * (r // tm)
        in_specs += [pl.BlockSpec(memory_space=pltpu.SMEM), pl.BlockSpec(memory_space=pl.ANY)]
        km_spec, km_shape = _km_out(stream, cache, r // tm)
        out_specs.append(km_spec)
        out_shape.append(km_shape)
        scratch += _km_scratch(stream, cache)
        args += [page_table, cache]
    outs = pl.pallas_call(
        functools.partial(_ret_proj_kernel, col_chunk=512, stream=stream),
        grid=(g, r // tm),
        in_specs=in_specs,
        out_specs=out_specs,
        out_shape=out_shape,
        scratch_shapes=scratch,
        compiler_params=_params(("arbitrary", "arbitrary"), 58),
        name="ret_proj",
    )(*args)
    return outs if km is not None else outs[0]


def _out_proj_kernel(a_ref, w_ref, x_ref, gt_ref, o_ref, wb_scr):
    @pl.when((pl.program_id(0) == 0) & (pl.program_id(1) == 0))
    def _():
        _cast_weight(w_ref, wb_scr, 512)

    o_ref[...] = x_ref[...] + gt_ref[...] * jnp.dot(a_ref[...], wb_scr[...], preferred_element_type=F32)


def _out_proj(a, w, x, gate, tm):
    g, r, k = a.shape
    d = w.shape[1]
    return pl.pallas_call(
        _out_proj_kernel,
        grid=(g, r // tm),
        in_specs=[
            pl.BlockSpec((None, tm, k), lambda gi, i: (gi, i, 0)),
            _weight_spec((k, d), lambda gi, i: (0, 0), True),
            pl.BlockSpec((None, tm, d), lambda gi, i: (gi, i, 0)),
            _mod_spec(gate, tm),
        ],
        out_specs=pl.BlockSpec((None, tm, d), lambda gi, i: (gi, i, 0)),
        out_shape=jax.ShapeDtypeStruct((g, r, d), F32),
        scratch_shapes=[pltpu.VMEM((k, d), BF16)],
        compiler_params=_params(("arbitrary", "arbitrary"), 48),
        name="out_proj",
    )(a, w, x, gate)


def _ret_prompt_kernel(p_ref, gn_ref, og_ref, s_ref, *, chunk, n_chunks):
    @pl.when(pl.program_id(1) == 0)
    def _():
        s_ref[...] = jnp.zeros_like(s_ref)

    ii = lax.broadcasted_iota(I32, (chunk, chunk), 0)
    jj = lax.broadcasted_iota(I32, (chunk, chunk), 1)
    diff = (ii - jj).astype(F32)
    pos = lax.broadcasted_iota(I32, (chunk, 1), 0).astype(F32)
    for h in range(RET_HEADS):
        lg = _ret_log_gamma(h)
        decay = jnp.where(diff >= 0, jnp.exp(lg * jnp.maximum(diff, 0.0)), 0.0)
        q_decay = jnp.exp(lg * (pos + 1.0))
        k_decay = jnp.exp(lg * (chunk - 1.0 - pos))
        chunk_decay = math.exp(lg * chunk)
        for c in range(n_chunks):
            rows = slice(c * chunk, (c + 1) * chunk)
            q = p_ref[rows, h * RET_DK:(h + 1) * RET_DK]
            k = p_ref[rows, RET_QK + h * RET_DK:RET_QK + (h + 1) * RET_DK].astype(F32) * (RET_DK ** -0.5)
            v = p_ref[rows, 2 * RET_QK + h * RET_DV:2 * RET_QK + (h + 1) * RET_DV]
            gate = p_ref[rows, 2 * RET_QK + RET_V + h * RET_DV:2 * RET_QK + RET_V + (h + 1) * RET_DV].astype(F32)
            state = s_ref[h]
            scores = lax.dot_general(q, k.astype(BF16), NT_DIMS, preferred_element_type=F32) * decay
            inner = jnp.dot(scores.astype(BF16), v, preferred_element_type=F32)
            cross = jnp.dot((q.astype(F32) * q_decay).astype(BF16), state.astype(BF16), preferred_element_type=F32)
            kv = lax.dot_general((k * k_decay).astype(BF16), v, (((0,), (0,)), ((), ())), preferred_element_type=F32)
            s_ref[h] = chunk_decay * state + kv
            o = inner + cross
            oc = o - jnp.mean(o, axis=-1, keepdims=True)
            y = oc * lax.rsqrt(jnp.mean(oc * oc, axis=-1, keepdims=True) + EPS) * gn_ref[h:h + 1, :]
            og_ref[rows, h * RET_DV:(h + 1) * RET_DV] = (y * _silu(gate)).astype(og_ref.dtype)


def _ret_prompt(proj, gn_g, rows_per_step):
    b, t, n = proj.shape
    assert t % RET_CHUNK == 0 and rows_per_step % RET_CHUNK == 0
    return pl.pallas_call(
        functools.partial(_ret_prompt_kernel, chunk=RET_CHUNK, n_chunks=rows_per_step // RET_CHUNK),
        grid=(b, t // rows_per_step),
        in_specs=[
            pl.BlockSpec((None, rows_per_step, n), lambda bi, i: (bi, i, 0)),
            pl.BlockSpec((RET_HEADS, RET_DV), lambda bi, i: (0, 0)),
        ],
        out_specs=[
            pl.BlockSpec((None, rows_per_step, RET_V), lambda bi, i: (bi, i, 0)),
            pl.BlockSpec((None, RET_HEADS, RET_DK, RET_DV), lambda bi, i: (bi, 0, 0, 0)),
        ],
        out_shape=[
            jax.ShapeDtypeStruct((b, t, RET_V), BF16),
            jax.ShapeDtypeStruct((b, RET_HEADS, RET_DK, RET_DV), F32),
        ],
        compiler_params=_params(("arbitrary", "arbitrary"), 40),
        name="ret_prompt",
    )(proj, gn_g)


def _ret_sample_kernel(p_ref, s0_ref, gn_ref, og_ref, s_ref, *, t):
    ii = lax.broadcasted_iota(I32, (t, t), 0)
    jj = lax.broadcasted_iota(I32, (t, t), 1)
    diff = (ii - jj).astype(F32)
    pos = lax.broadcasted_iota(I32, (t, 1), 0).astype(F32)
    pad = -t % 8
    for h in range(RET_HEADS):
        lg = _ret_log_gamma(h)
        decay = jnp.where(diff >= 0, jnp.exp(lg * jnp.maximum(diff, 0.0)), 0.0)
        q_decay = jnp.exp(lg * (pos + 1.0))
        k_decay = jnp.exp(lg * (t - 1.0 - pos))
        chunk_decay = math.exp(lg * t)
        q = p_ref[:, h * RET_DK:(h + 1) * RET_DK]
        k = p_ref[:, RET_QK + h * RET_DK:RET_QK + (h + 1) * RET_DK] * (RET_DK ** -0.5)
        v = p_ref[:, 2 * RET_QK + h * RET_DV:2 * RET_QK + (h + 1) * RET_DV]
        gate = p_ref[:, 2 * RET_QK + RET_V + h * RET_DV:2 * RET_QK + RET_V + (h + 1) * RET_DV]
        state = s0_ref[h]
        o = jnp.dot(q * q_decay, state, preferred_element_type=F32)
        for j in range(t):
            s_j = jnp.sum(q * k[j:j + 1, :], axis=-1, keepdims=True) * decay[:, j:j + 1]
            o = o + s_j * v[j:j + 1, :]
        kd = jnp.concatenate([k * k_decay, jnp.zeros((pad, RET_DK), F32)], axis=0)
        vp = jnp.concatenate([v, jnp.zeros((pad, RET_DV), F32)], axis=0)
        kv = lax.dot_general(kd, vp, (((0,), (0,)), ((), ())), preferred_element_type=F32)
        s_ref[h] = chunk_decay * state + kv
        oc = o - jnp.mean(o, axis=-1, keepdims=True)
        y = oc * lax.rsqrt(jnp.mean(oc * oc, axis=-1, keepdims=True) + EPS) * gn_ref[h:h + 1, :]
        og_ref[:, h * RET_DV:(h + 1) * RET_DV] = (y * _silu(gate)).astype(og_ref.dtype)


def _ret_sample(proj, state0, gn_g):
    b, t, n = proj.shape
    return pl.pallas_call(
        functools.partial(_ret_sample_kernel, t=t),
        grid=(b,),
        in_specs=[
            pl.BlockSpec((None, t, n), lambda bi: (bi, 0, 0)),
            pl.BlockSpec((None, RET_HEADS, RET_DK, RET_DV), lambda bi: (bi, 0, 0, 0)),
            pl.BlockSpec((RET_HEADS, RET_DV), lambda bi: (0, 0)),
        ],
        out_specs=[
            pl.BlockSpec((None, t, RET_V), lambda bi: (bi, 0, 0)),
            pl.BlockSpec((None, RET_HEADS, RET_DK, RET_DV), lambda bi: (bi, 0, 0, 0)),
        ],
        out_shape=[
            jax.ShapeDtypeStruct((b, t, RET_V), F32),
            jax.ShapeDtypeStruct((b, RET_HEADS, RET_DK, RET_DV), F32),
        ],
        compiler_params=_params(("arbitrary",), 40),
        name="ret_sample",
    )(proj, state0, gn_g)


def _moba_proj_kernel(*refs, stream):
    x_ref, g_ref, sh_ref, sc_ref, w_ref, qg_ref, kg_ref = refs[:7]
    if stream is None:
        q_ref, k_ref, v_ref, gt_ref, wb_scr = refs[7:]
        km_refs = None
    else:
        pt_ref, ck_ref, q_ref, k_ref, v_ref, gt_ref, km_ref, wb_scr, ring, sems = refs[7:]
        km_refs = (pt_ref, ck_ref, km_ref, ring, sems)

    @pl.when((pl.program_id(0) == 0) & (pl.program_id(1) == 0))
    def _():
        _cast_weight(w_ref, wb_scr, 512)

    h = _norm_mod(x_ref[...], g_ref[...], sh_ref[...], sc_ref[...]).astype(BF16)
    w = MOBA_W

    def q_part():
        q = jnp.dot(h, wb_scr[:, 0:w], preferred_element_type=F32)
        for hd in range(MOBA_HEADS):
            cs = slice(hd * MOBA_DH, (hd + 1) * MOBA_DH)
            q_ref[:, cs] = (_head_rms(q[:, cs], qg_ref[...]) * (MOBA_DH ** -0.5)).astype(q_ref.dtype)

    def k_part():
        k = jnp.dot(h, wb_scr[:, w:2 * w], preferred_element_type=F32)
        for hd in range(MOBA_HEADS):
            cs = slice(hd * MOBA_DH, (hd + 1) * MOBA_DH)
            k_ref[:, cs] = _head_rms(k[:, cs], kg_ref[...])

    def v_part():
        v_ref[...] = jnp.dot(h, wb_scr[:, 2 * w:3 * w], preferred_element_type=F32)

    def gate_part():
        gt_ref[...] = jnp.dot(h, wb_scr[:, 3 * w:4 * w], preferred_element_type=F32).astype(gt_ref.dtype)

    _run_with_km_stream(stream, km_refs, [q_part, k_part, v_part, gate_part])


def _moba_proj(x, norm_g, sh, sc, w, q_g, k_g, act_dtype, tm, km=None):
    g, r, d = x.shape
    n = w.shape[1]
    row_spec = pl.BlockSpec((None, tm, MOBA_W), lambda gi, i: (gi, i, 0))
    in_specs = [
        pl.BlockSpec((None, tm, d), lambda gi, i: (gi, i, 0)),
        pl.BlockSpec((1, d), lambda gi, i: (0, 0)),
        _mod_spec(sh, tm),
        _mod_spec(sc, tm),
        _weight_spec((d, n), lambda gi, i: (0, 0), True),
        pl.BlockSpec((1, MOBA_DH), lambda gi, i: (0, 0)),
        pl.BlockSpec((1, MOBA_DH), lambda gi, i: (0, 0)),
    ]
    out_specs = [row_spec, row_spec, row_spec, row_spec]
    out_shape = [
        jax.ShapeDtypeStruct((g, r, MOBA_W), act_dtype),
        jax.ShapeDtypeStruct((g, r, MOBA_W), F32),
        jax.ShapeDtypeStruct((g, r, MOBA_W), F32),
        jax.ShapeDtypeStruct((g, r, MOBA_W), act_dtype),
    ]
    scratch = [pltpu.VMEM((d, n), BF16)]
    args = [x, norm_g.reshape(1, d), sh, sc, w, q_g.reshape(1, MOBA_DH), k_g.reshape(1, MOBA_DH)]
    stream = None
    if km is not None:
        page_table, cache, stream = km
        assert stream.n_steps == g * (r // tm)
        in_specs += [pl.BlockSpec(memory_space=pltpu.SMEM), pl.BlockSpec(memory_space=pl.ANY)]
        km_spec, km_shape = _km_out(stream, cache, r // tm)
        out_specs.append(km_spec)
        out_shape.append(km_shape)
        scratch += _km_scratch(stream, cache)
        args += [page_table, cache]
    return pl.pallas_call(
        functools.partial(_moba_proj_kernel, stream=stream),
        grid=(g, r // tm),
        in_specs=in_specs,
        out_specs=out_specs,
        out_shape=out_shape,
        scratch_shapes=scratch,
        compiler_params=_params(("arbitrary", "arbitrary"), 56),
        name="moba_proj",
    )(*args)


def _alibi_slope(head):
    hv = jnp.full((1, 1), head, I32).astype(F32)
    return jnp.exp2(-8.0 * (hv + 1.0) / MOBA_HEADS)


AUG_PEN0 = 8


def _moba_prompt_head(q_ref, k_ref, v_ref, g_ref, o_ref, ka_scr, vt_scr, qa_scr, km_scr, *, n_blocks, head, side_work):
    blk = MOBA_BLOCK
    dh = MOBA_DH
    t_len = n_blocks * blk
    slope = _alibi_slope(head)
    pen_rows = km_scr.shape[0] // 4

    lane = lax.broadcasted_iota(I32, (blk, dh), 1)
    k_off = lax.broadcasted_iota(I32, (blk, dh), 0).astype(F32)
    km_scr[...] = jnp.zeros_like(km_scr)
    for n in range(n_blocks):
        rows = slice(n * blk, (n + 1) * blk)
        kn = k_ref[rows, :]
        ka_scr[rows, 0:dh] = kn.astype(BF16)
        aug = jnp.where(lane < 2, 1.0,
                        jnp.where(lane == 2, slope * float(blk * n),
                                  jnp.where(lane == 3, slope * k_off,
                                            jnp.where(lane == AUG_PEN0 + n, 1.0, 0.0))))
        ka_scr[rows, dh:2 * dh] = aug.astype(BF16)
        vt_scr[:, rows] = v_ref[rows, :].T.astype(BF16)
        mean = jnp.mean(kn, axis=0, keepdims=True)
        hi = mean.astype(BF16).astype(F32)
        mid = (mean - hi).astype(BF16).astype(F32)
        lo = (mean - hi - mid).astype(BF16).astype(F32)
        km_scr[n:n + 1, :] = hi
        km_scr[pen_rows + n:pen_rows + n + 1, :] = mid
        km_scr[2 * pen_rows + n:2 * pen_rows + n + 1, :] = lo
        qa_scr[0:dh, rows] = q_ref[rows, :].astype(F32).T.astype(BF16)

    parts = jnp.dot(km_scr[...].astype(BF16), qa_scr[0:dh, :], preferred_element_type=F32)
    gs = parts[0:pen_rows] + parts[pen_rows:2 * pen_rows] + parts[2 * pen_rows:3 * pen_rows]
    nid = lax.broadcasted_iota(I32, (pen_rows, t_len), 0)
    q_pos = lax.broadcasted_iota(I32, (pen_rows, t_len), 1)
    own = lax.shift_right_logical(q_pos, int(math.log2(blk)))
    past = nid < own
    attended = nid == own
    gs = jnp.where(past, gs, NEG)
    for _ in range(min(MOBA_TOPK, n_blocks)):
        mx = jnp.max(gs, axis=0, keepdims=True)
        idx = jnp.min(jnp.where(gs == mx, nid, pen_rows), axis=0, keepdims=True)
        pick = nid == idx
        attended = attended | (pick & past)
        gs = jnp.where(pick, BELOW_NEG, gs)
    pen = jnp.where(attended, 0.0, NEG)
    r8 = lax.broadcasted_iota(I32, (8, t_len), 0)
    q8 = lax.broadcasted_iota(I32, (8, t_len), 1)
    own8 = lax.shift_right_logical(q8, int(math.log2(blk)))
    bias = jnp.where(r8 == 0, -slope * (own8 * blk).astype(F32),
                     jnp.where(r8 == 1, -slope * (q8 - own8 * blk).astype(F32), jnp.where(r8 < 4, 1.0, 0.0)))
    extra = jnp.concatenate([bias, pen, jnp.zeros((dh - 8 - pen_rows, t_len), F32)], axis=0)
    qa_scr[dh:2 * dh, :] = extra.astype(BF16)

    def query_block(own):
        n_keys = (own + 1) * blk
        cols = slice(own * blk, n_keys)
        s = jnp.dot(ka_scr[0:n_keys, :], qa_scr[:, cols], preferred_element_type=F32)
        k_idx = lax.broadcasted_iota(I32, (blk, blk), 0)
        q_idx = lax.broadcasted_iota(I32, (blk, blk), 1)
        s_own = jnp.where(q_idx >= k_idx, s[own * blk:n_keys], NEG)
        m = jnp.max(s_own, axis=0, keepdims=True)
        if own > 0:
            s_past = s[0:own * blk]
            m = jnp.maximum(m, jnp.max(s_past, axis=0, keepdims=True))
        p = jnp.exp(s_own - m)
        l = jnp.sum(p, axis=0, keepdims=True)
        p = p.astype(BF16)
        if own > 0:
            p_past = jnp.exp(s_past - m)
            l = l + jnp.sum(p_past, axis=0, keepdims=True)
            p = jnp.concatenate([p_past.astype(BF16), p], axis=0)
        acc = jnp.dot(vt_scr[:, 0:n_keys], p, preferred_element_type=F32)
        o = (acc / l).T
        o_ref[cols, :] = (o * _silu(g_ref[cols, :].astype(F32))).astype(o_ref.dtype)

    for own in range(n_blocks):
        for work in side_work.get(own, ()):
            work()
        query_block(own)


class _SampleAttn(NamedTuple):
    units: int
    t: int
    past_len: int
    n_sel: int


def _sample_slab_copies(ck_ref, cv_ref, kbuf, vbuf, sems, page, head, slot, j):
    return (pltpu.make_async_copy(ck_ref.at[page, :, head, :], kbuf.at[slot, j], sems.at[0, slot]),
            pltpu.make_async_copy(cv_ref.at[page, :, head, :], vbuf.at[slot, j], sems.at[1, slot]))


def _sample_start(cfg, pt_ref, sel_ref, ck_ref, cv_ref, kbuf, vbuf, sems, step, slot):
    for u in range(cfg.units):
        unit = step * cfg.units + u
        sb = unit // MOBA_HEADS
        sh = unit % MOBA_HEADS
        for i in range(cfg.t):
            for s in range(MOBA_TOPK):
                blk_id = sel_ref[sb, (sh * cfg.t + i) * MOBA_TOPK + s]
                for p in range(PAGES_PER_BLOCK):
                    page = pt_ref[sb, blk_id * PAGES_PER_BLOCK + p]
                    j = u * cfg.n_sel + (i * MOBA_TOPK + s) * PAGES_PER_BLOCK + p
                    for c in _sample_slab_copies(ck_ref, cv_ref, kbuf, vbuf, sems, page, sh, slot, j):
                        c.start()


def _sample_wait(cfg, ck_ref, cv_ref, kbuf, vbuf, sems, slot):
    for j in range(cfg.units * cfg.n_sel):
        for c in _sample_slab_copies(ck_ref, cv_ref, kbuf, vbuf, sems, 0, 0, slot, j):
            c.wait()


def _sample_unit_stages(cfg, sel_ref, q_ref, kn_ref, vn_ref, g_ref, o_ref, kbuf, vbuf, slot, u, sb, sh):
    t = cfg.t
    cols = slice(u * MOBA_DH, (u + 1) * MOBA_DH)
    keys_per_query = cfg.n_sel // t * PAGE_SIZE
    n_keys = cfg.n_sel * PAGE_SIZE
    vals = {}

    def scores():
        slope = _alibi_slope(sh)
        q = q_ref[:, cols]
        kn = kn_ref[:, cols]
        k_all = kbuf[slot, u * cfg.n_sel:(u + 1) * cfg.n_sel].reshape(n_keys, MOBA_DH)
        off = lax.broadcasted_iota(I32, (1, PAGE_SIZE), 1)
        k_pos = jnp.concatenate(
            [sel_ref[sb, (sh * t + i) * MOBA_TOPK + s] * MOBA_BLOCK + p * PAGE_SIZE + off
             for i in range(t) for s in range(MOBA_TOPK) for p in range(PAGES_PER_BLOCK)], axis=1)
        row = lax.broadcasted_iota(I32, (t, n_keys), 0)
        col = lax.broadcasted_iota(I32, (t, n_keys), 1)
        owned = (col >= row * keys_per_query) & (col < (row + 1) * keys_per_query)
        dist = (cfg.past_len + row) - k_pos
        s_sel = lax.dot_general(q, k_all, NT_DIMS, preferred_element_type=F32) - slope * dist.astype(F32)
        vals["s_sel"] = jnp.where(owned & (dist >= 0), s_sel, NEG)
        s_new = jnp.concatenate([jnp.sum(q * kn[j:j + 1, :], axis=-1, keepdims=True) for j in range(t)], axis=1)
        d_new = lax.broadcasted_iota(I32, (t, t), 0) - lax.broadcasted_iota(I32, (t, t), 1)
        vals["s_new"] = jnp.where(d_new >= 0, s_new - slope * d_new.astype(F32), NEG)

    def softmax():
        s_sel, s_new = vals["s_sel"], vals["s_new"]
        m = jnp.maximum(jnp.max(s_sel, axis=1, keepdims=True), jnp.max(s_new, axis=1, keepdims=True))
        vals["p_sel"] = jnp.exp(s_sel - m)
        vals["p_new"] = jnp.exp(s_new - m)
        vals["l"] = jnp.sum(vals["p_sel"], axis=1, keepdims=True) + jnp.sum(vals["p_new"], axis=1, keepdims=True)

    def values():
        vn = vn_ref[:, cols]
        v_all = vbuf[slot, u * cfg.n_sel:(u + 1) * cfg.n_sel].reshape(n_keys, MOBA_DH)
        o = jnp.dot(vals["p_sel"], v_all, preferred_element_type=F32)
        for j in range(t):
            o = o + vals["p_new"][:, j:j + 1] * vn[j:j + 1, :]
        o_ref[:, cols] = (o / vals["l"]) * _silu(g_ref[:, cols])

    return scores, softmax, values


def _moba_attn_kernel(pt_ref, sel_ref, q_ref, k_ref, v_ref, g_ref, qs_ref, kn_ref, vn_ref, gs_ref, ck_ref, cv_ref,
                      o_ref, os_ref, ka_scr, vt_scr, qa_scr, km_scr, kbuf, vbuf, sems, *, n_blocks, cfg):
    head = pl.program_id(1)
    n_steps = pl.num_programs(0) * pl.num_programs(1)
    step = pl.program_id(0) * pl.num_programs(1) + head
    slot = step % 2

    @pl.when(step == 0)
    def _():
        _sample_start(cfg, pt_ref, sel_ref, ck_ref, cv_ref, kbuf, vbuf, sems, step, 0)

    _sample_wait(cfg, ck_ref, cv_ref, kbuf, vbuf, sems, slot)
    _sample_start(cfg, pt_ref, sel_ref, ck_ref, cv_ref, kbuf, vbuf, sems, jnp.where(step + 1 == n_steps, 0, step + 1),
                  1 - slot)

    unit0 = step * cfg.units
    stages = [_sample_unit_stages(cfg, sel_ref, qs_ref, kn_ref, vn_ref, gs_ref, os_ref, kbuf, vbuf, slot, u,
                                  unit0 // MOBA_HEADS, unit0 % MOBA_HEADS + u) for u in range(cfg.units)]
    gap = max(1, n_blocks // 4)
    side_work = {k * gap: [unit[k] for unit in stages] for k in range(3)}
    _moba_prompt_head(q_ref, k_ref, v_ref, g_ref, o_ref, ka_scr, vt_scr, qa_scr, km_scr, n_blocks=n_blocks, head=head,
                      side_work=side_work)

    @pl.when(step == n_steps - 1)
    def _():
        _sample_wait(cfg, ck_ref, cv_ref, kbuf, vbuf, sems, 1 - slot)


def _moba_attn(q, k, v, gate, q_s, k_s, v_s, g_s, cache_k_pages, cache_v_pages, page_table, sel, past_len):
    b, t, w = q.shape
    bs, ts, _ = q_s.shape
    assert t % MOBA_BLOCK == 0
    n_blocks = t // MOBA_BLOCK
    pen_rows = -(-n_blocks // 8) * 8
    assert AUG_PEN0 + pen_rows <= MOBA_DH
    assert MOBA_BLOCK & (MOBA_BLOCK - 1) == 0
    n_steps = b * MOBA_HEADS
    assert (bs * MOBA_HEADS) % n_steps == 0
    units = bs * MOBA_HEADS // n_steps
    assert MOBA_HEADS % units == 0 and n_blocks >= 3
    cfg = _SampleAttn(units=units, t=ts, past_len=past_len, n_sel=ts * MOBA_TOPK * PAGES_PER_BLOCK)
    steps_per_sb = MOBA_HEADS // units
    spec = pl.BlockSpec((None, t, MOBA_DH), lambda bi, h: (bi, 0, h))
    sspec = pl.BlockSpec((None, ts, units * MOBA_DH),
                         lambda bi, h: ((bi * MOBA_HEADS + h) // steps_per_sb, 0, (bi * MOBA_HEADS + h) % steps_per_sb))
    smem = pl.BlockSpec(memory_space=pltpu.SMEM)
    hbm = pl.BlockSpec(memory_space=pl.ANY)
    slab = (2, units * cfg.n_sel, PAGE_SIZE, MOBA_DH)
    return pl.pallas_call(
        functools.partial(_moba_attn_kernel, n_blocks=n_blocks, cfg=cfg),
        grid=(b, MOBA_HEADS),
        in_specs=[smem, smem, spec, spec, spec, spec, sspec, sspec, sspec, sspec, hbm, hbm],
        out_specs=[spec, sspec],
        out_shape=[jax.ShapeDtypeStruct((b, t, w), BF16), jax.ShapeDtypeStruct((bs, ts, w), F32)],
        scratch_shapes=[
            pltpu.VMEM((t, 2 * MOBA_DH), BF16),
            pltpu.VMEM((MOBA_DH, t), BF16),
            pltpu.VMEM((2 * MOBA_DH, t), BF16),
            pltpu.VMEM((4 * pen_rows, MOBA_DH), F32),
            pltpu.VMEM(slab, F32),
            pltpu.VMEM(slab, F32),
            pltpu.SemaphoreType.DMA((2, 2)),
        ],
        compiler_params=_params(("arbitrary", "arbitrary"), 56),
        name="moba_attn",
    )(page_table, sel, q, k, v, gate, q_s, k_s, v_s, g_s, cache_k_pages, cache_v_pages)


def _moba_select_kernel(q_ref, km_ref, sel_ref, *, t, topk):
    n_blocks = km_ref.shape[0]
    rows = sel_ref.shape[0]
    rid = lax.broadcasted_iota(I32, (rows, 128), 0)
    lid = lax.broadcasted_iota(I32, (rows, 128), 1)
    nid = lax.broadcasted_iota(I32, (n_blocks, 1), 0)
    out = jnp.zeros((rows, 128), I32)
    for h in range(MOBA_HEADS):
        cs = slice(h * MOBA_DH, (h + 1) * MOBA_DH)
        km = km_ref[:, h, :]
        for i in range(t):
            gs = jnp.sum(km * q_ref[i:i + 1, cs], axis=-1, keepdims=True)
            for r in range(topk):
                mx = jnp.max(gs, axis=0, keepdims=True)
                idx = jnp.min(jnp.where(gs == mx, nid, n_blocks), axis=0, keepdims=True)
                out = jnp.where((rid == h * t + i) & (lid == r), idx, out)
                gs = jnp.where(nid == idx, BELOW_NEG, gs)
    sel_ref[...] = out


def _moba_select(q, kmean):
    b, t, w = q.shape
    n_blocks = kmean.shape[1]
    rows = MOBA_HEADS * t
    return pl.pallas_call(
        functools.partial(_moba_select_kernel, t=t, topk=MOBA_TOPK),
        grid=(b,),
        in_specs=[
            pl.BlockSpec((None, t, w), lambda bi: (bi, 0, 0)),
            pl.BlockSpec((None, n_blocks, MOBA_HEADS, MOBA_DH), lambda bi: (bi, 0, 0, 0)),
        ],
        out_specs=pl.BlockSpec((None, rows, 128), lambda bi: (bi, 0, 0)),
        out_shape=jax.ShapeDtypeStruct((b, rows, 128), I32),
        compiler_params=_params(("arbitrary",), 32),
        name="moba_select",
    )(q, kmean)


def kernel(x_prompt, x_sample, c_prompt, c_sample, state_ret, cache_k, cache_v, page_table, norm_g, w_ada, b_ada,
           w_ret_in, ret_gn_g, w_ret_out, w_moba_in, moba_q_g, moba_k_g, w_moba_out):
    bp, tp, d = x_prompt.shape
    bs, ts, _ = x_sample.shape
    n_pages = page_table.shape[1]
    past_len = n_pages * PAGE_SIZE
    assert past_len % MOBA_BLOCK == 0 and ts <= MOBA_BLOCK and past_len // MOBA_BLOCK >= MOBA_TOPK
    assert w_ada.shape[0] == 2 and w_ret_in.shape[0] == 1 and w_moba_in.shape[0] == 1

    n_c = bp + bs
    c_all = jnp.concatenate([c_prompt, c_sample, jnp.zeros((-n_c % 8, d), F32)], axis=0)
    mod = _ada(c_all, w_ada, b_ada)

    def group_mod(layer):
        mp = mod[layer, :bp].reshape(bp, 1, 3 * d)
        ms = jnp.repeat(mod[layer, bp:n_c], ts, axis=0).reshape(1, bs * ts, 3 * d)
        return [(m[..., :d], m[..., d:2 * d], m[..., 2 * d:]) for m in (mp, ms)]

    xs = x_sample.reshape(1, bs * ts, d)
    ck = cache_k[0]
    cv = cache_v[0]

    tm_proj = 256
    n_steps = bp * (tp // tm_proj)
    n_blocks_past = bs * n_pages // PAGES_PER_BLOCK
    assert n_pages % PAGES_PER_BLOCK == 0 and n_blocks_past % n_steps == 0 and n_blocks_past // n_steps >= 2
    bps = n_blocks_past // n_steps
    bps_ret = min(bps - 1, -(-bps * 5 // 8))
    stream_ret = _KmStream(0, bps_ret * PAGES_PER_BLOCK, n_steps, n_pages)
    stream_moba = _KmStream(bps_ret * PAGES_PER_BLOCK * n_steps, (bps - bps_ret) * PAGES_PER_BLOCK, n_steps, n_pages)
    w_ret_in_b = w_ret_in[0].astype(BF16)

    (sh_p, sc_p, gt_p), (sh_s, sc_s, gt_s) = group_mod(0)
    proj_p, km_a = _ret_proj(x_prompt, norm_g[0], sh_p, sc_p, w_ret_in_b, BF16, tm=tm_proj,
                             km=(page_table, ck, stream_ret))
    og_p, ret_p = _ret_prompt(proj_p, ret_gn_g[0], rows_per_step=256)
    y_p = _out_proj(og_p, w_ret_out[0], x_prompt, gt_p, tm=512)

    proj_s = _ret_proj(xs, norm_g[0], sh_s, sc_s, w_ret_in_b, F32, tm=bs * ts).reshape(bs, ts, -1)
    og_s, ret_s = _ret_sample(proj_s, state_ret[0], ret_gn_g[0])
    y_s = _out_proj(og_s.reshape(1, bs * ts, RET_V).astype(BF16), w_ret_out[0], xs, gt_s, tm=bs * ts)

    (sh_p, sc_p, gt_p), (sh_s, sc_s, gt_s) = group_mod(1)
    q_p, k_p, v_p, g_p, km_b = _moba_proj(y_p, norm_g[1], sh_p, sc_p, w_moba_in[0], moba_q_g[0], moba_k_g[0], BF16,
                                          tm=tm_proj, km=(page_table, ck, stream_moba))
    q_s, k_s, v_s, g_s = _moba_proj(y_s, norm_g[1], sh_s, sc_s, w_moba_in[0], moba_q_g[0], moba_k_g[0], F32,
                                    tm=bs * ts)
    q_s, k_s, v_s, g_s = (a.reshape(bs, ts, MOBA_W) for a in (q_s, k_s, v_s, g_s))
    kmean = jnp.concatenate([km_a, km_b], axis=0).reshape(bs, past_len // MOBA_BLOCK, MOBA_HEADS, MOBA_DH)
    sel = _moba_select(q_s, kmean)[:, :, :MOBA_TOPK].reshape(bs, MOBA_HEADS * ts * MOBA_TOPK)
    oa_p, oa_s = _moba_attn(q_p, k_p, v_p, g_p, q_s, k_s, v_s, g_s, ck, cv, page_table, sel, past_len)
    y_p = _out_proj(oa_p, w_moba_out[0], y_p, gt_p, tm=512)
    y_s = _out_proj(oa_s.reshape(1, bs * ts, MOBA_W).astype(BF16), w_moba_out[0], y_s, gt_s, tm=bs * ts)

    hd = (MOBA_HEADS, MOBA_DH)
    return (y_p, y_s.reshape(bs, ts, d), ret_p[None], ret_s[None],
            k_p.reshape(1, bp, tp, *hd), v_p.reshape(1, bp, tp, *hd),
            k_s.reshape(1, bs, ts, *hd), v_s.reshape(1, bs, ts, *hd))
```

```python
import functools
import math
from typing import NamedTuple

import jax
import jax.numpy as jnp
from jax import lax
from jax.experimental import pallas as pl
from jax.experimental.pallas import tpu as pltpu

F32 = jnp.float32
BF16 = jnp.bfloat16
I32 = jnp.int32

RET_HEADS = 4
RET_DK = 256
RET_DV = 512
RET_CHUNK = 128
RET_QK = RET_HEADS * RET_DK
RET_V = RET_HEADS * RET_DV
MOBA_HEADS = 8
MOBA_DH = 128
MOBA_W = MOBA_HEADS * MOBA_DH
MOBA_BLOCK = 256
MOBA_TOPK = 3
PAGE_SIZE = 128
EPS = 1e-6
NEG = -1e30
BELOW_NEG = -3e38

V7X_VMEM_BYTES = 64 * 1024 * 1024
NT_DIMS = (((1,), (1,)), ((), ()))


def _params(semantics, vmem_mb):
    assert vmem_mb * 1024 * 1024 < V7X_VMEM_BYTES
    return pltpu.CompilerParams(dimension_semantics=semantics, vmem_limit_bytes=vmem_mb * 1024 * 1024)


def _silu(x):
    return x * jax.nn.sigmoid(x)


def _norm_mod(x, g, sh, sc):
    ms = jnp.mean(x * x, axis=-1, keepdims=True)
    y = x * lax.rsqrt(ms + EPS) * g
    return y * (1.0 + sc) + sh


def _head_rms(x, g):
    ms = jnp.mean(x * x, axis=-1, keepdims=True)
    return x * lax.rsqrt(ms + EPS) * g


def _ret_log_gamma(h):
    return math.log1p(-(2.0 ** (-5.0 - h)))


def _ada_kernel(c_ref, w_ref, b_ref, o_ref):
    s = _silu(c_ref[...]).astype(BF16)
    o_ref[...] = jnp.dot(s, w_ref[...].astype(BF16), preferred_element_type=F32) + b_ref[...]


def _ada(c_all, w_ada, b_ada):
    depth, d, n = w_ada.shape
    m = c_all.shape[0]
    tn = 512
    return pl.pallas_call(
        _ada_kernel,
        grid=(depth, n // tn),
        in_specs=[
            pl.BlockSpec((m, d), lambda l, j: (0, 0)),
            pl.BlockSpec((None, d, tn), lambda l, j: (l, 0, j)),
            pl.BlockSpec((None, 1, tn), lambda l, j: (l, 0, j)),
        ],
        out_specs=pl.BlockSpec((None, m, tn), lambda l, j: (l, 0, j)),
        out_shape=jax.ShapeDtypeStruct((depth, m, n), F32),
        compiler_params=_params(("arbitrary", "arbitrary"), 32),
        name="ada",
    )(c_all, w_ada, b_ada.reshape(depth, 1, n))


def _mod_spec(mod, tm):
    d = mod.shape[-1]
    if mod.shape[1] == 1:
        return pl.BlockSpec((None, 1, d), lambda g, i, *_: (g, 0, 0))
    return pl.BlockSpec((None, tm, d), lambda g, i, *_: (g, i, 0))


def _weight_spec(shape, index_map, resident):
    if resident:
        return pl.BlockSpec(shape, index_map, pipeline_mode=pl.Buffered(1))
    return pl.BlockSpec(shape, index_map)


def _cast_weight(w_ref, wb_scr, col_chunk):
    for c in range(w_ref.shape[1] // col_chunk):
        cs = slice(c * col_chunk, (c + 1) * col_chunk)
        wb_scr[:, cs] = w_ref[:, cs].astype(BF16)


PAGES_PER_BLOCK = MOBA_BLOCK // PAGE_SIZE


class _KmStream(NamedTuple):
    page0: int
    pages_per_step: int
    n_steps: int
    n_pages: int


def _km_scratch(stream, cache):
    return [pltpu.VMEM((2, stream.pages_per_step, *cache.shape[1:]), F32), pltpu.SemaphoreType.DMA((2,))]


def _km_out(stream, cache, steps_per_group):
    hd = cache.shape[2:]
    blocks_per_step = stream.pages_per_step // PAGES_PER_BLOCK
    spec = pl.BlockSpec((blocks_per_step, *hd), lambda gi, i: (gi * steps_per_group + i, 0, 0))
    return spec, jax.ShapeDtypeStruct((stream.n_steps * blocks_per_step, *hd), F32)


def _km_copy(ck_ref, buf, sems, page, slot, p):
    return pltpu.make_async_copy(ck_ref.at[page], buf.at[slot, p], sems.at[slot])


def _km_start(stream, pt_ref, ck_ref, buf, sems, step, slot):
    for p in range(stream.pages_per_step):
        flat = stream.page0 + step * stream.pages_per_step + p
        page = pt_ref[flat // stream.n_pages, flat % stream.n_pages]
        _km_copy(ck_ref, buf, sems, page, slot, p).start()


def _km_wait(stream, ck_ref, buf, sems, slot):
    for p in range(stream.pages_per_step):
        _km_copy(ck_ref, buf, sems, 0, slot, p).wait()


def _run_with_km_stream(stream, km_refs, chunks):
    if stream is not None:
        pt_ref, ck_ref, km_ref, buf, sems = km_refs
        step = pl.program_id(0) * pl.num_programs(1) + pl.program_id(1)
        slot = step % 2

        @pl.when(step == 0)
        def _():
            _km_start(stream, pt_ref, ck_ref, buf, sems, step, 0)

        _km_wait(stream, ck_ref, buf, sems, slot)
        _km_start(stream, pt_ref, ck_ref, buf, sems, jnp.where(step + 1 == stream.n_steps, 0, step + 1), 1 - slot)
        for blk in range(stream.pages_per_step // PAGES_PER_BLOCK):
            tot = jnp.sum(buf[slot, blk * PAGES_PER_BLOCK], axis=0)
            for p in range(1, PAGES_PER_BLOCK):
                tot = tot + jnp.sum(buf[slot, blk * PAGES_PER_BLOCK + p], axis=0)
            km_ref[blk] = tot / float(MOBA_BLOCK)

    for chunk in chunks:
        chunk()

    if stream is not None:
        @pl.when(step == stream.n_steps - 1)
        def _():
            _km_wait(stream, ck_ref, buf, sems, 1 - slot)


def _ret_proj_kernel(*refs, col_chunk, stream):
    x_ref, g_ref, sh_ref, sc_ref, w_ref = refs[:5]
    if stream is None:
        (o_ref,) = refs[5:]
        km_refs = None
    else:
        pt_ref, ck_ref, o_ref, km_ref, buf, sems = refs[5:]
        km_refs = (pt_ref, ck_ref, km_ref, buf, sems)

    h = _norm_mod(x_ref[...], g_ref[...], sh_ref[...], sc_ref[...]).astype(BF16)

    def column_chunk(c):
        cs = slice(c * col_chunk, (c + 1) * col_chunk)
        o_ref[:, cs] = jnp.dot(h, w_ref[:, cs], preferred_element_type=F32).astype(o_ref.dtype)

    _run_with_km_stream(stream, km_refs,
                        [functools.partial(column_chunk, c) for c in range(w_ref.shape[1] // col_chunk)])


def _ret_proj(x, norm_g, sh, sc, w, out_dtype, tm, km=None):
    g, r, d = x.shape
    n = w.shape[1]
    assert w.dtype == BF16
    in_specs = [
        pl.BlockSpec((None, tm, d), lambda gi, i: (gi, i, 0)),
        pl.BlockSpec((1, d), lambda gi, i: (0, 0)),
        _mod_spec(sh, tm),
        _mod_spec(sc, tm),
        _weight_spec((d, n), lambda gi, i: (0, 0), True),
    ]
    out_specs = [pl.BlockSpec((None, tm, n), lambda gi, i: (gi, i, 0))]
    out_shape = [jax.ShapeDtypeStruct((g, r, n), out_dtype)]
    scratch = []
    args = [x, norm_g.reshape(1, d), sh, sc, w]
    stream = None
    if km is not None:
        page_table, cache, stream = km
        assert stream.n_steps == g * (r // tm)
        in_specs += [pl.BlockSpec(memory_space=pltpu.SMEM), pl.BlockSpec(memory_space=pl.ANY)]
        km_spec, km_shape = _km_out(stream, cache, r // tm)
        out_specs.append(km_spec)
        out_shape.append(km_shape)
        scratch += _km_scratch(stream, cache)
        args += [page_table, cache]
    outs = pl.pallas_call(
        functools.partial(_ret_proj_kernel, col_chunk=512, stream=stream),
        grid=(g, r // tm),
        in_specs=in_specs,
        out_specs=out_specs,
        out_shape=out_shape,
        scratch_shapes=scratch,
        compiler_params=_params(("arbitrary", "arbitrary"), 58),
        name="ret_proj",
    )(*args)
    return outs if km is not None else outs[0]


def _out_proj_kernel(a_ref, w_ref, x_ref, gt_ref, o_ref, wb_scr):
    @pl.when((pl.program_id(0) == 0) & (pl.program_id(1) == 0))
    def _():
        _cast_weight(w_ref, wb_scr, 512)

    o_ref[...] = x_ref[...] + gt_ref[...] * jnp.dot(a_ref[...], wb_scr[...], preferred_element_type=F32)


def _out_proj(a, w, x, gate, tm):
    g, r, k = a.shape
    d = w.shape[1]
    return pl.pallas_call(
        _out_proj_kernel,
        grid=(g, r // tm),
        in_specs=[
            pl.BlockSpec((None, tm, k), lambda gi, i: (gi, i, 0)),
            _weight_spec((k, d), lambda gi, i: (0, 0), True),
            pl.BlockSpec((None, tm, d), lambda gi, i: (gi, i, 0)),
            _mod_spec(gate, tm),
        ],
        out_specs=pl.BlockSpec((None, tm, d), lambda gi, i: (gi, i, 0)),
        out_shape=jax.ShapeDtypeStruct((g, r, d), F32),
        scratch_shapes=[pltpu.VMEM((k, d), BF16)],
        compiler_params=_params(("arbitrary", "arbitrary"), 48),
        name="out_proj",
    )(a, w, x, gate)


def _pick(index, values):
    out = values[0]
    for k in range(1, len(values)):
        out = jnp.where(index == k, values[k], out)
    return out


def _ret_sample_heads(q_ref, k_ref, v_ref, g_ref, s0_ref, gn_ref, og_ref, s_ref, *, t, hps, part):
    ii = lax.broadcasted_iota(I32, (t, t), 0)
    jj = lax.broadcasted_iota(I32, (t, t), 1)
    diff = (ii - jj).astype(F32)
    pos = lax.broadcasted_iota(I32, (t, 1), 0).astype(F32)
    pad = -t % 8
    for j in range(hps):
        heads = [p * hps + j for p in range(RET_HEADS // hps)]
        lg = _pick(part, [_ret_log_gamma(h) for h in heads])
        chunk_decay = _pick(part, [math.exp(_ret_log_gamma(h) * t) for h in heads])
        decay = jnp.where(diff >= 0, jnp.exp(lg * jnp.maximum(diff, 0.0)), 0.0)
        q_decay = jnp.exp(lg * (pos + 1.0))
        k_decay = jnp.exp(lg * (t - 1.0 - pos))
        q = q_ref[:, j * RET_DK:(j + 1) * RET_DK]
        k = k_ref[:, j * RET_DK:(j + 1) * RET_DK] * (RET_DK ** -0.5)
        v = v_ref[:, j * RET_DV:(j + 1) * RET_DV]
        gate = g_ref[:, j * RET_DV:(j + 1) * RET_DV]
        state = s0_ref[j]
        o = jnp.dot(q * q_decay, state, preferred_element_type=F32)
        for i in range(t):
            s_i = jnp.sum(q * k[i:i + 1, :], axis=-1, keepdims=True) * decay[:, i:i + 1]
            o = o + s_i * v[i:i + 1, :]
        kd = jnp.concatenate([k * k_decay, jnp.zeros((pad, RET_DK), F32)], axis=0)
        vp = jnp.concatenate([v, jnp.zeros((pad, RET_DV), F32)], axis=0)
        kv = lax.dot_general(kd, vp, (((0,), (0,)), ((), ())), preferred_element_type=F32)
        s_ref[j] = chunk_decay * state + kv
        oc = o - jnp.mean(o, axis=-1, keepdims=True)
        y = oc * lax.rsqrt(jnp.mean(oc * oc, axis=-1, keepdims=True) + EPS) * gn_ref[pl.ds(part * hps + j, 1), :]
        og_ref[:, j * RET_DV:(j + 1) * RET_DV] = (y * _silu(gate)).astype(og_ref.dtype)


def _ret_mixer_kernel(p_ref, x_ref, gt_ref, gn_ref, w_ref, qs_ref, ks_ref, vs_ref, gs_ref, s0_ref,
                      y_ref, s_ref, ogs_ref, ss_ref, wb_scr, og_ref, *, chunk, n_chunks, t_s, hps):
    step = pl.program_id(0) * pl.num_programs(1) + pl.program_id(1)

    @pl.when(step == 0)
    def _():
        _cast_weight(w_ref, wb_scr, 512)

    @pl.when(pl.program_id(1) == 0)
    def _():
        s_ref[...] = jnp.zeros_like(s_ref)

    _ret_sample_heads(qs_ref, ks_ref, vs_ref, gs_ref, s0_ref, gn_ref, ogs_ref, ss_ref, t=t_s, hps=hps,
                      part=step % (RET_HEADS // hps))

    ii = lax.broadcasted_iota(I32, (chunk, chunk), 0)
    jj = lax.broadcasted_iota(I32, (chunk, chunk), 1)
    diff = (ii - jj).astype(F32)
    pos = lax.broadcasted_iota(I32, (chunk, 1), 0).astype(F32)
    for h in range(RET_HEADS):
        lg = _ret_log_gamma(h)
        decay = jnp.where(diff >= 0, jnp.exp(lg * jnp.maximum(diff, 0.0)), 0.0)
        q_decay = jnp.exp(lg * (pos + 1.0))
        k_decay = jnp.exp(lg * (chunk - 1.0 - pos))
        chunk_decay = math.exp(lg * chunk)
        for c in range(n_chunks):
            rows = slice(c * chunk, (c + 1) * chunk)
            q = p_ref[rows, h * RET_DK:(h + 1) * RET_DK]
            k = p_ref[rows, RET_QK + h * RET_DK:RET_QK + (h + 1) * RET_DK].astype(F32) * (RET_DK ** -0.5)
            v = p_ref[rows, 2 * RET_QK + h * RET_DV:2 * RET_QK + (h + 1) * RET_DV]
            gate = p_ref[rows, 2 * RET_QK + RET_V + h * RET_DV:2 * RET_QK + RET_V + (h + 1) * RET_DV].astype(F32)
            state = s_ref[h]
            scores = lax.dot_general(q, k.astype(BF16), NT_DIMS, preferred_element_type=F32) * decay
            inner = jnp.dot(scores.astype(BF16), v, preferred_element_type=F32)
            cross = jnp.dot((q.astype(F32) * q_decay).astype(BF16), state.astype(BF16), preferred_element_type=F32)
            kv = lax.dot_general((k * k_decay).astype(BF16), v, (((0,), (0,)), ((), ())), preferred_element_type=F32)
            s_ref[h] = chunk_decay * state + kv
            o = inner + cross
            oc = o - jnp.mean(o, axis=-1, keepdims=True)
            y = oc * lax.rsqrt(jnp.mean(oc * oc, axis=-1, keepdims=True) + EPS) * gn_ref[h:h + 1, :]
            og_ref[rows, h * RET_DV:(h + 1) * RET_DV] = (y * _silu(gate)).astype(og_ref.dtype)

    y_ref[...] = x_ref[...] + gt_ref[...] * jnp.dot(og_ref[...], wb_scr[...], preferred_element_type=F32)


def _ret_mixer(proj, x, gate, gn_g, w_out, proj_s, state0, rows_per_step):
    b, t, n = proj.shape
    d = x.shape[-1]
    bs, ts, _ = proj_s.shape
    assert t % RET_CHUNK == 0 and rows_per_step % RET_CHUNK == 0
    n_t = t // rows_per_step
    n_steps = b * n_t
    assert (bs * RET_HEADS) % n_steps == 0
    hps = bs * RET_HEADS // n_steps
    assert RET_HEADS % hps == 0
    per_sb = RET_HEADS // hps

    def sample_spec(width, first_col):
        first = first_col // (hps * width)
        return pl.BlockSpec((None, ts, hps * width),
                            lambda bi, i: ((bi * n_t + i) // per_sb, 0, first + (bi * n_t + i) % per_sb))

    state_s_spec = pl.BlockSpec((None, hps, RET_DK, RET_DV),
                                lambda bi, i: ((bi * n_t + i) // per_sb, (bi * n_t + i) % per_sb, 0, 0))
    return pl.pallas_call(
        functools.partial(_ret_mixer_kernel, chunk=RET_CHUNK, n_chunks=rows_per_step // RET_CHUNK, t_s=ts, hps=hps),
        grid=(b, n_t),
        in_specs=[
            pl.BlockSpec((None, rows_per_step, n), lambda bi, i: (bi, i, 0)),
            pl.BlockSpec((None, rows_per_step, d), lambda bi, i: (bi, i, 0)),
            _mod_spec(gate, rows_per_step),
            pl.BlockSpec((RET_HEADS, RET_DV), lambda bi, i: (0, 0)),
            _weight_spec((RET_V, d), lambda bi, i: (0, 0), True),
            sample_spec(RET_DK, 0),
            sample_spec(RET_DK, RET_QK),
            sample_spec(RET_DV, 2 * RET_QK),
            sample_spec(RET_DV, 2 * RET_QK + RET_V),
            state_s_spec,
        ],
        out_specs=[
            pl.BlockSpec((None, rows_per_step, d), lambda bi, i: (bi, i, 0)),
            pl.BlockSpec((None, RET_HEADS, RET_DK, RET_DV), lambda bi, i: (bi, 0, 0, 0)),
            sample_spec(RET_DV, 0),
            state_s_spec,
        ],
        out_shape=[
            jax.ShapeDtypeStruct((b, t, d), F32),
            jax.ShapeDtypeStruct((b, RET_HEADS, RET_DK, RET_DV), F32),
            jax.ShapeDtypeStruct((bs, ts, RET_V), F32),
            jax.ShapeDtypeStruct((bs, RET_HEADS, RET_DK, RET_DV), F32),
        ],
        scratch_shapes=[pltpu.VMEM((RET_V, d), BF16), pltpu.VMEM((rows_per_step, RET_V), BF16)],
        compiler_params=_params(("arbitrary", "arbitrary"), 48),
        name="ret_mixer",
    )(proj, x, gate, gn_g, w_out, proj_s, proj_s, proj_s, proj_s, state0)


def _moba_proj_kernel(*refs, stream):
    x_ref, g_ref, sh_ref, sc_ref, w_ref, qg_ref, kg_ref = refs[:7]
    if stream is None:
        q_ref, k_ref, v_ref, gt_ref, wb_scr = refs[7:]
        km_refs = None
    else:
        pt_ref, ck_ref, q_ref, k_ref, v_ref, gt_ref, km_ref, wb_scr, ring, sems = refs[7:]
        km_refs = (pt_ref, ck_ref, km_ref, ring, sems)

    @pl.when((pl.program_id(0) == 0) & (pl.program_id(1) == 0))
    def _():
        _cast_weight(w_ref, wb_scr, 512)

    h = _norm_mod(x_ref[...], g_ref[...], sh_ref[...], sc_ref[...]).astype(BF16)
    w = MOBA_W

    def q_part():
        q = jnp.dot(h, wb_scr[:, 0:w], preferred_element_type=F32)
        for hd in range(MOBA_HEADS):
            cs = slice(hd * MOBA_DH, (hd + 1) * MOBA_DH)
            q_ref[:, cs] = (_head_rms(q[:, cs], qg_ref[...]) * (MOBA_DH ** -0.5)).astype(q_ref.dtype)

    def k_part():
        k = jnp.dot(h, wb_scr[:, w:2 * w], preferred_element_type=F32)
        for hd in range(MOBA_HEADS):
            cs = slice(hd * MOBA_DH, (hd + 1) * MOBA_DH)
            k_ref[:, cs] = _head_rms(k[:, cs], kg_ref[...])

    def v_part():
        v_ref[...] = jnp.dot(h, wb_scr[:, 2 * w:3 * w], preferred_element_type=F32)

    def gate_part():
        gt_ref[...] = jnp.dot(h, wb_scr[:, 3 * w:4 * w], preferred_element_type=F32).astype(gt_ref.dtype)

    _run_with_km_stream(stream, km_refs, [q_part, k_part, v_part, gate_part])


def _moba_proj(x, norm_g, sh, sc, w, q_g, k_g, act_dtype, tm, km=None):
    g, r, d = x.shape
    n = w.shape[1]
    row_spec = pl.BlockSpec((None, tm, MOBA_W), lambda gi, i: (gi, i, 0))
    in_specs = [
        pl.BlockSpec((None, tm, d), lambda gi, i: (gi, i, 0)),
        pl.BlockSpec((1, d), lambda gi, i: (0, 0)),
        _mod_spec(sh, tm),
        _mod_spec(sc, tm),
        _weight_spec((d, n), lambda gi, i: (0, 0), True),
        pl.BlockSpec((1, MOBA_DH), lambda gi, i: (0, 0)),
        pl.BlockSpec((1, MOBA_DH), lambda gi, i: (0, 0)),
    ]
    out_specs = [row_spec, row_spec, row_spec, row_spec]
    out_shape = [
        jax.ShapeDtypeStruct((g, r, MOBA_W), act_dtype),
        jax.ShapeDtypeStruct((g, r, MOBA_W), F32),
        jax.ShapeDtypeStruct((g, r, MOBA_W), F32),
        jax.ShapeDtypeStruct((g, r, MOBA_W), act_dtype),
    ]
    scratch = [pltpu.VMEM((d, n), BF16)]
    args = [x, norm_g.reshape(1, d), sh, sc, w, q_g.reshape(1, MOBA_DH), k_g.reshape(1, MOBA_DH)]
    stream = None
    if km is not None:
        page_table, cache, stream = km
        assert stream.n_steps == g * (r // tm)
        in_specs += [pl.BlockSpec(memory_space=pltpu.SMEM), pl.BlockSpec(memory_space=pl.ANY)]
        km_spec, km_shape = _km_out(stream, cache, r // tm)
        out_specs.append(km_spec)
        out_shape.append(km_shape)
        scratch += _km_scratch(stream, cache)
        args += [page_table, cache]
    return pl.pallas_call(
        functools.partial(_moba_proj_kernel, stream=stream),
        grid=(g, r // tm),
        in_specs=in_specs,
        out_specs=out_specs,
        out_shape=out_shape,
        scratch_shapes=scratch,
        compiler_params=_params(("arbitrary", "arbitrary"), 56),
        name="moba_proj",
    )(*args)


def _alibi_slope(head):
    hv = jnp.full((1, 1), head, I32).astype(F32)
    return jnp.exp2(-8.0 * (hv + 1.0) / MOBA_HEADS)


AUG_PEN0 = 8


def _moba_prompt_head(q_ref, k_ref, v_ref, g_ref, o_ref, ka_scr, vt_scr, qa_scr, km_scr, *, n_blocks, head, side_work):
    blk = MOBA_BLOCK
    dh = MOBA_DH
    t_len = n_blocks * blk
    slope = _alibi_slope(head)
    pen_rows = km_scr.shape[0] // 4

    lane = lax.broadcasted_iota(I32, (blk, dh), 1)
    k_off = lax.broadcasted_iota(I32, (blk, dh), 0).astype(F32)
    km_scr[...] = jnp.zeros_like(km_scr)
    for n in range(n_blocks):
        rows = slice(n * blk, (n + 1) * blk)
        kn = k_ref[rows, :]
        ka_scr[rows, 0:dh] = kn.astype(BF16)
        aug = jnp.where(lane < 2, 1.0,
                        jnp.where(lane == 2, slope * float(blk * n),
                                  jnp.where(lane == 3, slope * k_off,
                                            jnp.where(lane == AUG_PEN0 + n, 1.0, 0.0))))
        ka_scr[rows, dh:2 * dh] = aug.astype(BF16)
        vt_scr[:, rows] = v_ref[rows, :].T.astype(BF16)
        mean = jnp.mean(kn, axis=0, keepdims=True)
        hi = mean.astype(BF16).astype(F32)
        mid = (mean - hi).astype(BF16).astype(F32)
        lo = (mean - hi - mid).astype(BF16).astype(F32)
        km_scr[n:n + 1, :] = hi
        km_scr[pen_rows + n:pen_rows + n + 1, :] = mid
        km_scr[2 * pen_rows + n:2 * pen_rows + n + 1, :] = lo
        qa_scr[0:dh, rows] = q_ref[rows, :].astype(F32).T.astype(BF16)

    parts = jnp.dot(km_scr[...].astype(BF16), qa_scr[0:dh, :], preferred_element_type=F32)
    gs = parts[0:pen_rows] + parts[pen_rows:2 * pen_rows] + parts[2 * pen_rows:3 * pen_rows]
    nid = lax.broadcasted_iota(I32, (pen_rows, t_len), 0)
    q_pos = lax.broadcasted_iota(I32, (pen_rows, t_len), 1)
    own = lax.shift_right_logical(q_pos, int(math.log2(blk)))
    past = nid < own
    attended = nid == own
    gs = jnp.where(past, gs, NEG)
    for _ in range(min(MOBA_TOPK, n_blocks)):
        mx = jnp.max(gs, axis=0, keepdims=True)
        idx = jnp.min(jnp.where(gs == mx, nid, pen_rows), axis=0, keepdims=True)
        pick = nid == idx
        attended = attended | (pick & past)
        gs = jnp.where(pick, BELOW_NEG, gs)
    pen = jnp.where(attended, 0.0, NEG)
    r8 = lax.broadcasted_iota(I32, (8, t_len), 0)
    q8 = lax.broadcasted_iota(I32, (8, t_len), 1)
    own8 = lax.shift_right_logical(q8, int(math.log2(blk)))
    bias = jnp.where(r8 == 0, -slope * (own8 * blk).astype(F32),
                     jnp.where(r8 == 1, -slope * (q8 - own8 * blk).astype(F32), jnp.where(r8 < 4, 1.0, 0.0)))
    extra = jnp.concatenate([bias, pen, jnp.zeros((dh - 8 - pen_rows, t_len), F32)], axis=0)
    qa_scr[dh:2 * dh, :] = extra.astype(BF16)

    def query_block(own):
        n_keys = (own + 1) * blk
        cols = slice(own * blk, n_keys)
        s = jnp.dot(ka_scr[0:n_keys, :], qa_scr[:, cols], preferred_element_type=F32)
        k_idx = lax.broadcasted_iota(I32, (blk, blk), 0)
        q_idx = lax.broadcasted_iota(I32, (blk, blk), 1)
        s_own = jnp.where(q_idx >= k_idx, s[own * blk:n_keys], NEG)
        m = jnp.max(s_own, axis=0, keepdims=True)
        if own > 0:
            s_past = s[0:own * blk]
            m = jnp.maximum(m, jnp.max(s_past, axis=0, keepdims=True))
        p = jnp.exp(s_own - m)
        l = jnp.sum(p, axis=0, keepdims=True)
        p = p.astype(BF16)
        if own > 0:
            p_past = jnp.exp(s_past - m)
            l = l + jnp.sum(p_past, axis=0, keepdims=True)
            p = jnp.concatenate([p_past.astype(BF16), p], axis=0)
        acc = jnp.dot(vt_scr[:, 0:n_keys], p, preferred_element_type=F32)
        o = (acc / l).T
        o_ref[cols, :] = (o * _silu(g_ref[cols, :].astype(F32))).astype(o_ref.dtype)

    for own in range(n_blocks):
        for work in side_work.get(own, ()):
            work()
        query_block(own)


class _SampleAttn(NamedTuple):
    units: int
    t: int
    past_len: int
    n_sel: int


def _sample_slab_copies(ck_ref, cv_ref, kbuf, vbuf, sems, page, head, slot, j):
    return (pltpu.make_async_copy(ck_ref.at[page, :, head, :], kbuf.at[slot, j], sems.at[0, slot]),
            pltpu.make_async_copy(cv_ref.at[page, :, head, :], vbuf.at[slot, j], sems.at[1, slot]))


def _sample_start(cfg, pt_ref, sel_ref, ck_ref, cv_ref, kbuf, vbuf, sems, step, slot):
    for u in range(cfg.units):
        unit = step * cfg.units + u
        sb = unit // MOBA_HEADS
        sh = unit % MOBA_HEADS
        for i in range(cfg.t):
            for s in range(MOBA_TOPK):
                blk_id = sel_ref[sb, (sh * cfg.t + i) * MOBA_TOPK + s]
                for p in range(PAGES_PER_BLOCK):
                    page = pt_ref[sb, blk_id * PAGES_PER_BLOCK + p]
                    j = u * cfg.n_sel + (i * MOBA_TOPK + s) * PAGES_PER_BLOCK + p
                    for c in _sample_slab_copies(ck_ref, cv_ref, kbuf, vbuf, sems, page, sh, slot, j):
                        c.start()


def _sample_wait(cfg, ck_ref, cv_ref, kbuf, vbuf, sems, slot):
    for j in range(cfg.units * cfg.n_sel):
        for c in _sample_slab_copies(ck_ref, cv_ref, kbuf, vbuf, sems, 0, 0, slot, j):
            c.wait()


def _sample_unit_stages(cfg, sel_ref, q_ref, kn_ref, vn_ref, g_ref, o_ref, kbuf, vbuf, slot, u, sb, sh):
    t = cfg.t
    cols = slice(u * MOBA_DH, (u + 1) * MOBA_DH)
    keys_per_query = cfg.n_sel // t * PAGE_SIZE
    n_keys = cfg.n_sel * PAGE_SIZE
    vals = {}

    def scores():
        slope = _alibi_slope(sh)
        q = q_ref[:, cols]
        kn = kn_ref[:, cols]
        k_all = kbuf[slot, u * cfg.n_sel:(u + 1) * cfg.n_sel].reshape(n_keys, MOBA_DH)
        off = lax.broadcasted_iota(I32, (1, PAGE_SIZE), 1)
        k_pos = jnp.concatenate(
            [sel_ref[sb, (sh * t + i) * MOBA_TOPK + s] * MOBA_BLOCK + p * PAGE_SIZE + off
             for i in range(t) for s in range(MOBA_TOPK) for p in range(PAGES_PER_BLOCK)], axis=1)
        row = lax.broadcasted_iota(I32, (t, n_keys), 0)
        col = lax.broadcasted_iota(I32, (t, n_keys), 1)
        owned = (col >= row * keys_per_query) & (col < (row + 1) * keys_per_query)
        dist = (cfg.past_len + row) - k_pos
        s_sel = lax.dot_general(q, k_all, NT_DIMS, preferred_element_type=F32) - slope * dist.astype(F32)
        vals["s_sel"] = jnp.where(owned & (dist >= 0), s_sel, NEG)
        s_new = jnp.concatenate([jnp.sum(q * kn[j:j + 1, :], axis=-1, keepdims=True) for j in range(t)], axis=1)
        d_new = lax.broadcasted_iota(I32, (t, t), 0) - lax.broadcasted_iota(I32, (t, t), 1)
        vals["s_new"] = jnp.where(d_new >= 0, s_new - slope * d_new.astype(F32), NEG)

    def softmax():
        s_sel, s_new = vals["s_sel"], vals["s_new"]
        m = jnp.maximum(jnp.max(s_sel, axis=1, keepdims=True), jnp.max(s_new, axis=1, keepdims=True))
        vals["p_sel"] = jnp.exp(s_sel - m)
        vals["p_new"] = jnp.exp(s_new - m)
        vals["l"] = jnp.sum(vals["p_sel"], axis=1, keepdims=True) + jnp.sum(vals["p_new"], axis=1, keepdims=True)

    def values():
        vn = vn_ref[:, cols]
        v_all = vbuf[slot, u * cfg.n_sel:(u + 1) * cfg.n_sel].reshape(n_keys, MOBA_DH)
        o = jnp.dot(vals["p_sel"], v_all, preferred_element_type=F32)
        for j in range(t):
            o = o + vals["p_new"][:, j:j + 1] * vn[j:j + 1, :]
        o_ref[:, cols] = (o / vals["l"]) * _silu(g_ref[:, cols])

    return scores, softmax, values


def _moba_attn_kernel(pt_ref, sel_ref, q_ref, k_ref, v_ref, g_ref, qs_ref, kn_ref, vn_ref, gs_ref, ck_ref, cv_ref,
                      o_ref, os_ref, ka_scr, vt_scr, qa_scr, km_scr, kbuf, vbuf, sems, *, n_blocks, cfg):
    head = pl.program_id(1)
    n_steps = pl.num_programs(0) * pl.num_programs(1)
    step = pl.program_id(0) * pl.num_programs(1) + head
    slot = step % 2

    @pl.when(step == 0)
    def _():
        _sample_start(cfg, pt_ref, sel_ref, ck_ref, cv_ref, kbuf, vbuf, sems, step, 0)

    _sample_wait(cfg, ck_ref, cv_ref, kbuf, vbuf, sems, slot)
    _sample_start(cfg, pt_ref, sel_ref, ck_ref, cv_ref, kbuf, vbuf, sems, jnp.where(step + 1 == n_steps, 0, step + 1),
                  1 - slot)

    unit0 = step * cfg.units
    stages = [_sample_unit_stages(cfg, sel_ref, qs_ref, kn_ref, vn_ref, gs_ref, os_ref, kbuf, vbuf, slot, u,
                                  unit0 // MOBA_HEADS, unit0 % MOBA_HEADS + u) for u in range(cfg.units)]
    gap = max(1, n_blocks // 4)
    side_work = {k * gap: [unit[k] for unit in stages] for k in range(3)}
    _moba_prompt_head(q_ref, k_ref, v_ref, g_ref, o_ref, ka_scr, vt_scr, qa_scr, km_scr, n_blocks=n_blocks, head=head,
                      side_work=side_work)

    @pl.when(step == n_steps - 1)
    def _():
        _sample_wait(cfg, ck_ref, cv_ref, kbuf, vbuf, sems, 1 - slot)


def _moba_attn(q, k, v, gate, q_s, k_s, v_s, g_s, cache_k_pages, cache_v_pages, page_table, sel, past_len):
    b, t, w = q.shape
    bs, ts, _ = q_s.shape
    assert t % MOBA_BLOCK == 0
    n_blocks = t // MOBA_BLOCK
    pen_rows = -(-n_blocks // 8) * 8
    assert AUG_PEN0 + pen_rows <= MOBA_DH
    assert MOBA_BLOCK & (MOBA_BLOCK - 1) == 0
    n_steps = b * MOBA_HEADS
    assert (bs * MOBA_HEADS) % n_steps == 0
    units = bs * MOBA_HEADS // n_steps
    assert MOBA_HEADS % units == 0 and n_blocks >= 3
    cfg = _SampleAttn(units=units, t=ts, past_len=past_len, n_sel=ts * MOBA_TOPK * PAGES_PER_BLOCK)
    steps_per_sb = MOBA_HEADS // units
    spec = pl.BlockSpec((None, t, MOBA_DH), lambda bi, h: (bi, 0, h))
    sspec = pl.BlockSpec((None, ts, units * MOBA_DH),
                         lambda bi, h: ((bi * MOBA_HEADS + h) // steps_per_sb, 0, (bi * MOBA_HEADS + h) % steps_per_sb))
    smem = pl.BlockSpec(memory_space=pltpu.SMEM)
    hbm = pl.BlockSpec(memory_space=pl.ANY)
    slab = (2, units * cfg.n_sel, PAGE_SIZE, MOBA_DH)
    return pl.pallas_call(
        functools.partial(_moba_attn_kernel, n_blocks=n_blocks, cfg=cfg),
        grid=(b, MOBA_HEADS),
        in_specs=[smem, smem, spec, spec, spec, spec, sspec, sspec, sspec, sspec, hbm, hbm],
        out_specs=[spec, sspec],
        out_shape=[jax.ShapeDtypeStruct((b, t, w), BF16), jax.ShapeDtypeStruct((bs, ts, w), F32)],
        scratch_shapes=[
            pltpu.VMEM((t, 2 * MOBA_DH), BF16),
            pltpu.VMEM((MOBA_DH, t), BF16),
            pltpu.VMEM((2 * MOBA_DH, t), BF16),
            pltpu.VMEM((4 * pen_rows, MOBA_DH), F32),
            pltpu.VMEM(slab, F32),
            pltpu.VMEM(slab, F32),
            pltpu.SemaphoreType.DMA((2, 2)),
        ],
        compiler_params=_params(("arbitrary", "arbitrary"), 56),
        name="moba_attn",
    )(page_table, sel, q, k, v, gate, q_s, k_s, v_s, g_s, cache_k_pages, cache_v_pages)


def _moba_select_kernel(q_ref, km_ref, sel_ref, *, t, topk):
    n_blocks = km_ref.shape[0]
    rows = sel_ref.shape[0]
    rid = lax.broadcasted_iota(I32, (rows, 128), 0)
    lid = lax.broadcasted_iota(I32, (rows, 128), 1)
    nid = lax.broadcasted_iota(I32, (n_blocks, 1), 0)
    out = jnp.zeros((rows, 128), I32)
    for h in range(MOBA_HEADS):
        cs = slice(h * MOBA_DH, (h + 1) * MOBA_DH)
        km = km_ref[:, h, :]
        for i in range(t):
            gs = jnp.sum(km * q_ref[i:i + 1, cs], axis=-1, keepdims=True)
            for r in range(topk):
                mx = jnp.max(gs, axis=0, keepdims=True)
                idx = jnp.min(jnp.where(gs == mx, nid, n_blocks), axis=0, keepdims=True)
                out = jnp.where((rid == h * t + i) & (lid == r), idx, out)
                gs = jnp.where(nid == idx, BELOW_NEG, gs)
    sel_ref[...] = out


def _moba_select(q, kmean):
    b, t, w = q.shape
    n_blocks = kmean.shape[1]
    rows = MOBA_HEADS * t
    return pl.pallas_call(
        functools.partial(_moba_select_kernel, t=t, topk=MOBA_TOPK),
        grid=(b,),
        in_specs=[
            pl.BlockSpec((None, t, w), lambda bi: (bi, 0, 0)),
            pl.BlockSpec((None, n_blocks, MOBA_HEADS, MOBA_DH), lambda bi: (bi, 0, 0, 0)),
        ],
        out_specs=pl.BlockSpec((None, rows, 128), lambda bi: (bi, 0, 0)),
        out_shape=jax.ShapeDtypeStruct((b, rows, 128), I32),
        compiler_params=_params(("arbitrary",), 32),
        name="moba_select",
    )(q, kmean)


def kernel(x_prompt, x_sample, c_prompt, c_sample, state_ret, cache_k, cache_v, page_table, norm_g, w_ada, b_ada,
           w_ret_in, ret_gn_g, w_ret_out, w_moba_in, moba_q_g, moba_k_g, w_moba_out):
    bp, tp, d = x_prompt.shape
    bs, ts, _ = x_sample.shape
    n_pages = page_table.shape[1]
    past_len = n_pages * PAGE_SIZE
    assert past_len % MOBA_BLOCK == 0 and ts <= MOBA_BLOCK and past_len // MOBA_BLOCK >= MOBA_TOPK
    assert w_ada.shape[0] == 2 and w_ret_in.shape[0] == 1 and w_moba_in.shape[0] == 1

    n_c = bp + bs
    c_all = jnp.concatenate([c_prompt, c_sample, jnp.zeros((-n_c % 8, d), F32)], axis=0)
    mod = _ada(c_all, w_ada, b_ada)

    def group_mod(layer):
        mp = mod[layer, :bp].reshape(bp, 1, 3 * d)
        ms = jnp.repeat(mod[layer, bp:n_c], ts, axis=0).reshape(1, bs * ts, 3 * d)
        return [(m[..., :d], m[..., d:2 * d], m[..., 2 * d:]) for m in (mp, ms)]

    xs = x_sample.reshape(1, bs * ts, d)
    ck = cache_k[0]
    cv = cache_v[0]

    tm_proj = 256
    n_steps = bp * (tp // tm_proj)
    n_blocks_past = bs * n_pages // PAGES_PER_BLOCK
    assert n_pages % PAGES_PER_BLOCK == 0 and n_blocks_past % n_steps == 0 and n_blocks_past // n_steps >= 2
    bps = n_blocks_past // n_steps
    bps_ret = min(bps - 1, -(-bps * 5 // 8))
    stream_ret = _KmStream(0, bps_ret * PAGES_PER_BLOCK, n_steps, n_pages)
    stream_moba = _KmStream(bps_ret * PAGES_PER_BLOCK * n_steps, (bps - bps_ret) * PAGES_PER_BLOCK, n_steps, n_pages)
    w_ret_in_b = w_ret_in[0].astype(BF16)

    (sh_p, sc_p, gt_p), (sh_s, sc_s, gt_s) = group_mod(0)
    proj_p, km_a = _ret_proj(x_prompt, norm_g[0], sh_p, sc_p, w_ret_in_b, BF16, tm=tm_proj,
                             km=(page_table, ck, stream_ret))
    proj_s = _ret_proj(xs, norm_g[0], sh_s, sc_s, w_ret_in_b, F32, tm=bs * ts).reshape(bs, ts, -1)
    y_p, ret_p, og_s, ret_s = _ret_mixer(proj_p, x_prompt, gt_p, ret_gn_g[0], w_ret_out[0], proj_s, state_ret[0],
                                         rows_per_step=256)
    y_s = _out_proj(og_s.reshape(1, bs * ts, RET_V).astype(BF16), w_ret_out[0], xs, gt_s, tm=bs * ts)

    (sh_p, sc_p, gt_p), (sh_s, sc_s, gt_s) = group_mod(1)
    q_p, k_p, v_p, g_p, km_b = _moba_proj(y_p, norm_g[1], sh_p, sc_p, w_moba_in[0], moba_q_g[0], moba_k_g[0], BF16,
                                          tm=tm_proj, km=(page_table, ck, stream_moba))
    q_s, k_s, v_s, g_s = _moba_proj(y_s, norm_g[1], sh_s, sc_s, w_moba_in[0], moba_q_g[0], moba_k_g[0], F32,
                                    tm=bs * ts)
    q_s, k_s, v_s, g_s = (a.reshape(bs, ts, MOBA_W) for a in (q_s, k_s, v_s, g_s))
    kmean = jnp.concatenate([km_a, km_b], axis=0).reshape(bs, past_len // MOBA_BLOCK, MOBA_HEADS, MOBA_DH)
    sel = _moba_select(q_s, kmean)[:, :, :MOBA_TOPK].reshape(bs, MOBA_HEADS * ts * MOBA_TOPK)
    oa_p, oa_s = _moba_attn(q_p, k_p, v_p, g_p, q_s, k_s, v_s, g_s, ck, cv, page_table, sel, past_len)
    y_p = _out_proj(oa_p, w_moba_out[0], y_p, gt_p, tm=512)
    y_s = _out_proj(oa_s.reshape(1, bs * ts, MOBA_W).astype(BF16), w_moba_out[0], y_s, gt_s, tm=bs * ts)

    hd = (MOBA_HEADS, MOBA_DH)
    return (y_p, y_s.reshape(bs, ts, d), ret_p[None], ret_s[None],
            k_p.reshape(1, bp, tp, *hd), v_p.reshape(1, bp, tp, *hd),
            k_s.reshape(1, bs, ts, *hd), v_s.reshape(1, bs, ts, *hd))
```

```python
import functools
import math
from typing import NamedTuple

import jax
import jax.numpy as jnp
from jax import lax
from jax.experimental import pallas as pl
from jax.experimental.pallas import tpu as pltpu

F32 = jnp.float32
BF16 = jnp.bfloat16
I32 = jnp.int32

RET_HEADS = 4
RET_DK = 256
RET_DV = 512
RET_CHUNK = 128
RET_QK = RET_HEADS * RET_DK
RET_V = RET_HEADS * RET_DV
MOBA_HEADS = 8
MOBA_DH = 128
MOBA_W = MOBA_HEADS * MOBA_DH
MOBA_BLOCK = 256
MOBA_TOPK = 3
PAGE_SIZE = 128
EPS = 1e-6
NEG = -1e30
BELOW_NEG = -3e38

V7X_VMEM_BYTES = 64 * 1024 * 1024
NT_DIMS = (((1,), (1,)), ((), ()))


def _params(semantics, vmem_mb):
    assert vmem_mb * 1024 * 1024 < V7X_VMEM_BYTES
    return pltpu.CompilerParams(dimension_semantics=semantics, vmem_limit_bytes=vmem_mb * 1024 * 1024)


def _silu(x):
    return x * jax.nn.sigmoid(x)


def _norm_mod(x, g, sh, sc):
    ms = jnp.mean(x * x, axis=-1, keepdims=True)
    y = x * lax.rsqrt(ms + EPS) * g
    return y * (1.0 + sc) + sh


def _head_rms(x, g):
    ms = jnp.mean(x * x, axis=-1, keepdims=True)
    return x * lax.rsqrt(ms + EPS) * g


def _ret_log_gamma(h):
    return math.log1p(-(2.0 ** (-5.0 - h)))


def _ada_kernel(c_ref, w_ref, b_ref, o_ref):
    s = _silu(c_ref[...]).astype(BF16)
    o_ref[...] = jnp.dot(s, w_ref[...].astype(BF16), preferred_element_type=F32) + b_ref[...]


def _ada(c_all, w_ada, b_ada):
    depth, d, n = w_ada.shape
    m = c_all.shape[0]
    tn = 512
    return pl.pallas_call(
        _ada_kernel,
        grid=(depth, n // tn),
        in_specs=[
            pl.BlockSpec((m, d), lambda l, j: (0, 0)),
            pl.BlockSpec((None, d, tn), lambda l, j: (l, 0, j)),
            pl.BlockSpec((None, 1, tn), lambda l, j: (l, 0, j)),
        ],
        out_specs=pl.BlockSpec((None, m, tn), lambda l, j: (l, 0, j)),
        out_shape=jax.ShapeDtypeStruct((depth, m, n), F32),
        compiler_params=_params(("arbitrary", "arbitrary"), 32),
        name="ada",
    )(c_all, w_ada, b_ada.reshape(depth, 1, n))


def _mod_spec(mod, tm):
    d = mod.shape[-1]
    if mod.shape[1] == 1:
        return pl.BlockSpec((None, 1, d), lambda g, i, *_: (g, 0, 0))
    return pl.BlockSpec((None, tm, d), lambda g, i, *_: (g, i, 0))


def _weight_spec(shape, index_map, resident):
    if resident:
        return pl.BlockSpec(shape, index_map, pipeline_mode=pl.Buffered(1))
    return pl.BlockSpec(shape, index_map)


def _cast_weight(w_ref, wb_scr, col_chunk):
    for c in range(w_ref.shape[1] // col_chunk):
        cs = slice(c * col_chunk, (c + 1) * col_chunk)
        wb_scr[:, cs] = w_ref[:, cs].astype(BF16)


PAGES_PER_BLOCK = MOBA_BLOCK // PAGE_SIZE


class _KmStream(NamedTuple):
    page0: int
    pages_per_step: int
    n_steps: int
    n_pages: int


def _km_scratch(stream, cache):
    return [pltpu.VMEM((2, stream.pages_per_step, *cache.shape[1:]), F32), pltpu.SemaphoreType.DMA((2,))]


def _km_out(stream, cache, share_of):
    hd = cache.shape[2:]
    blocks_per_step = stream.pages_per_step // PAGES_PER_BLOCK
    spec = pl.BlockSpec((blocks_per_step, *hd), lambda *grid_idx: (share_of(*grid_idx), 0, 0))
    return spec, jax.ShapeDtypeStruct((stream.n_steps * blocks_per_step, *hd), F32)


def _km_copy(ck_ref, buf, sems, page, slot, p):
    return pltpu.make_async_copy(ck_ref.at[page], buf.at[slot, p], sems.at[slot])


def _km_start(stream, pt_ref, ck_ref, buf, sems, step, slot):
    for p in range(stream.pages_per_step):
        flat = stream.page0 + step * stream.pages_per_step + p
        page = pt_ref[flat // stream.n_pages, flat % stream.n_pages]
        _km_copy(ck_ref, buf, sems, page, slot, p).start()


def _km_wait(stream, ck_ref, buf, sems, slot):
    for p in range(stream.pages_per_step):
        _km_copy(ck_ref, buf, sems, 0, slot, p).wait()


def _run_with_km_stream(stream, km_refs, chunks, step=None, n_grid_steps=None):
    if stream is not None:
        pt_ref, ck_ref, km_ref, buf, sems = km_refs
        if step is None:
            step = pl.program_id(0) * pl.num_programs(1) + pl.program_id(1)
            n_grid_steps = stream.n_steps
        slot = step % 2
        last_share = stream.n_steps - 1

        @pl.when(step == 0)
        def _():
            _km_start(stream, pt_ref, ck_ref, buf, sems, step, 0)

        _km_wait(stream, ck_ref, buf, sems, slot)
        nxt = jnp.where(step + 1 == n_grid_steps, 0, jnp.minimum(step + 1, last_share))
        _km_start(stream, pt_ref, ck_ref, buf, sems, nxt, 1 - slot)
        for blk in range(stream.pages_per_step // PAGES_PER_BLOCK):
            tot = jnp.sum(buf[slot, blk * PAGES_PER_BLOCK], axis=0)
            for p in range(1, PAGES_PER_BLOCK):
                tot = tot + jnp.sum(buf[slot, blk * PAGES_PER_BLOCK + p], axis=0)
            km_ref[blk] = tot / float(MOBA_BLOCK)

    for chunk in chunks:
        chunk()

    if stream is not None:
        @pl.when(step == n_grid_steps - 1)
        def _():
            _km_wait(stream, ck_ref, buf, sems, 1 - slot)


def _ret_proj_kernel(x_ref, g_ref, sh_ref, sc_ref, w_ref, o_ref, *, col_chunk):
    h = _norm_mod(x_ref[...], g_ref[...], sh_ref[...], sc_ref[...]).astype(BF16)
    for c in range(w_ref.shape[1] // col_chunk):
        cs = slice(c * col_chunk, (c + 1) * col_chunk)
        o_ref[:, cs] = jnp.dot(h, w_ref[:, cs], preferred_element_type=F32).astype(o_ref.dtype)


def _ret_proj(x, norm_g, sh, sc, w, out_dtype, tm):
    g, r, d = x.shape
    n = w.shape[1]
    assert w.dtype == BF16
    return pl.pallas_call(
        functools.partial(_ret_proj_kernel, col_chunk=512),
        grid=(g, r // tm),
        in_specs=[
            pl.BlockSpec((None, tm, d), lambda gi, i: (gi, i, 0)),
            pl.BlockSpec((1, d), lambda gi, i: (0, 0)),
            _mod_spec(sh, tm),
            _mod_spec(sc, tm),
            _weight_spec((d, n), lambda gi, i: (0, 0), True),
        ],
        out_specs=pl.BlockSpec((None, tm, n), lambda gi, i: (gi, i, 0)),
        out_shape=jax.ShapeDtypeStruct((g, r, n), out_dtype),
        compiler_params=_params(("arbitrary", "arbitrary"), 40),
        name="ret_proj",
    )(x, norm_g.reshape(1, d), sh, sc, w)


def _out_proj_kernel(a_ref, w_ref, x_ref, gt_ref, o_ref, wb_scr):
    @pl.when((pl.program_id(0) == 0) & (pl.program_id(1) == 0))
    def _():
        _cast_weight(w_ref, wb_scr, 512)

    o_ref[...] = x_ref[...] + gt_ref[...] * jnp.dot(a_ref[...], wb_scr[...], preferred_element_type=F32)


def _out_proj(a, w, x, gate, tm):
    g, r, k = a.shape
    d = w.shape[1]
    return pl.pallas_call(
        _out_proj_kernel,
        grid=(g, r // tm),
        in_specs=[
            pl.BlockSpec((None, tm, k), lambda gi, i: (gi, i, 0)),
            _weight_spec((k, d), lambda gi, i: (0, 0), True),
            pl.BlockSpec((None, tm, d), lambda gi, i: (gi, i, 0)),
            _mod_spec(gate, tm),
        ],
        out_specs=pl.BlockSpec((None, tm, d), lambda gi, i: (gi, i, 0)),
        out_shape=jax.ShapeDtypeStruct((g, r, d), F32),
        scratch_shapes=[pltpu.VMEM((k, d), BF16)],
        compiler_params=_params(("arbitrary", "arbitrary"), 48),
        name="out_proj",
    )(a, w, x, gate)


def _pick(index, values):
    out = values[0]
    for k in range(1, len(values)):
        out = jnp.where(index == k, values[k], out)
    return out


def _ret_sample_heads(q_ref, k_ref, v_ref, g_ref, s0_ref, gn_ref, og_ref, s_ref, *, t, hps, part):
    ii = lax.broadcasted_iota(I32, (t, t), 0)
    jj = lax.broadcasted_iota(I32, (t, t), 1)
    diff = (ii - jj).astype(F32)
    pos = lax.broadcasted_iota(I32, (t, 1), 0).astype(F32)
    pad = -t % 8
    for j in range(hps):
        heads = [p * hps + j for p in range(RET_HEADS // hps)]
        lg = _pick(part, [_ret_log_gamma(h) for h in heads])
        chunk_decay = _pick(part, [math.exp(_ret_log_gamma(h) * t) for h in heads])
        decay = jnp.where(diff >= 0, jnp.exp(lg * jnp.maximum(diff, 0.0)), 0.0)
        q_decay = jnp.exp(lg * (pos + 1.0))
        k_decay = jnp.exp(lg * (t - 1.0 - pos))
        q = q_ref[:, j * RET_DK:(j + 1) * RET_DK]
        k = k_ref[:, j * RET_DK:(j + 1) * RET_DK] * (RET_DK ** -0.5)
        v = v_ref[:, j * RET_DV:(j + 1) * RET_DV]
        gate = g_ref[:, j * RET_DV:(j + 1) * RET_DV]
        state = s0_ref[j]
        o = jnp.dot(q * q_decay, state, preferred_element_type=F32)
        for i in range(t):
            s_i = jnp.sum(q * k[i:i + 1, :], axis=-1, keepdims=True) * decay[:, i:i + 1]
            o = o + s_i * v[i:i + 1, :]
        kd = jnp.concatenate([k * k_decay, jnp.zeros((pad, RET_DK), F32)], axis=0)
        vp = jnp.concatenate([v, jnp.zeros((pad, RET_DV), F32)], axis=0)
        kv = lax.dot_general(kd, vp, (((0,), (0,)), ((), ())), preferred_element_type=F32)
        s_ref[j] = chunk_decay * state + kv
        oc = o - jnp.mean(o, axis=-1, keepdims=True)
        y = oc * lax.rsqrt(jnp.mean(oc * oc, axis=-1, keepdims=True) + EPS) * gn_ref[pl.ds(part * hps + j, 1), :]
        og_ref[:, j * RET_DV:(j + 1) * RET_DV] = (y * _silu(gate)).astype(og_ref.dtype)


def _ret_layer_kernel(pt_ref, ck_ref, xc_ref, xp_ref, ng_ref, sh_ref, sc_ref, gt_ref, gn_ref, win_ref, wout_ref,
                      qs_ref, ks_ref, vs_ref, gs_ref, s0_ref,
                      y_ref, s_ref, ogs_ref, ss_ref, km_ref,
                      proj_scr, og_scr, buf, sems, *, chunk, n_chunks, tiles_per_seq, t_s, hps, stream, col_chunk):
    step = pl.program_id(0)
    cur = step % 2
    prev = 1 - cur
    prev_tile = jnp.maximum(step - 1, 0)

    @pl.when(step == 0)
    def _():
        proj_scr[1] = jnp.zeros(proj_scr.shape[1:], proj_scr.dtype)

    @pl.when(prev_tile % tiles_per_seq == 0)
    def _():
        s_ref[...] = jnp.zeros_like(s_ref)

    _ret_sample_heads(qs_ref, ks_ref, vs_ref, gs_ref, s0_ref, gn_ref, ogs_ref, ss_ref, t=t_s, hps=hps,
                      part=prev_tile % (RET_HEADS // hps))

    h_in = _norm_mod(xc_ref[...], ng_ref[...], sh_ref[...], sc_ref[...]).astype(BF16)

    def project(c):
        cs = slice(c * col_chunk, (c + 1) * col_chunk)
        proj_scr[cur, :, cs] = jnp.dot(h_in, win_ref[:, cs], preferred_element_type=F32).astype(proj_scr.dtype)

    ii = lax.broadcasted_iota(I32, (chunk, chunk), 0)
    jj = lax.broadcasted_iota(I32, (chunk, chunk), 1)
    diff = (ii - jj).astype(F32)
    pos = lax.broadcasted_iota(I32, (chunk, 1), 0).astype(F32)

    def retain(h, c):
        lg = _ret_log_gamma(h)
        decay = jnp.where(diff >= 0, jnp.exp(lg * jnp.maximum(diff, 0.0)), 0.0)
        q_decay = jnp.exp(lg * (pos + 1.0))
        k_decay = jnp.exp(lg * (chunk - 1.0 - pos))
        chunk_decay = math.exp(lg * chunk)
        rows = slice(c * chunk, (c + 1) * chunk)
        q = proj_scr[prev, rows, h * RET_DK:(h + 1) * RET_DK]
        k = proj_scr[prev, rows, RET_QK + h * RET_DK:RET_QK + (h + 1) * RET_DK].astype(F32) * (RET_DK ** -0.5)
        v = proj_scr[prev, rows, 2 * RET_QK + h * RET_DV:2 * RET_QK + (h + 1) * RET_DV]
        gate = proj_scr[prev, rows, 2 * RET_QK + RET_V + h * RET_DV:2 * RET_QK + RET_V + (h + 1) * RET_DV].astype(F32)
        state = s_ref[h]
        scores = lax.dot_general(q, k.astype(BF16), NT_DIMS, preferred_element_type=F32) * decay
        inner = jnp.dot(scores.astype(BF16), v, preferred_element_type=F32)
        cross = jnp.dot((q.astype(F32) * q_decay).astype(BF16), state.astype(BF16), preferred_element_type=F32)
        kv = lax.dot_general((k * k_decay).astype(BF16), v, (((0,), (0,)), ((), ())), preferred_element_type=F32)
        s_ref[h] = chunk_decay * state + kv
        o = inner + cross
        oc = o - jnp.mean(o, axis=-1, keepdims=True)
        y = oc * lax.rsqrt(jnp.mean(oc * oc, axis=-1, keepdims=True) + EPS) * gn_ref[h:h + 1, :]
        og_scr[rows, h * RET_DV:(h + 1) * RET_DV] = (y * _silu(gate)).astype(og_scr.dtype)

    def out_project():
        y_ref[...] = xp_ref[...] + gt_ref[...] * jnp.dot(og_scr[...], wout_ref[...], preferred_element_type=F32)

    projections = [functools.partial(project, c) for c in range(win_ref.shape[1] // col_chunk)]
    pieces = [functools.partial(retain, h, c) for h in range(RET_HEADS) for c in range(n_chunks)]
    lead = 2
    work = pieces[:lead]
    for i in range(len(pieces) - lead):
        work += [projections[i], pieces[lead + i]]
    work.append(out_project)
    work += projections[len(pieces) - lead:]
    _run_with_km_stream(stream, (pt_ref, ck_ref, km_ref, buf, sems), work, step=step,
                        n_grid_steps=pl.num_programs(0))


def _ret_layer(x, norm_g, sh, sc, gate, w_in, gn_g, w_out, proj_s, state0, page_table, cache, stream, tm):
    b, t, d = x.shape
    n = w_in.shape[1]
    bs, ts, _ = proj_s.shape
    assert w_in.dtype == BF16 and w_out.dtype == BF16
    assert t % tm == 0 and tm % RET_CHUNK == 0
    n_t = t // tm
    n_tiles = b * n_t
    assert stream.n_steps == n_tiles
    assert (bs * RET_HEADS) % n_tiles == 0
    hps = bs * RET_HEADS // n_tiles
    assert RET_HEADS % hps == 0
    per_sb = RET_HEADS // hps

    def cur(s):
        return jnp.minimum(s, n_tiles - 1)

    def prev(s):
        return jnp.maximum(s - 1, 0)

    def sample_spec(width, first_col):
        first = first_col // (hps * width)
        return pl.BlockSpec((None, ts, hps * width), lambda s: (prev(s) // per_sb, 0, first + prev(s) % per_sb))

    state_s_spec = pl.BlockSpec((None, hps, RET_DK, RET_DV), lambda s: (prev(s) // per_sb, prev(s) % per_sb, 0, 0))
    km_spec, km_shape = _km_out(stream, cache, cur)
    return pl.pallas_call(
        functools.partial(_ret_layer_kernel, chunk=RET_CHUNK, n_chunks=tm // RET_CHUNK, tiles_per_seq=n_t, t_s=ts,
                          hps=hps, stream=stream, col_chunk=512),
        grid=(n_tiles + 1,),
        in_specs=[
            pl.BlockSpec(memory_space=pltpu.SMEM),
            pl.BlockSpec(memory_space=pl.ANY),
            pl.BlockSpec((None, tm, d), lambda s: (cur(s) // n_t, cur(s) % n_t, 0)),
            pl.BlockSpec((None, tm, d), lambda s: (prev(s) // n_t, prev(s) % n_t, 0)),
            pl.BlockSpec((1, d), lambda s: (0, 0)),
            pl.BlockSpec((None, 1, d), lambda s: (cur(s) // n_t, 0, 0)),
            pl.BlockSpec((None, 1, d), lambda s: (cur(s) // n_t, 0, 0)),
            pl.BlockSpec((None, 1, d), lambda s: (prev(s) // n_t, 0, 0)),
            pl.BlockSpec((RET_HEADS, RET_DV), lambda s: (0, 0)),
            _weight_spec((d, n), lambda s: (0, 0), True),
            _weight_spec((RET_V, d), lambda s: (0, 0), True),
            sample_spec(RET_DK, 0),
            sample_spec(RET_DK, RET_QK),
            sample_spec(RET_DV, 2 * RET_QK),
            sample_spec(RET_DV, 2 * RET_QK + RET_V),
            state_s_spec,
        ],
        out_specs=[
            pl.BlockSpec((None, tm, d), lambda s: (prev(s) // n_t, prev(s) % n_t, 0)),
            pl.BlockSpec((None, RET_HEADS, RET_DK, RET_DV), lambda s: (prev(s) // n_t, 0, 0, 0)),
            sample_spec(RET_DV, 0),
            state_s_spec,
            km_spec,
        ],
        out_shape=[
            jax.ShapeDtypeStruct((b, t, d), F32),
            jax.ShapeDtypeStruct((b, RET_HEADS, RET_DK, RET_DV), F32),
            jax.ShapeDtypeStruct((bs, ts, RET_V), F32),
            jax.ShapeDtypeStruct((bs, RET_HEADS, RET_DK, RET_DV), F32),
            km_shape,
        ],
        scratch_shapes=[pltpu.VMEM((2, tm, n), BF16), pltpu.VMEM((tm, RET_V), BF16)] + _km_scratch(stream, cache),
        compiler_params=_params(("arbitrary",), 62),
        name="ret_layer",
    )(page_table, cache, x, x, norm_g.reshape(1, d), sh, sc, gate, gn_g, w_in, w_out,
      proj_s, proj_s, proj_s, proj_s, state0)


def _moba_proj_kernel(*refs, stream):
    x_ref, g_ref, sh_ref, sc_ref, w_ref, qg_ref, kg_ref = refs[:7]
    if stream is None:
        q_ref, k_ref, v_ref, gt_ref, wb_scr = refs[7:]
        km_refs = None
    else:
        pt_ref, ck_ref, q_ref, k_ref, v_ref, gt_ref, km_ref, wb_scr, ring, sems = refs[7:]
        km_refs = (pt_ref, ck_ref, km_ref, ring, sems)

    @pl.when((pl.program_id(0) == 0) & (pl.program_id(1) == 0))
    def _():
        _cast_weight(w_ref, wb_scr, 512)

    h = _norm_mod(x_ref[...], g_ref[...], sh_ref[...], sc_ref[...]).astype(BF16)
    w = MOBA_W

    def q_part():
        q = jnp.dot(h, wb_scr[:, 0:w], preferred_element_type=F32)
        for hd in range(MOBA_HEADS):
            cs = slice(hd * MOBA_DH, (hd + 1) * MOBA_DH)
            q_ref[:, cs] = (_head_rms(q[:, cs], qg_ref[...]) * (MOBA_DH ** -0.5)).astype(q_ref.dtype)

    def k_part():
        k = jnp.dot(h, wb_scr[:, w:2 * w], preferred_element_type=F32)
        for hd in range(MOBA_HEADS):
            cs = slice(hd * MOBA_DH, (hd + 1) * MOBA_DH)
            k_ref[:, cs] = _head_rms(k[:, cs], kg_ref[...])

    def v_part():
        v_ref[...] = jnp.dot(h, wb_scr[:, 2 * w:3 * w], preferred_element_type=F32)

    def gate_part():
        gt_ref[...] = jnp.dot(h, wb_scr[:, 3 * w:4 * w], preferred_element_type=F32).astype(gt_ref.dtype)

    _run_with_km_stream(stream, km_refs, [q_part, k_part, v_part, gate_part])


def _moba_proj(x, norm_g, sh, sc, w, q_g, k_g, act_dtype, tm, km=None):
    g, r, d = x.shape
    n = w.shape[1]
    row_spec = pl.BlockSpec((None, tm, MOBA_W), lambda gi, i: (gi, i, 0))
    in_specs = [
        pl.BlockSpec((None, tm, d), lambda gi, i: (gi, i, 0)),
        pl.BlockSpec((1, d), lambda gi, i: (0, 0)),
        _mod_spec(sh, tm),
        _mod_spec(sc, tm),
        _weight_spec((d, n), lambda gi, i: (0, 0), True),
        pl.BlockSpec((1, MOBA_DH), lambda gi, i: (0, 0)),
        pl.BlockSpec((1, MOBA_DH), lambda gi, i: (0, 0)),
    ]
    out_specs = [row_spec, row_spec, row_spec, row_spec]
    out_shape = [
        jax.ShapeDtypeStruct((g, r, MOBA_W), act_dtype),
        jax.ShapeDtypeStruct((g, r, MOBA_W), F32),
        jax.ShapeDtypeStruct((g, r, MOBA_W), F32),
        jax.ShapeDtypeStruct((g, r, MOBA_W), act_dtype),
    ]
    scratch = [pltpu.VMEM((d, n), BF16)]
    args = [x, norm_g.reshape(1, d), sh, sc, w, q_g.reshape(1, MOBA_DH), k_g.reshape(1, MOBA_DH)]
    stream = None
    if km is not None:
        page_table, cache, stream = km
        assert stream.n_steps == g * (r // tm)
        in_specs += [pl.BlockSpec(memory_space=pltpu.SMEM), pl.BlockSpec(memory_space=pl.ANY)]
        km_spec, km_shape = _km_out(stream, cache, lambda gi, i: gi * (r // tm) + i)
        out_specs.append(km_spec)
        out_shape.append(km_shape)
        scratch += _km_scratch(stream, cache)
        args += [page_table, cache]
    return pl.pallas_call(
        functools.partial(_moba_proj_kernel, stream=stream),
        grid=(g, r // tm),
        in_specs=in_specs,
        out_specs=out_specs,
        out_shape=out_shape,
        scratch_shapes=scratch,
        compiler_params=_params(("arbitrary", "arbitrary"), 56),
        name="moba_proj",
    )(*args)


def _alibi_slope(head):
    hv = jnp.full((1, 1), head, I32).astype(F32)
    return jnp.exp2(-8.0 * (hv + 1.0) / MOBA_HEADS)


AUG_PEN0 = 8


def _moba_prompt_head(q_ref, k_ref, v_ref, g_ref, o_ref, ka_scr, vt_scr, qa_scr, km_scr, *, n_blocks, head, side_work):
    blk = MOBA_BLOCK
    dh = MOBA_DH
    t_len = n_blocks * blk
    slope = _alibi_slope(head)
    pen_rows = km_scr.shape[0] // 4

    lane = lax.broadcasted_iota(I32, (blk, dh), 1)
    k_off = lax.broadcasted_iota(I32, (blk, dh), 0).astype(F32)
    km_scr[...] = jnp.zeros_like(km_scr)
    for n in range(n_blocks):
        rows = slice(n * blk, (n + 1) * blk)
        kn = k_ref[rows, :]
        ka_scr[rows, 0:dh] = kn.astype(BF16)
        aug = jnp.where(lane < 2, 1.0,
                        jnp.where(lane == 2, slope * float(blk * n),
                                  jnp.where(lane == 3, slope * k_off,
                                            jnp.where(lane == AUG_PEN0 + n, 1.0, 0.0))))
        ka_scr[rows, dh:2 * dh] = aug.astype(BF16)
        vt_scr[:, rows] = v_ref[rows, :].T.astype(BF16)
        mean = jnp.mean(kn, axis=0, keepdims=True)
        hi = mean.astype(BF16).astype(F32)
        mid = (mean - hi).astype(BF16).astype(F32)
        lo = (mean - hi - mid).astype(BF16).astype(F32)
        km_scr[n:n + 1, :] = hi
        km_scr[pen_rows + n:pen_rows + n + 1, :] = mid
        km_scr[2 * pen_rows + n:2 * pen_rows + n + 1, :] = lo
        qa_scr[0:dh, rows] = q_ref[rows, :].astype(F32).T.astype(BF16)

    parts = jnp.dot(km_scr[...].astype(BF16), qa_scr[0:dh, :], preferred_element_type=F32)
    gs = parts[0:pen_rows] + parts[pen_rows:2 * pen_rows] + parts[2 * pen_rows:3 * pen_rows]
    nid = lax.broadcasted_iota(I32, (pen_rows, t_len), 0)
    q_pos = lax.broadcasted_iota(I32, (pen_rows, t_len), 1)
    own = lax.shift_right_logical(q_pos, int(math.log2(blk)))
    past = nid < own
    attended = nid == own
    gs = jnp.where(past, gs, NEG)
    for _ in range(min(MOBA_TOPK, n_blocks)):
        mx = jnp.max(gs, axis=0, keepdims=True)
        idx = jnp.min(jnp.where(gs == mx, nid, pen_rows), axis=0, keepdims=True)
        pick = nid == idx
        attended = attended | (pick & past)
        gs = jnp.where(pick, BELOW_NEG, gs)
    pen = jnp.where(attended, 0.0, NEG)
    r8 = lax.broadcasted_iota(I32, (8, t_len), 0)
    q8 = lax.broadcasted_iota(I32, (8, t_len), 1)
    own8 = lax.shift_right_logical(q8, int(math.log2(blk)))
    bias = jnp.where(r8 == 0, -slope * (own8 * blk).astype(F32),
                     jnp.where(r8 == 1, -slope * (q8 - own8 * blk).astype(F32), jnp.where(r8 < 4, 1.0, 0.0)))
    extra = jnp.concatenate([bias, pen, jnp.zeros((dh - 8 - pen_rows, t_len), F32)], axis=0)
    qa_scr[dh:2 * dh, :] = extra.astype(BF16)

    def query_block(own):
        n_keys = (own + 1) * blk
        cols = slice(own * blk, n_keys)
        s = jnp.dot(ka_scr[0:n_keys, :], qa_scr[:, cols], preferred_element_type=F32)
        k_idx = lax.broadcasted_iota(I32, (blk, blk), 0)
        q_idx = lax.broadcasted_iota(I32, (blk, blk), 1)
        s_own = jnp.where(q_idx >= k_idx, s[own * blk:n_keys], NEG)
        m = jnp.max(s_own, axis=0, keepdims=True)
        if own > 0:
            s_past = s[0:own * blk]
            m = jnp.maximum(m, jnp.max(s_past, axis=0, keepdims=True))
        p = jnp.exp(s_own - m)
        l = jnp.sum(p, axis=0, keepdims=True)
        p = p.astype(BF16)
        if own > 0:
            p_past = jnp.exp(s_past - m)
            l = l + jnp.sum(p_past, axis=0, keepdims=True)
            p = jnp.concatenate([p_past.astype(BF16), p], axis=0)
        acc = jnp.dot(vt_scr[:, 0:n_keys], p, preferred_element_type=F32)
        o = (acc / l).T
        o_ref[cols, :] = (o * _silu(g_ref[cols, :].astype(F32))).astype(o_ref.dtype)

    for own in range(n_blocks):
        for work in side_work.get(own, ()):
            work()
        query_block(own)


class _SampleAttn(NamedTuple):
    units: int
    t: int
    past_len: int
    n_sel: int


def _sample_slab_copies(ck_ref, cv_ref, kbuf, vbuf, sems, page, head, slot, j):
    return (pltpu.make_async_copy(ck_ref.at[page, :, head, :], kbuf.at[slot, j], sems.at[0, slot]),
            pltpu.make_async_copy(cv_ref.at[page, :, head, :], vbuf.at[slot, j], sems.at[1, slot]))


def _sample_start(cfg, pt_ref, sel_ref, ck_ref, cv_ref, kbuf, vbuf, sems, step, slot):
    for u in range(cfg.units):
        unit = step * cfg.units + u
        sb = unit // MOBA_HEADS
        sh = unit % MOBA_HEADS
        for i in range(cfg.t):
            for s in range(MOBA_TOPK):
                blk_id = sel_ref[sb, (sh * cfg.t + i) * MOBA_TOPK + s]
                for p in range(PAGES_PER_BLOCK):
                    page = pt_ref[sb, blk_id * PAGES_PER_BLOCK + p]
                    j = u * cfg.n_sel + (i * MOBA_TOPK + s) * PAGES_PER_BLOCK + p
                    for c in _sample_slab_copies(ck_ref, cv_ref, kbuf, vbuf, sems, page, sh, slot, j):
                        c.start()


def _sample_wait(cfg, ck_ref, cv_ref, kbuf, vbuf, sems, slot):
    for j in range(cfg.units * cfg.n_sel):
        for c in _sample_slab_copies(ck_ref, cv_ref, kbuf, vbuf, sems, 0, 0, slot, j):
            c.wait()


def _sample_unit_stages(cfg, sel_ref, q_ref, kn_ref, vn_ref, g_ref, o_ref, kbuf, vbuf, slot, u, sb, sh):
    t = cfg.t
    cols = slice(u * MOBA_DH, (u + 1) * MOBA_DH)
    keys_per_query = cfg.n_sel // t * PAGE_SIZE
    n_keys = cfg.n_sel * PAGE_SIZE
    vals = {}

    def scores():
        slope = _alibi_slope(sh)
        q = q_ref[:, cols]
        kn = kn_ref[:, cols]
        k_all = kbuf[slot, u * cfg.n_sel:(u + 1) * cfg.n_sel].reshape(n_keys, MOBA_DH)
        off = lax.broadcasted_iota(I32, (1, PAGE_SIZE), 1)
        k_pos = jnp.concatenate(
            [sel_ref[sb, (sh * t + i) * MOBA_TOPK + s] * MOBA_BLOCK + p * PAGE_SIZE + off
             for i in range(t) for s in range(MOBA_TOPK) for p in range(PAGES_PER_BLOCK)], axis=1)
        row = lax.broadcasted_iota(I32, (t, n_keys), 0)
        col = lax.broadcasted_iota(I32, (t, n_keys), 1)
        owned = (col >= row * keys_per_query) & (col < (row + 1) * keys_per_query)
        dist = (cfg.past_len + row) - k_pos
        s_sel = lax.dot_general(q, k_all, NT_DIMS, preferred_element_type=F32) - slope * dist.astype(F32)
        vals["s_sel"] = jnp.where(owned & (dist >= 0), s_sel, NEG)
        s_new = jnp.concatenate([jnp.sum(q * kn[j:j + 1, :], axis=-1, keepdims=True) for j in range(t)], axis=1)
        d_new = lax.broadcasted_iota(I32, (t, t), 0) - lax.broadcasted_iota(I32, (t, t), 1)
        vals["s_new"] = jnp.where(d_new >= 0, s_new - slope * d_new.astype(F32), NEG)

    def softmax():
        s_sel, s_new = vals["s_sel"], vals["s_new"]
        m = jnp.maximum(jnp.max(s_sel, axis=1, keepdims=True), jnp.max(s_new, axis=1, keepdims=True))
        vals["p_sel"] = jnp.exp(s_sel - m)
        vals["p_new"] = jnp.exp(s_new - m)
        vals["l"] = jnp.sum(vals["p_sel"], axis=1, keepdims=True) + jnp.sum(vals["p_new"], axis=1, keepdims=True)

    def values():
        vn = vn_ref[:, cols]
        v_all = vbuf[slot, u * cfg.n_sel:(u + 1) * cfg.n_sel].reshape(n_keys, MOBA_DH)
        o = jnp.dot(vals["p_sel"], v_all, preferred_element_type=F32)
        for j in range(t):
            o = o + vals["p_new"][:, j:j + 1] * vn[j:j + 1, :]
        o_ref[:, cols] = (o / vals["l"]) * _silu(g_ref[:, cols])

    return scores, softmax, values


def _moba_attn_kernel(pt_ref, sel_ref, q_ref, k_ref, v_ref, g_ref, qs_ref, kn_ref, vn_ref, gs_ref, ck_ref, cv_ref,
                      o_ref, os_ref, ka_scr, vt_scr, qa_scr, km_scr, kbuf, vbuf, sems, *, n_blocks, cfg):
    head = pl.program_id(1)
    n_steps = pl.num_programs(0) * pl.num_programs(1)
    step = pl.program_id(0) * pl.num_programs(1) + head
    slot = step % 2

    @pl.when(step == 0)
    def _():
        _sample_start(cfg, pt_ref, sel_ref, ck_ref, cv_ref, kbuf, vbuf, sems, step, 0)

    _sample_wait(cfg, ck_ref, cv_ref, kbuf, vbuf, sems, slot)
    _sample_start(cfg, pt_ref, sel_ref, ck_ref, cv_ref, kbuf, vbuf, sems, jnp.where(step + 1 == n_steps, 0, step + 1),
                  1 - slot)

    unit0 = step * cfg.units
    stages = [_sample_unit_stages(cfg, sel_ref, qs_ref, kn_ref, vn_ref, gs_ref, os_ref, kbuf, vbuf, slot, u,
                                  unit0 // MOBA_HEADS, unit0 % MOBA_HEADS + u) for u in range(cfg.units)]
    gap = max(1, n_blocks // 4)
    side_work = {k * gap: [unit[k] for unit in stages] for k in range(3)}
    _moba_prompt_head(q_ref, k_ref, v_ref, g_ref, o_ref, ka_scr, vt_scr, qa_scr, km_scr, n_blocks=n_blocks, head=head,
                      side_work=side_work)

    @pl.when(step == n_steps - 1)
    def _():
        _sample_wait(cfg, ck_ref, cv_ref, kbuf, vbuf, sems, 1 - slot)


def _moba_attn(q, k, v, gate, q_s, k_s, v_s, g_s, cache_k_pages, cache_v_pages, page_table, sel, past_len):
    b, t, w = q.shape
    bs, ts, _ = q_s.shape
    assert t % MOBA_BLOCK == 0
    n_blocks = t // MOBA_BLOCK
    pen_rows = -(-n_blocks // 8) * 8
    assert AUG_PEN0 + pen_rows <= MOBA_DH
    assert MOBA_BLOCK & (MOBA_BLOCK - 1) == 0
    n_steps = b * MOBA_HEADS
    assert (bs * MOBA_HEADS) % n_steps == 0
    units = bs * MOBA_HEADS // n_steps
    assert MOBA_HEADS % units == 0 and n_blocks >= 3
    cfg = _SampleAttn(units=units, t=ts, past_len=past_len, n_sel=ts * MOBA_TOPK * PAGES_PER_BLOCK)
    steps_per_sb = MOBA_HEADS // units
    spec = pl.BlockSpec((None, t, MOBA_DH), lambda bi, h: (bi, 0, h))
    sspec = pl.BlockSpec((None, ts, units * MOBA_DH),
                         lambda bi, h: ((bi * MOBA_HEADS + h) // steps_per_sb, 0, (bi * MOBA_HEADS + h) % steps_per_sb))
    smem = pl.BlockSpec(memory_space=pltpu.SMEM)
    hbm = pl.BlockSpec(memory_space=pl.ANY)
    slab = (2, units * cfg.n_sel, PAGE_SIZE, MOBA_DH)
    return pl.pallas_call(
        functools.partial(_moba_attn_kernel, n_blocks=n_blocks, cfg=cfg),
        grid=(b, MOBA_HEADS),
        in_specs=[smem, smem, spec, spec, spec, spec, sspec, sspec, sspec, sspec, hbm, hbm],
        out_specs=[spec, sspec],
        out_shape=[jax.ShapeDtypeStruct((b, t, w), BF16), jax.ShapeDtypeStruct((bs, ts, w), F32)],
        scratch_shapes=[
            pltpu.VMEM((t, 2 * MOBA_DH), BF16),
            pltpu.VMEM((MOBA_DH, t), BF16),
            pltpu.VMEM((2 * MOBA_DH, t), BF16),
            pltpu.VMEM((4 * pen_rows, MOBA_DH), F32),
            pltpu.VMEM(slab, F32),
            pltpu.VMEM(slab, F32),
            pltpu.SemaphoreType.DMA((2, 2)),
        ],
        compiler_params=_params(("arbitrary", "arbitrary"), 56),
        name="moba_attn",
    )(page_table, sel, q, k, v, gate, q_s, k_s, v_s, g_s, cache_k_pages, cache_v_pages)


def _moba_select_kernel(q_ref, km_ref, sel_ref, *, t, topk):
    n_blocks = km_ref.shape[0]
    rows = sel_ref.shape[0]
    rid = lax.broadcasted_iota(I32, (rows, 128), 0)
    lid = lax.broadcasted_iota(I32, (rows, 128), 1)
    nid = lax.broadcasted_iota(I32, (n_blocks, 1), 0)
    out = jnp.zeros((rows, 128), I32)
    for h in range(MOBA_HEADS):
        cs = slice(h * MOBA_DH, (h + 1) * MOBA_DH)
        km = km_ref[:, h, :]
        for i in range(t):
            gs = jnp.sum(km * q_ref[i:i + 1, cs], axis=-1, keepdims=True)
            for r in range(topk):
                mx = jnp.max(gs, axis=0, keepdims=True)
                idx = jnp.min(jnp.where(gs == mx, nid, n_blocks), axis=0, keepdims=True)
                out = jnp.where((rid == h * t + i) & (lid == r), idx, out)
                gs = jnp.where(nid == idx, BELOW_NEG, gs)
    sel_ref[...] = out


def _moba_select(q, kmean):
    b, t, w = q.shape
    n_blocks = kmean.shape[1]
    rows = MOBA_HEADS * t
    return pl.pallas_call(
        functools.partial(_moba_select_kernel, t=t, topk=MOBA_TOPK),
        grid=(b,),
        in_specs=[
            pl.BlockSpec((None, t, w), lambda bi: (bi, 0, 0)),
            pl.BlockSpec((None, n_blocks, MOBA_HEADS, MOBA_DH), lambda bi: (bi, 0, 0, 0)),
        ],
        out_specs=pl.BlockSpec((None, rows, 128), lambda bi: (bi, 0, 0)),
        out_shape=jax.ShapeDtypeStruct((b, rows, 128), I32),
        compiler_params=_params(("arbitrary",), 32),
        name="moba_select",
    )(q, kmean)


def kernel(x_prompt, x_sample, c_prompt, c_sample, state_ret, cache_k, cache_v, page_table, norm_g, w_ada, b_ada,
           w_ret_in, ret_gn_g, w_ret_out, w_moba_in, moba_q_g, moba_k_g, w_moba_out):
    bp, tp, d = x_prompt.shape
    bs, ts, _ = x_sample.shape
    n_pages = page_table.shape[1]
    past_len = n_pages * PAGE_SIZE
    assert past_len % MOBA_BLOCK == 0 and ts <= MOBA_BLOCK and past_len // MOBA_BLOCK >= MOBA_TOPK
    assert w_ada.shape[0] == 2 and w_ret_in.shape[0] == 1 and w_moba_in.shape[0] == 1

    n_c = bp + bs
    c_all = jnp.concatenate([c_prompt, c_sample, jnp.zeros((-n_c % 8, d), F32)], axis=0)
    mod = _ada(c_all, w_ada, b_ada)

    def group_mod(layer):
        mp = mod[layer, :bp].reshape(bp, 1, 3 * d)
        ms = jnp.repeat(mod[layer, bp:n_c], ts, axis=0).reshape(1, bs * ts, 3 * d)
        return [(m[..., :d], m[..., d:2 * d], m[..., 2 * d:]) for m in (mp, ms)]

    xs = x_sample.reshape(1, bs * ts, d)
    ck = cache_k[0]
    cv = cache_v[0]

    tm_proj = 256
    n_steps = bp * (tp // tm_proj)
    n_blocks_past = bs * n_pages // PAGES_PER_BLOCK
    assert n_pages % PAGES_PER_BLOCK == 0 and n_blocks_past % n_steps == 0 and n_blocks_past // n_steps >= 2
    bps = n_blocks_past // n_steps
    bps_ret = bps // 2
    stream_ret = _KmStream(0, bps_ret * PAGES_PER_BLOCK, n_steps, n_pages)
    stream_moba = _KmStream(bps_ret * PAGES_PER_BLOCK * n_steps, (bps - bps_ret) * PAGES_PER_BLOCK, n_steps, n_pages)
    w_ret_in_b = w_ret_in[0].astype(BF16)

    (sh_p, sc_p, gt_p), (sh_s, sc_s, gt_s) = group_mod(0)
    proj_s = _ret_proj(xs, norm_g[0], sh_s, sc_s, w_ret_in_b, F32, tm=bs * ts).reshape(bs, ts, -1)
    y_p, ret_p, og_s, ret_s, km_a = _ret_layer(x_prompt, norm_g[0], sh_p, sc_p, gt_p, w_ret_in_b, ret_gn_g[0],
                                                w_ret_out[0].astype(BF16), proj_s, state_ret[0], page_table, ck,
                                                stream_ret, tm=tm_proj)
    y_s = _out_proj(og_s.reshape(1, bs * ts, RET_V).astype(BF16), w_ret_out[0], xs, gt_s, tm=bs * ts)

    (sh_p, sc_p, gt_p), (sh_s, sc_s, gt_s) = group_mod(1)
    q_p, k_p, v_p, g_p, km_b = _moba_proj(y_p, norm_g[1], sh_p, sc_p, w_moba_in[0], moba_q_g[0], moba_k_g[0], BF16,
                                          tm=tm_proj, km=(page_table, ck, stream_moba))
    q_s, k_s, v_s, g_s = _moba_proj(y_s, norm_g[1], sh_s, sc_s, w_moba_in[0], moba_q_g[0], moba_k_g[0], F32,
                                    tm=bs * ts)
    q_s, k_s, v_s, g_s = (a.reshape(bs, ts, MOBA_W) for a in (q_s, k_s, v_s, g_s))
    kmean = jnp.concatenate([km_a, km_b], axis=0).reshape(bs, past_len // MOBA_BLOCK, MOBA_HEADS, MOBA_DH)
    sel = _moba_select(q_s, kmean)[:, :, :MOBA_TOPK].reshape(bs, MOBA_HEADS * ts * MOBA_TOPK)
    oa_p, oa_s = _moba_attn(q_p, k_p, v_p, g_p, q_s, k_s, v_s, g_s, ck, cv, page_table, sel, past_len)
    y_p = _out_proj(oa_p, w_moba_out[0], y_p, gt_p, tm=512)
    y_s = _out_proj(oa_s.reshape(1, bs * ts, MOBA_W).astype(BF16), w_moba_out[0], y_s, gt_s, tm=bs * ts)

    hd = (MOBA_HEADS, MOBA_DH)
    return (y_p, y_s.reshape(bs, ts, d), ret_p[None], ret_s[None],
            k_p.reshape(1, bp, tp, *hd), v_p.reshape(1, bp, tp, *hd),
            k_s.reshape(1, bs, ts, *hd), v_s.reshape(1, bs, ts, *hd))
```

```python
import functools
import math
from typing import NamedTuple

import jax
import jax.numpy as jnp
from jax import lax
from jax.experimental import pallas as pl
from jax.experimental.pallas import tpu as pltpu

F32 = jnp.float32
BF16 = jnp.bfloat16
I32 = jnp.int32

RET_HEADS = 4
RET_DK = 256
RET_DV = 512
RET_CHUNK = 256
RET_QK = RET_HEADS * RET_DK
RET_V = RET_HEADS * RET_DV
MOBA_HEADS = 8
MOBA_DH = 128
MOBA_W = MOBA_HEADS * MOBA_DH
MOBA_BLOCK = 256
MOBA_TOPK = 3
PAGE_SIZE = 128
EPS = 1e-6
NEG = -1e30
BELOW_NEG = -3e38

V7X_VMEM_BYTES = 64 * 1024 * 1024
NT_DIMS = (((1,), (1,)), ((), ()))


def _params(semantics, vmem_mb):
    assert vmem_mb * 1024 * 1024 < V7X_VMEM_BYTES
    return pltpu.CompilerParams(dimension_semantics=semantics, vmem_limit_bytes=vmem_mb * 1024 * 1024)


def _silu(x):
    return x * jax.nn.sigmoid(x)


def _norm_mod(x, g, sh, sc):
    ms = jnp.mean(x * x, axis=-1, keepdims=True)
    y = x * lax.rsqrt(ms + EPS) * g
    return y * (1.0 + sc) + sh


def _head_rms(x, g):
    ms = jnp.mean(x * x, axis=-1, keepdims=True)
    return x * lax.rsqrt(ms + EPS) * g


def _ret_log_gamma(h):
    return math.log1p(-(2.0 ** (-5.0 - h)))


def _ada_kernel(c_ref, w_ref, b_ref, o_ref):
    s = _silu(c_ref[...]).astype(BF16)
    o_ref[...] = jnp.dot(s, w_ref[...].astype(BF16), preferred_element_type=F32) + b_ref[...]


def _ada(c_all, w_ada, b_ada):
    depth, d, n = w_ada.shape
    m = c_all.shape[0]
    tn = 512
    return pl.pallas_call(
        _ada_kernel,
        grid=(depth, n // tn),
        in_specs=[
            pl.BlockSpec((m, d), lambda l, j: (0, 0)),
            pl.BlockSpec((None, d, tn), lambda l, j: (l, 0, j)),
            pl.BlockSpec((None, 1, tn), lambda l, j: (l, 0, j)),
        ],
        out_specs=pl.BlockSpec((None, m, tn), lambda l, j: (l, 0, j)),
        out_shape=jax.ShapeDtypeStruct((depth, m, n), F32),
        compiler_params=_params(("arbitrary", "arbitrary"), 32),
        name="ada",
    )(c_all, w_ada, b_ada.reshape(depth, 1, n))


def _mod_spec(mod, tm):
    d = mod.shape[-1]
    if mod.shape[1] == 1:
        return pl.BlockSpec((None, 1, d), lambda g, i, *_: (g, 0, 0))
    return pl.BlockSpec((None, tm, d), lambda g, i, *_: (g, i, 0))


def _weight_spec(shape, index_map, resident):
    if resident:
        return pl.BlockSpec(shape, index_map, pipeline_mode=pl.Buffered(1))
    return pl.BlockSpec(shape, index_map)


def _cast_weight(w_ref, wb_scr, col_chunk):
    for c in range(w_ref.shape[1] // col_chunk):
        cs = slice(c * col_chunk, (c + 1) * col_chunk)
        wb_scr[:, cs] = w_ref[:, cs].astype(BF16)


PAGES_PER_BLOCK = MOBA_BLOCK // PAGE_SIZE


class _KmStream(NamedTuple):
    page0: int
    pages_per_step: int
    n_steps: int
    n_pages: int


def _km_scratch(stream, cache):
    return [pltpu.VMEM((2, stream.pages_per_step, *cache.shape[1:]), F32), pltpu.SemaphoreType.DMA((2,))]


def _km_out(stream, cache, steps_per_group):
    hd = cache.shape[2:]
    blocks_per_step = stream.pages_per_step // PAGES_PER_BLOCK
    spec = pl.BlockSpec((blocks_per_step, *hd), lambda gi, i: (gi * steps_per_group + i, 0, 0))
    return spec, jax.ShapeDtypeStruct((stream.n_steps * blocks_per_step, *hd), F32)


def _km_copy(ck_ref, buf, sems, page, slot, p):
    return pltpu.make_async_copy(ck_ref.at[page], buf.at[slot, p], sems.at[slot])


def _km_start(stream, pt_ref, ck_ref, buf, sems, step, slot):
    for p in range(stream.pages_per_step):
        flat = stream.page0 + step * stream.pages_per_step + p
        page = pt_ref[flat // stream.n_pages, flat % stream.n_pages]
        _km_copy(ck_ref, buf, sems, page, slot, p).start()


def _km_wait(stream, ck_ref, buf, sems, slot):
    pltpu.make_async_copy(ck_ref.at[pl.ds(0, stream.pages_per_step)], buf.at[slot], sems.at[slot]).wait()


def _run_with_km_stream(stream, km_refs, chunks):
    if stream is not None:
        pt_ref, ck_ref, km_ref, buf, sems = km_refs
        step = pl.program_id(0) * pl.num_programs(1) + pl.program_id(1)
        slot = step % 2

        @pl.when(step == 0)
        def _():
            _km_start(stream, pt_ref, ck_ref, buf, sems, step, 0)

        _km_wait(stream, ck_ref, buf, sems, slot)
        _km_start(stream, pt_ref, ck_ref, buf, sems, jnp.where(step + 1 == stream.n_steps, 0, step + 1), 1 - slot)
        for blk in range(stream.pages_per_step // PAGES_PER_BLOCK):
            tot = jnp.sum(buf[slot, blk * PAGES_PER_BLOCK], axis=0)
            for p in range(1, PAGES_PER_BLOCK):
                tot = tot + jnp.sum(buf[slot, blk * PAGES_PER_BLOCK + p], axis=0)
            km_ref[blk] = tot / float(MOBA_BLOCK)

    for chunk in chunks:
        chunk()

    if stream is not None:
        @pl.when(step == stream.n_steps - 1)
        def _():
            _km_wait(stream, ck_ref, buf, sems, 1 - slot)


def _ret_proj_kernel(*refs, col_chunk, stream):
    x_ref, g_ref, sh_ref, sc_ref, w_ref = refs[:5]
    if stream is None:
        (o_ref,) = refs[5:]
        km_refs = None
    else:
        pt_ref, ck_ref, o_ref, km_ref, buf, sems = refs[5:]
        km_refs = (pt_ref, ck_ref, km_ref, buf, sems)

    h = _norm_mod(x_ref[...], g_ref[...], sh_ref[...], sc_ref[...]).astype(BF16)

    def column_chunk(c):
        cs = slice(c * col_chunk, (c + 1) * col_chunk)
        o_ref[:, cs] = jnp.dot(h, w_ref[:, cs], preferred_element_type=F32).astype(o_ref.dtype)

    _run_with_km_stream(stream, km_refs,
                        [functools.partial(column_chunk, c) for c in range(w_ref.shape[1] // col_chunk)])


def _ret_proj(x, norm_g, sh, sc, w, out_dtype, tm, km=None):
    g, r, d = x.shape
    n = w.shape[1]
    assert w.dtype == BF16
    in_specs = [
        pl.BlockSpec((None, tm, d), lambda gi, i: (gi, i, 0)),
        pl.BlockSpec((1, d), lambda gi, i: (0, 0)),
        _mod_spec(sh, tm),
        _mod_spec(sc, tm),
        _weight_spec((d, n), lambda gi, i: (0, 0), True),
    ]
    out_specs = [pl.BlockSpec((None, tm, n), lambda gi, i: (gi, i, 0))]
    out_shape = [jax.ShapeDtypeStruct((g, r, n), out_dtype)]
    scratch = []
    args = [x, norm_g.reshape(1, d), sh, sc, w]
    stream = None
    if km is not None:
        page_table, cache, stream = km
        assert stream.n_steps == g * (r // tm)
        in_specs += [pl.BlockSpec(memory_space=pltpu.SMEM), pl.BlockSpec(memory_space=pl.ANY)]
        km_spec, km_shape = _km_out(stream, cache, r // tm)
        out_specs.append(km_spec)
        out_shape.append(km_shape)
        scratch += _km_scratch(stream, cache)
        args += [page_table, cache]
    outs = pl.pallas_call(
        functools.partial(_ret_proj_kernel, col_chunk=512, stream=stream),
        grid=(g, r // tm),
        in_specs=in_specs,
        out_specs=out_specs,
        out_shape=out_shape,
        scratch_shapes=scratch,
        compiler_params=_params(("arbitrary", "arbitrary"), 58),
        name="ret_proj",
    )(*args)
    return outs if km is not None else outs[0]


def _out_proj_kernel(a_ref, w_ref, x_ref, gt_ref, o_ref, wb_scr):
    @pl.when((pl.program_id(0) == 0) & (pl.program_id(1) == 0))
    def _():
        _cast_weight(w_ref, wb_scr, 512)

    o_ref[...] = x_ref[...] + gt_ref[...] * jnp.dot(a_ref[...], wb_scr[...], preferred_element_type=F32)


def _out_proj(a, w, x, gate, tm):
    g, r, k = a.shape
    d = w.shape[1]
    return pl.pallas_call(
        _out_proj_kernel,
        grid=(g, r // tm),
        in_specs=[
            pl.BlockSpec((None, tm, k), lambda gi, i: (gi, i, 0)),
            _weight_spec((k, d), lambda gi, i: (0, 0), True),
            pl.BlockSpec((None, tm, d), lambda gi, i: (gi, i, 0)),
            _mod_spec(gate, tm),
        ],
        out_specs=pl.BlockSpec((None, tm, d), lambda gi, i: (gi, i, 0)),
        out_shape=jax.ShapeDtypeStruct((g, r, d), F32),
        scratch_shapes=[pltpu.VMEM((k, d), BF16)],
        compiler_params=_params(("arbitrary", "arbitrary"), 48),
        name="out_proj",
    )(a, w, x, gate)


def _pick(index, values):
    out = values[0]
    for k in range(1, len(values)):
        out = jnp.where(index == k, values[k], out)
    return out


def _ret_sample_heads(q_ref, k_ref, v_ref, g_ref, s0_ref, gn_ref, og_ref, s_ref, *, t, hps, part):
    ii = lax.broadcasted_iota(I32, (t, t), 0)
    jj = lax.broadcasted_iota(I32, (t, t), 1)
    diff = (ii - jj).astype(F32)
    pos = lax.broadcasted_iota(I32, (t, 1), 0).astype(F32)
    pad = -t % 8
    for j in range(hps):
        heads = [p * hps + j for p in range(RET_HEADS // hps)]
        lg = _pick(part, [_ret_log_gamma(h) for h in heads])
        chunk_decay = _pick(part, [math.exp(_ret_log_gamma(h) * t) for h in heads])
        decay = jnp.where(diff >= 0, jnp.exp(lg * jnp.maximum(diff, 0.0)), 0.0)
        q_decay = jnp.exp(lg * (pos + 1.0))
        k_decay = jnp.exp(lg * (t - 1.0 - pos))
        q = q_ref[:, j * RET_DK:(j + 1) * RET_DK]
        k = k_ref[:, j * RET_DK:(j + 1) * RET_DK] * (RET_DK ** -0.5)
        v = v_ref[:, j * RET_DV:(j + 1) * RET_DV]
        gate = g_ref[:, j * RET_DV:(j + 1) * RET_DV]
        state = s0_ref[j]
        o = jnp.dot(q * q_decay, state, preferred_element_type=F32)
        for i in range(t):
            s_i = jnp.sum(q * k[i:i + 1, :], axis=-1, keepdims=True) * decay[:, i:i + 1]
            o = o + s_i * v[i:i + 1, :]
        kd = jnp.concatenate([k * k_decay, jnp.zeros((pad, RET_DK), F32)], axis=0)
        vp = jnp.concatenate([v, jnp.zeros((pad, RET_DV), F32)], axis=0)
        kv = lax.dot_general(kd, vp, (((0,), (0,)), ((), ())), preferred_element_type=F32)
        s_ref[j] = chunk_decay * state + kv
        oc = o - jnp.mean(o, axis=-1, keepdims=True)
        y = oc * lax.rsqrt(jnp.mean(oc * oc, axis=-1, keepdims=True) + EPS) * gn_ref[pl.ds(part * hps + j, 1), :]
        og_ref[:, j * RET_DV:(j + 1) * RET_DV] = (y * _silu(gate)).astype(og_ref.dtype)


def _ret_mixer_kernel(p_ref, x_ref, gt_ref, gn_ref, w_ref, qs_ref, ks_ref, vs_ref, gs_ref, s0_ref,
                      y_ref, s_ref, ogs_ref, ss_ref, wb_scr, og_ref, *, chunk, n_chunks, t_s, hps):
    step = pl.program_id(0) * pl.num_programs(1) + pl.program_id(1)

    @pl.when(step == 0)
    def _():
        _cast_weight(w_ref, wb_scr, 512)

    @pl.when(pl.program_id(1) == 0)
    def _():
        s_ref[...] = jnp.zeros_like(s_ref)

    _ret_sample_heads(qs_ref, ks_ref, vs_ref, gs_ref, s0_ref, gn_ref, ogs_ref, ss_ref, t=t_s, hps=hps,
                      part=step % (RET_HEADS // hps))

    ii = lax.broadcasted_iota(I32, (chunk, chunk), 0)
    jj = lax.broadcasted_iota(I32, (chunk, chunk), 1)
    diff = (ii - jj).astype(F32)
    pos = lax.broadcasted_iota(I32, (chunk, 1), 0).astype(F32)
    for h in range(RET_HEADS):
        lg = _ret_log_gamma(h)
        decay = jnp.where(diff >= 0, jnp.exp(lg * jnp.maximum(diff, 0.0)), 0.0)
        q_decay = jnp.exp(lg * (pos + 1.0))
        k_decay = jnp.exp(lg * (chunk - 1.0 - pos))
        chunk_decay = math.exp(lg * chunk)
        for c in range(n_chunks):
            rows = slice(c * chunk, (c + 1) * chunk)
            q = p_ref[rows, h * RET_DK:(h + 1) * RET_DK]
            k = p_ref[rows, RET_QK + h * RET_DK:RET_QK + (h + 1) * RET_DK].astype(F32) * (RET_DK ** -0.5)
            v = p_ref[rows, 2 * RET_QK + h * RET_DV:2 * RET_QK + (h + 1) * RET_DV]
            gate = p_ref[rows, 2 * RET_QK + RET_V + h * RET_DV:2 * RET_QK + RET_V + (h + 1) * RET_DV].astype(F32)
            state = s_ref[h]
            scores = lax.dot_general(q, k.astype(BF16), NT_DIMS, preferred_element_type=F32) * decay
            inner = jnp.dot(scores.astype(BF16), v, preferred_element_type=F32)
            cross = jnp.dot((q.astype(F32) * q_decay).astype(BF16), state.astype(BF16), preferred_element_type=F32)
            kv = lax.dot_general((k * k_decay).astype(BF16), v, (((0,), (0,)), ((), ())), preferred_element_type=F32)
            s_ref[h] = chunk_decay * state + kv
            o = inner + cross
            oc = o - jnp.mean(o, axis=-1, keepdims=True)
            y = oc * lax.rsqrt(jnp.mean(oc * oc, axis=-1, keepdims=True) + EPS) * gn_ref[h:h + 1, :]
            og_ref[rows, h * RET_DV:(h + 1) * RET_DV] = (y * _silu(gate)).astype(og_ref.dtype)

    y_ref[...] = x_ref[...] + gt_ref[...] * jnp.dot(og_ref[...], wb_scr[...], preferred_element_type=F32)


def _ret_mixer(proj, x, gate, gn_g, w_out, proj_s, state0, rows_per_step):
    b, t, n = proj.shape
    d = x.shape[-1]
    bs, ts, _ = proj_s.shape
    assert t % RET_CHUNK == 0 and rows_per_step % RET_CHUNK == 0
    n_t = t // rows_per_step
    n_steps = b * n_t
    assert (bs * RET_HEADS) % n_steps == 0
    hps = bs * RET_HEADS // n_steps
    assert RET_HEADS % hps == 0
    per_sb = RET_HEADS // hps

    def sample_spec(width, first_col):
        first = first_col // (hps * width)
        return pl.BlockSpec((None, ts, hps * width),
                            lambda bi, i: ((bi * n_t + i) // per_sb, 0, first + (bi * n_t + i) % per_sb))

    state_s_spec = pl.BlockSpec((None, hps, RET_DK, RET_DV),
                                lambda bi, i: ((bi * n_t + i) // per_sb, (bi * n_t + i) % per_sb, 0, 0))
    return pl.pallas_call(
        functools.partial(_ret_mixer_kernel, chunk=RET_CHUNK, n_chunks=rows_per_step // RET_CHUNK, t_s=ts, hps=hps),
        grid=(b, n_t),
        in_specs=[
            pl.BlockSpec((None, rows_per_step, n), lambda bi, i: (bi, i, 0)),
            pl.BlockSpec((None, rows_per_step, d), lambda bi, i: (bi, i, 0)),
            _mod_spec(gate, rows_per_step),
            pl.BlockSpec((RET_HEADS, RET_DV), lambda bi, i: (0, 0)),
            _weight_spec((RET_V, d), lambda bi, i: (0, 0), True),
            sample_spec(RET_DK, 0),
            sample_spec(RET_DK, RET_QK),
            sample_spec(RET_DV, 2 * RET_QK),
            sample_spec(RET_DV, 2 * RET_QK + RET_V),
            state_s_spec,
        ],
        out_specs=[
            pl.BlockSpec((None, rows_per_step, d), lambda bi, i: (bi, i, 0)),
            pl.BlockSpec((None, RET_HEADS, RET_DK, RET_DV), lambda bi, i: (bi, 0, 0, 0)),
            sample_spec(RET_DV, 0),
            state_s_spec,
        ],
        out_shape=[
            jax.ShapeDtypeStruct((b, t, d), F32),
            jax.ShapeDtypeStruct((b, RET_HEADS, RET_DK, RET_DV), F32),
            jax.ShapeDtypeStruct((bs, ts, RET_V), F32),
            jax.ShapeDtypeStruct((bs, RET_HEADS, RET_DK, RET_DV), F32),
        ],
        scratch_shapes=[pltpu.VMEM((RET_V, d), BF16), pltpu.VMEM((rows_per_step, RET_V), BF16)],
        compiler_params=_params(("arbitrary", "arbitrary"), 48),
        name="ret_mixer",
    )(proj, x, gate, gn_g, w_out, proj_s, proj_s, proj_s, proj_s, state0)


def _moba_proj_kernel(*refs, stream):
    x_ref, g_ref, sh_ref, sc_ref, w_ref, qg_ref, kg_ref = refs[:7]
    if stream is None:
        q_ref, k_ref, v_ref, gt_ref, wb_scr = refs[7:]
        km_refs = None
    else:
        pt_ref, ck_ref, q_ref, k_ref, v_ref, gt_ref, km_ref, wb_scr, buf, sems = refs[7:]
        km_refs = (pt_ref, ck_ref, km_ref, buf, sems)

    @pl.when((pl.program_id(0) == 0) & (pl.program_id(1) == 0))
    def _():
        _cast_weight(w_ref, wb_scr, 512)

    h = _norm_mod(x_ref[...], g_ref[...], sh_ref[...], sc_ref[...]).astype(BF16)
    w = MOBA_W

    def q_part():
        q = jnp.dot(h, wb_scr[:, 0:w], preferred_element_type=F32)
        for hd in range(MOBA_HEADS):
            cs = slice(hd * MOBA_DH, (hd + 1) * MOBA_DH)
            q_ref[:, cs] = (_head_rms(q[:, cs], qg_ref[...]) * (MOBA_DH ** -0.5)).astype(q_ref.dtype)

    def k_part():
        k = jnp.dot(h, wb_scr[:, w:2 * w], preferred_element_type=F32)
        for hd in range(MOBA_HEADS):
            cs = slice(hd * MOBA_DH, (hd + 1) * MOBA_DH)
            k_ref[:, cs] = _head_rms(k[:, cs], kg_ref[...])

    def v_part():
        v_ref[...] = jnp.dot(h, wb_scr[:, 2 * w:3 * w], preferred_element_type=F32)

    def gate_part():
        gt_ref[...] = jnp.dot(h, wb_scr[:, 3 * w:4 * w], preferred_element_type=F32).astype(gt_ref.dtype)

    _run_with_km_stream(stream, km_refs, [q_part, k_part, v_part, gate_part])


def _moba_proj(x, norm_g, sh, sc, w, q_g, k_g, act_dtype, tm, km=None):
    g, r, d = x.shape
    n = w.shape[1]
    row_spec = pl.BlockSpec((None, tm, MOBA_W), lambda gi, i: (gi, i, 0))
    in_specs = [
        pl.BlockSpec((None, tm, d), lambda gi, i: (gi, i, 0)),
        pl.BlockSpec((1, d), lambda gi, i: (0, 0)),
        _mod_spec(sh, tm),
        _mod_spec(sc, tm),
        _weight_spec((d, n), lambda gi, i: (0, 0), True),
        pl.BlockSpec((1, MOBA_DH), lambda gi, i: (0, 0)),
        pl.BlockSpec((1, MOBA_DH), lambda gi, i: (0, 0)),
    ]
    out_specs = [row_spec, row_spec, row_spec, row_spec]
    out_shape = [
        jax.ShapeDtypeStruct((g, r, MOBA_W), act_dtype),
        jax.ShapeDtypeStruct((g, r, MOBA_W), F32),
        jax.ShapeDtypeStruct((g, r, MOBA_W), F32),
        jax.ShapeDtypeStruct((g, r, MOBA_W), act_dtype),
    ]
    scratch = [pltpu.VMEM((d, n), BF16)]
    args = [x, norm_g.reshape(1, d), sh, sc, w, q_g.reshape(1, MOBA_DH), k_g.reshape(1, MOBA_DH)]
    stream = None
    if km is not None:
        page_table, cache, stream = km
        assert stream.n_steps == g * (r // tm)
        in_specs += [pl.BlockSpec(memory_space=pltpu.SMEM), pl.BlockSpec(memory_space=pl.ANY)]
        km_spec, km_shape = _km_out(stream, cache, r // tm)
        out_specs.append(km_spec)
        out_shape.append(km_shape)
        scratch += _km_scratch(stream, cache)
        args += [page_table, cache]
    return pl.pallas_call(
        functools.partial(_moba_proj_kernel, stream=stream),
        grid=(g, r // tm),
        in_specs=in_specs,
        out_specs=out_specs,
        out_shape=out_shape,
        scratch_shapes=scratch,
        compiler_params=_params(("arbitrary", "arbitrary"), 56),
        name="moba_proj",
    )(*args)


def _alibi_slope(head):
    hv = jnp.full((1, 1), head, I32).astype(F32)
    return jnp.exp2(-8.0 * (hv + 1.0) / MOBA_HEADS)


AUG_PEN0 = 8


def _moba_prompt_head(q_ref, k_ref, v_ref, g_ref, o_ref, ka_scr, vt_scr, qa_scr, km_scr, *, n_blocks, head, side_work):
    blk = MOBA_BLOCK
    dh = MOBA_DH
    t_len = n_blocks * blk
    slope = _alibi_slope(head)
    pen_rows = km_scr.shape[0] // 4

    lane = lax.broadcasted_iota(I32, (blk, dh), 1)
    k_off = lax.broadcasted_iota(I32, (blk, dh), 0).astype(F32)
    km_scr[...] = jnp.zeros_like(km_scr)
    for n in range(n_blocks):
        rows = slice(n * blk, (n + 1) * blk)
        kn = k_ref[rows, :]
        ka_scr[rows, 0:dh] = kn.astype(BF16)
        aug = jnp.where(lane < 2, 1.0,
                        jnp.where(lane == 2, slope * float(blk * n),
                                  jnp.where(lane == 3, slope * k_off,
                                            jnp.where(lane == AUG_PEN0 + n, 1.0, 0.0))))
        ka_scr[rows, dh:2 * dh] = aug.astype(BF16)
        vt_scr[:, rows] = v_ref[rows, :].T.astype(BF16)
        mean = jnp.mean(kn, axis=0, keepdims=True)
        hi = mean.astype(BF16).astype(F32)
        mid = (mean - hi).astype(BF16).astype(F32)
        lo = (mean - hi - mid).astype(BF16).astype(F32)
        km_scr[n:n + 1, :] = hi
        km_scr[pen_rows + n:pen_rows + n + 1, :] = mid
        km_scr[2 * pen_rows + n:2 * pen_rows + n + 1, :] = lo
        qa_scr[0:dh, rows] = q_ref[rows, :].astype(F32).T.astype(BF16)

    parts = jnp.dot(km_scr[...].astype(BF16), qa_scr[0:dh, :], preferred_element_type=F32)
    gs = parts[0:pen_rows] + parts[pen_rows:2 * pen_rows] + parts[2 * pen_rows:3 * pen_rows]
    nid = lax.broadcasted_iota(I32, (pen_rows, t_len), 0)
    q_pos = lax.broadcasted_iota(I32, (pen_rows, t_len), 1)
    own = lax.shift_right_logical(q_pos, int(math.log2(blk)))
    past = nid < own
    attended = nid == own
    gs = jnp.where(past, gs, NEG)
    for _ in range(min(MOBA_TOPK, n_blocks)):
        mx = jnp.max(gs, axis=0, keepdims=True)
        idx = jnp.min(jnp.where(gs == mx, nid, pen_rows), axis=0, keepdims=True)
        pick = nid == idx
        attended = attended | (pick & past)
        gs = jnp.where(pick, BELOW_NEG, gs)
    pen = jnp.where(attended, 0.0, NEG)
    r8 = lax.broadcasted_iota(I32, (8, t_len), 0)
    q8 = lax.broadcasted_iota(I32, (8, t_len), 1)
    own8 = lax.shift_right_logical(q8, int(math.log2(blk)))
    bias = jnp.where(r8 == 0, -slope * (own8 * blk).astype(F32),
                     jnp.where(r8 == 1, -slope * (q8 - own8 * blk).astype(F32), jnp.where(r8 < 4, 1.0, 0.0)))
    extra = jnp.concatenate([bias, pen, jnp.zeros((dh - 8 - pen_rows, t_len), F32)], axis=0)
    qa_scr[dh:2 * dh, :] = extra.astype(BF16)

    def query_block(own):
        n_keys = (own + 1) * blk
        cols = slice(own * blk, n_keys)
        s = jnp.dot(ka_scr[0:n_keys, :], qa_scr[:, cols], preferred_element_type=F32)
        k_idx = lax.broadcasted_iota(I32, (blk, blk), 0)
        q_idx = lax.broadcasted_iota(I32, (blk, blk), 1)
        s_own = jnp.where(q_idx >= k_idx, s[own * blk:n_keys], NEG)
        m = jnp.max(s_own, axis=0, keepdims=True)
        if own > 0:
            s_past = s[0:own * blk]
            m = jnp.maximum(m, jnp.max(s_past, axis=0, keepdims=True))
        p = jnp.exp(s_own - m)
        l = jnp.sum(p, axis=0, keepdims=True)
        p = p.astype(BF16)
        if own > 0:
            p_past = jnp.exp(s_past - m)
            l = l + jnp.sum(p_past, axis=0, keepdims=True)
            p = jnp.concatenate([p_past.astype(BF16), p], axis=0)
        acc = jnp.dot(vt_scr[:, 0:n_keys], p, preferred_element_type=F32)
        o = (acc / l).T
        o_ref[cols, :] = (o * _silu(g_ref[cols, :].astype(F32))).astype(o_ref.dtype)

    for own in range(n_blocks):
        for work in side_work.get(own, ()):
            work()
        query_block(own)


class _SampleAttn(NamedTuple):
    units: int
    t: int
    past_len: int
    n_sel: int


def _sample_slab_copies(ck_ref, cv_ref, kbuf, vbuf, sems, page, head, slot, j):
    return (pltpu.make_async_copy(ck_ref.at[page, :, head, :], kbuf.at[slot, j], sems.at[0, slot]),
            pltpu.make_async_copy(cv_ref.at[page, :, head, :], vbuf.at[slot, j], sems.at[1, slot]))


def _sample_start(cfg, pt_ref, sel_ref, ck_ref, cv_ref, kbuf, vbuf, sems, step, slot):
    for u in range(cfg.units):
        unit = step * cfg.units + u
        sb = unit // MOBA_HEADS
        sh = unit % MOBA_HEADS
        for i in range(cfg.t):
            for s in range(MOBA_TOPK):
                blk_id = sel_ref[sb, (sh * cfg.t + i) * MOBA_TOPK + s]
                for p in range(PAGES_PER_BLOCK):
                    page = pt_ref[sb, blk_id * PAGES_PER_BLOCK + p]
                    j = u * cfg.n_sel + (i * MOBA_TOPK + s) * PAGES_PER_BLOCK + p
                    for c in _sample_slab_copies(ck_ref, cv_ref, kbuf, vbuf, sems, page, sh, slot, j):
                        c.start()


def _sample_wait(cfg, ck_ref, cv_ref, kbuf, vbuf, sems, slot):
    n = cfg.units * cfg.n_sel
    pltpu.make_async_copy(ck_ref.at[pl.ds(0, n), :, 0, :], kbuf.at[slot], sems.at[0, slot]).wait()
    pltpu.make_async_copy(cv_ref.at[pl.ds(0, n), :, 0, :], vbuf.at[slot], sems.at[1, slot]).wait()


def _sample_unit_stages(cfg, sel_ref, q_ref, kn_ref, vn_ref, g_ref, o_ref, kbuf, vbuf, slot, u, sb, sh):
    t = cfg.t
    cols = slice(u * MOBA_DH, (u + 1) * MOBA_DH)
    keys_per_query = cfg.n_sel // t * PAGE_SIZE
    n_keys = cfg.n_sel * PAGE_SIZE
    vals = {}

    def scores():
        slope = _alibi_slope(sh)
        q = q_ref[:, cols]
        kn = kn_ref[:, cols]
        k_all = kbuf[slot, u * cfg.n_sel:(u + 1) * cfg.n_sel].reshape(n_keys, MOBA_DH)
        off = lax.broadcasted_iota(I32, (1, PAGE_SIZE), 1)
        k_pos = jnp.concatenate(
            [sel_ref[sb, (sh * t + i) * MOBA_TOPK + s] * MOBA_BLOCK + p * PAGE_SIZE + off
             for i in range(t) for s in range(MOBA_TOPK) for p in range(PAGES_PER_BLOCK)], axis=1)
        row = lax.broadcasted_iota(I32, (t, n_keys), 0)
        col = lax.broadcasted_iota(I32, (t, n_keys), 1)
        owned = (col >= row * keys_per_query) & (col < (row + 1) * keys_per_query)
        dist = (cfg.past_len + row) - k_pos
        s_sel = lax.dot_general(q, k_all, NT_DIMS, preferred_element_type=F32) - slope * dist.astype(F32)
        vals["s_sel"] = jnp.where(owned & (dist >= 0), s_sel, NEG)
        s_new = jnp.concatenate([jnp.sum(q * kn[j:j + 1, :], axis=-1, keepdims=True) for j in range(t)], axis=1)
        d_new = lax.broadcasted_iota(I32, (t, t), 0) - lax.broadcasted_iota(I32, (t, t), 1)
        vals["s_new"] = jnp.where(d_new >= 0, s_new - slope * d_new.astype(F32), NEG)

    def softmax():
        s_sel, s_new = vals["s_sel"], vals["s_new"]
        m = jnp.maximum(jnp.max(s_sel, axis=1, keepdims=True), jnp.max(s_new, axis=1, keepdims=True))
        vals["p_sel"] = jnp.exp(s_sel - m)
        vals["p_new"] = jnp.exp(s_new - m)
        vals["l"] = jnp.sum(vals["p_sel"], axis=1, keepdims=True) + jnp.sum(vals["p_new"], axis=1, keepdims=True)

    def values():
        vn = vn_ref[:, cols]
        v_all = vbuf[slot, u * cfg.n_sel:(u + 1) * cfg.n_sel].reshape(n_keys, MOBA_DH)
        o = jnp.dot(vals["p_sel"], v_all, preferred_element_type=F32)
        for j in range(t):
            o = o + vals["p_new"][:, j:j + 1] * vn[j:j + 1, :]
        o_ref[:, cols] = (o / vals["l"]) * _silu(g_ref[:, cols])

    return scores, softmax, values


def _moba_attn_kernel(pt_ref, sel_ref, q_ref, k_ref, v_ref, g_ref, qs_ref, kn_ref, vn_ref, gs_ref, ck_ref, cv_ref,
                      o_ref, os_ref, ka_scr, vt_scr, qa_scr, km_scr, kbuf, vbuf, sems, *, n_blocks, cfg):
    head = pl.program_id(1)
    n_steps = pl.num_programs(0) * pl.num_programs(1)
    step = pl.program_id(0) * pl.num_programs(1) + head
    slot = step % 2

    @pl.when(step == 0)
    def _():
        _sample_start(cfg, pt_ref, sel_ref, ck_ref, cv_ref, kbuf, vbuf, sems, step, 0)

    _sample_wait(cfg, ck_ref, cv_ref, kbuf, vbuf, sems, slot)
    _sample_start(cfg, pt_ref, sel_ref, ck_ref, cv_ref, kbuf, vbuf, sems, jnp.where(step + 1 == n_steps, 0, step + 1),
                  1 - slot)

    unit0 = step * cfg.units
    stages = [_sample_unit_stages(cfg, sel_ref, qs_ref, kn_ref, vn_ref, gs_ref, os_ref, kbuf, vbuf, slot, u,
                                  unit0 // MOBA_HEADS, unit0 % MOBA_HEADS + u) for u in range(cfg.units)]
    gap = max(1, n_blocks // 4)
    side_work = {k * gap: [unit[k] for unit in stages] for k in range(3)}
    _moba_prompt_head(q_ref, k_ref, v_ref, g_ref, o_ref, ka_scr, vt_scr, qa_scr, km_scr, n_blocks=n_blocks, head=head,
                      side_work=side_work)

    @pl.when(step == n_steps - 1)
    def _():
        _sample_wait(cfg, ck_ref, cv_ref, kbuf, vbuf, sems, 1 - slot)


def _moba_attn(q, k, v, gate, q_s, k_s, v_s, g_s, cache_k_pages, cache_v_pages, page_table, sel, past_len):
    b, t, w = q.shape
    bs, ts, _ = q_s.shape
    assert t % MOBA_BLOCK == 0
    n_blocks = t // MOBA_BLOCK
    pen_rows = -(-n_blocks // 8) * 8
    assert AUG_PEN0 + pen_rows <= MOBA_DH
    assert MOBA_BLOCK & (MOBA_BLOCK - 1) == 0
    n_steps = b * MOBA_HEADS
    assert (bs * MOBA_HEADS) % n_steps == 0
    units = bs * MOBA_HEADS // n_steps
    assert MOBA_HEADS % units == 0 and n_blocks >= 3
    cfg = _SampleAttn(units=units, t=ts, past_len=past_len, n_sel=ts * MOBA_TOPK * PAGES_PER_BLOCK)
    steps_per_sb = MOBA_HEADS // units
    spec = pl.BlockSpec((None, t, MOBA_DH), lambda bi, h: (bi, 0, h))
    sspec = pl.BlockSpec((None, ts, units * MOBA_DH),
                         lambda bi, h: ((bi * MOBA_HEADS + h) // steps_per_sb, 0, (bi * MOBA_HEADS + h) % steps_per_sb))
    smem = pl.BlockSpec(memory_space=pltpu.SMEM)
    hbm = pl.BlockSpec(memory_space=pl.ANY)
    slab = (2, units * cfg.n_sel, PAGE_SIZE, MOBA_DH)
    return pl.pallas_call(
        functools.partial(_moba_attn_kernel, n_blocks=n_blocks, cfg=cfg),
        grid=(b, MOBA_HEADS),
        in_specs=[smem, smem, spec, spec, spec, spec, sspec, sspec, sspec, sspec, hbm, hbm],
        out_specs=[spec, sspec],
        out_shape=[jax.ShapeDtypeStruct((b, t, w), BF16), jax.ShapeDtypeStruct((bs, ts, w), F32)],
        scratch_shapes=[
            pltpu.VMEM((t, 2 * MOBA_DH), BF16),
            pltpu.VMEM((MOBA_DH, t), BF16),
            pltpu.VMEM((2 * MOBA_DH, t), BF16),
            pltpu.VMEM((4 * pen_rows, MOBA_DH), F32),
            pltpu.VMEM(slab, F32),
            pltpu.VMEM(slab, F32),
            pltpu.SemaphoreType.DMA((2, 2)),
        ],
        compiler_params=_params(("arbitrary", "arbitrary"), 56),
        name="moba_attn",
    )(page_table, sel, q, k, v, gate, q_s, k_s, v_s, g_s, cache_k_pages, cache_v_pages)


def _moba_select_kernel(q_ref, km_ref, sel_ref, *, t, topk):
    n_blocks = km_ref.shape[0]
    rows = sel_ref.shape[0]
    rid = lax.broadcasted_iota(I32, (rows, 128), 0)
    lid = lax.broadcasted_iota(I32, (rows, 128), 1)
    nid = lax.broadcasted_iota(I32, (n_blocks, 1), 0)
    out = jnp.zeros((rows, 128), I32)
    for h in range(MOBA_HEADS):
        cs = slice(h * MOBA_DH, (h + 1) * MOBA_DH)
        km = km_ref[:, h, :]
        for i in range(t):
            gs = jnp.sum(km * q_ref[i:i + 1, cs], axis=-1, keepdims=True)
            for r in range(topk):
                mx = jnp.max(gs, axis=0, keepdims=True)
                idx = jnp.min(jnp.where(gs == mx, nid, n_blocks), axis=0, keepdims=True)
                out = jnp.where((rid == h * t + i) & (lid == r), idx, out)
                gs = jnp.where(nid == idx, BELOW_NEG, gs)
    sel_ref[...] = out


def _moba_select(q, kmean):
    b, t, w = q.shape
    n_blocks = kmean.shape[1]
    rows = MOBA_HEADS * t
    return pl.pallas_call(
        functools.partial(_moba_select_kernel, t=t, topk=MOBA_TOPK),
        grid=(b,),
        in_specs=[
            pl.BlockSpec((None, t, w), lambda bi: (bi, 0, 0)),
            pl.BlockSpec((None, n_blocks, MOBA_HEADS, MOBA_DH), lambda bi: (bi, 0, 0, 0)),
        ],
        out_specs=pl.BlockSpec((None, rows, 128), lambda bi: (bi, 0, 0)),
        out_shape=jax.ShapeDtypeStruct((b, rows, 128), I32),
        compiler_params=_params(("arbitrary",), 32),
        name="moba_select",
    )(q, kmean)


def kernel(x_prompt, x_sample, c_prompt, c_sample, state_ret, cache_k, cache_v, page_table, norm_g, w_ada, b_ada,
           w_ret_in, ret_gn_g, w_ret_out, w_moba_in, moba_q_g, moba_k_g, w_moba_out):
    bp, tp, d = x_prompt.shape
    bs, ts, _ = x_sample.shape
    n_pages = page_table.shape[1]
    past_len = n_pages * PAGE_SIZE
    assert past_len % MOBA_BLOCK == 0 and ts <= MOBA_BLOCK and past_len // MOBA_BLOCK >= MOBA_TOPK
    assert w_ada.shape[0] == 2 and w_ret_in.shape[0] == 1 and w_moba_in.shape[0] == 1

    n_c = bp + bs
    c_all = jnp.concatenate([c_prompt, c_sample, jnp.zeros((-n_c % 8, d), F32)], axis=0)
    mod = _ada(c_all, w_ada, b_ada)

    def group_mod(layer):
        mp = mod[layer, :bp].reshape(bp, 1, 3 * d)
        ms = jnp.repeat(mod[layer, bp:n_c], ts, axis=0).reshape(1, bs * ts, 3 * d)
        return [(m[..., :d], m[..., d:2 * d], m[..., 2 * d:]) for m in (mp, ms)]

    xs = x_sample.reshape(1, bs * ts, d)
    ck = cache_k[0]
    cv = cache_v[0]

    tm_proj = 256
    n_steps = bp * (tp // tm_proj)
    n_blocks_past = bs * n_pages // PAGES_PER_BLOCK
    assert n_pages % PAGES_PER_BLOCK == 0 and n_blocks_past % n_steps == 0 and n_blocks_past // n_steps >= 2
    bps = n_blocks_past // n_steps
    bps_ret = min(bps - 1, -(-bps * 5 // 8))
    stream_ret = _KmStream(0, bps_ret * PAGES_PER_BLOCK, n_steps, n_pages)
    stream_moba = _KmStream(bps_ret * PAGES_PER_BLOCK * n_steps, (bps - bps_ret) * PAGES_PER_BLOCK, n_steps, n_pages)
    w_ret_in_b = w_ret_in[0].astype(BF16)

    (sh_p, sc_p, gt_p), (sh_s, sc_s, gt_s) = group_mod(0)
    proj_p, km_a = _ret_proj(x_prompt, norm_g[0], sh_p, sc_p, w_ret_in_b, BF16, tm=tm_proj,
                             km=(page_table, ck, stream_ret))
    proj_s = _ret_proj(xs, norm_g[0], sh_s, sc_s, w_ret_in_b, F32, tm=bs * ts).reshape(bs, ts, -1)
    y_p, ret_p, og_s, ret_s = _ret_mixer(proj_p, x_prompt, gt_p, ret_gn_g[0], w_ret_out[0], proj_s, state_ret[0],
                                         rows_per_step=256)
    y_s = _out_proj(og_s.reshape(1, bs * ts, RET_V).astype(BF16), w_ret_out[0], xs, gt_s, tm=bs * ts)

    (sh_p, sc_p, gt_p), (sh_s, sc_s, gt_s) = group_mod(1)
    q_p, k_p, v_p, g_p, km_b = _moba_proj(y_p, norm_g[1], sh_p, sc_p, w_moba_in[0], moba_q_g[0], moba_k_g[0], BF16,
                                          tm=tm_proj, km=(page_table, ck, stream_moba))
    q_s, k_s, v_s, g_s = _moba_proj(y_s, norm_g[1], sh_s, sc_s, w_moba_in[0], moba_q_g[0], moba_k_g[0], F32,
                                    tm=bs * ts)
    q_s, k_s, v_s, g_s = (a.reshape(bs, ts, MOBA_W) for a in (q_s, k_s, v_s, g_s))
    kmean = jnp.concatenate([km_a, km_b], axis=0).reshape(bs, past_len // MOBA_BLOCK, MOBA_HEADS, MOBA_DH)
    sel = _moba_select(q_s, kmean)[:, :, :MOBA_TOPK].reshape(bs, MOBA_HEADS * ts * MOBA_TOPK)
    oa_p, oa_s = _moba_attn(q_p, k_p, v_p, g_p, q_s, k_s, v_s, g_s, ck, cv, page_table, sel, past_len)
    y_p = _out_proj(oa_p, w_moba_out[0], y_p, gt_p, tm=512)
    y_s = _out_proj(oa_s.reshape(1, bs * ts, MOBA_W).astype(BF16), w_moba_out[0], y_s, gt_s, tm=bs * ts)

    hd = (MOBA_HEADS, MOBA_DH)
    return (y_p, y_s.reshape(bs, ts, d), ret_p[None], ret_s[None],
            k_p.reshape(1, bp, tp, *hd), v_p.reshape(1, bp, tp, *hd),
            k_s.reshape(1, bs, ts, *hd), v_s.reshape(1, bs, ts, *hd))
```

```python
import functools
import math
from typing import NamedTuple

import jax
import jax.numpy as jnp
from jax import lax
from jax.experimental import pallas as pl
from jax.experimental.pallas import tpu as pltpu

F32 = jnp.float32
BF16 = jnp.bfloat16
I32 = jnp.int32

RET_HEADS = 4
RET_DK = 256
RET_DV = 512
RET_CHUNK = 256
RET_QK = RET_HEADS * RET_DK
RET_V = RET_HEADS * RET_DV
MOBA_HEADS = 8
MOBA_DH = 128
MOBA_W = MOBA_HEADS * MOBA_DH
MOBA_BLOCK = 256
MOBA_TOPK = 3
PAGE_SIZE = 128
EPS = 1e-6
NEG = -1e30
BELOW_NEG = -3e38

V7X_VMEM_BYTES = 64 * 1024 * 1024
NT_DIMS = (((1,), (1,)), ((), ()))


def _params(semantics, vmem_mb):
    assert vmem_mb * 1024 * 1024 < V7X_VMEM_BYTES
    return pltpu.CompilerParams(dimension_semantics=semantics, vmem_limit_bytes=vmem_mb * 1024 * 1024)


def _silu(x):
    return x * jax.nn.sigmoid(x)


def _norm_mod(x, g, sh, sc):
    ms = jnp.mean(x * x, axis=-1, keepdims=True)
    y = x * lax.rsqrt(ms + EPS) * g
    return y * (1.0 + sc) + sh


def _head_rms(x, g):
    ms = jnp.mean(x * x, axis=-1, keepdims=True)
    return x * lax.rsqrt(ms + EPS) * g


def _ret_log_gamma(h):
    return math.log1p(-(2.0 ** (-5.0 - h)))


def _ada_kernel(c_ref, w_ref, b_ref, o_ref):
    s = _silu(c_ref[...]).astype(BF16)
    o_ref[...] = jnp.dot(s, w_ref[...].astype(BF16), preferred_element_type=F32) + b_ref[...]


def _ada(c_all, w_ada, b_ada):
    depth, d, n = w_ada.shape
    m = c_all.shape[0]
    tn = 512
    return pl.pallas_call(
        _ada_kernel,
        grid=(depth, n // tn),
        in_specs=[
            pl.BlockSpec((m, d), lambda l, j: (0, 0)),
            pl.BlockSpec((None, d, tn), lambda l, j: (l, 0, j)),
            pl.BlockSpec((None, 1, tn), lambda l, j: (l, 0, j)),
        ],
        out_specs=pl.BlockSpec((None, m, tn), lambda l, j: (l, 0, j)),
        out_shape=jax.ShapeDtypeStruct((depth, m, n), F32),
        compiler_params=_params(("arbitrary", "arbitrary"), 32),
        name="ada",
    )(c_all, w_ada, b_ada.reshape(depth, 1, n))


def _mod_spec(mod, tm):
    d = mod.shape[-1]
    if mod.shape[1] == 1:
        return pl.BlockSpec((None, 1, d), lambda g, i, *_: (g, 0, 0))
    return pl.BlockSpec((None, tm, d), lambda g, i, *_: (g, i, 0))


def _weight_spec(shape, index_map, resident):
    if resident:
        return pl.BlockSpec(shape, index_map, pipeline_mode=pl.Buffered(1))
    return pl.BlockSpec(shape, index_map)


def _cast_weight(w_ref, wb_scr, col_chunk):
    for c in range(w_ref.shape[1] // col_chunk):
        cs = slice(c * col_chunk, (c + 1) * col_chunk)
        wb_scr[:, cs] = w_ref[:, cs].astype(BF16)


PAGES_PER_BLOCK = MOBA_BLOCK // PAGE_SIZE


class _KmStream(NamedTuple):
    page0: int
    pages_per_step: int
    n_steps: int
    n_pages: int


def _km_scratch(stream, cache):
    return [pltpu.VMEM((2, stream.pages_per_step, *cache.shape[1:]), F32), pltpu.SemaphoreType.DMA((2,))]


def _km_out(stream, cache, steps_per_group):
    hd = cache.shape[2:]
    blocks_per_step = stream.pages_per_step // PAGES_PER_BLOCK
    spec = pl.BlockSpec((blocks_per_step, *hd), lambda gi, i: (gi * steps_per_group + i, 0, 0))
    return spec, jax.ShapeDtypeStruct((stream.n_steps * blocks_per_step, *hd), F32)


def _km_copy(ck_ref, buf, sems, page, slot, p):
    return pltpu.make_async_copy(ck_ref.at[page], buf.at[slot, p], sems.at[slot])


def _km_start(stream, pt_ref, ck_ref, buf, sems, step, slot):
    for p in range(stream.pages_per_step):
        flat = stream.page0 + step * stream.pages_per_step + p
        page = pt_ref[flat // stream.n_pages, flat % stream.n_pages]
        _km_copy(ck_ref, buf, sems, page, slot, p).start()


def _km_wait(stream, ck_ref, buf, sems, slot):
    pltpu.make_async_copy(ck_ref.at[pl.ds(0, stream.pages_per_step)], buf.at[slot], sems.at[slot]).wait()


def _run_with_km_stream(stream, km_refs, chunks):
    if stream is not None:
        pt_ref, ck_ref, km_ref, buf, sems = km_refs
        step = pl.program_id(0) * pl.num_programs(1) + pl.program_id(1)
        slot = step % 2

        @pl.when(step == 0)
        def _():
            _km_start(stream, pt_ref, ck_ref, buf, sems, step, 0)

        _km_wait(stream, ck_ref, buf, sems, slot)
        _km_start(stream, pt_ref, ck_ref, buf, sems, jnp.where(step + 1 == stream.n_steps, 0, step + 1), 1 - slot)
        for blk in range(stream.pages_per_step // PAGES_PER_BLOCK):
            tot = jnp.sum(buf[slot, blk * PAGES_PER_BLOCK], axis=0)
            for p in range(1, PAGES_PER_BLOCK):
                tot = tot + jnp.sum(buf[slot, blk * PAGES_PER_BLOCK + p], axis=0)
            km_ref[blk] = tot / float(MOBA_BLOCK)

    for chunk in chunks:
        chunk()

    if stream is not None:
        @pl.when(step == stream.n_steps - 1)
        def _():
            _km_wait(stream, ck_ref, buf, sems, 1 - slot)


def _ret_proj_kernel(*refs, col_chunk, stream):
    x_ref, g_ref, sh_ref, sc_ref, w_ref = refs[:5]
    if stream is None:
        (o_ref,) = refs[5:]
        km_refs = None
    else:
        pt_ref, ck_ref, o_ref, km_ref, buf, sems = refs[5:]
        km_refs = (pt_ref, ck_ref, km_ref, buf, sems)

    h = _norm_mod(x_ref[...], g_ref[...], sh_ref[...], sc_ref[...]).astype(BF16)

    def column_chunk(c):
        cs = slice(c * col_chunk, (c + 1) * col_chunk)
        o_ref[:, cs] = jnp.dot(h, w_ref[:, cs], preferred_element_type=F32).astype(o_ref.dtype)

    _run_with_km_stream(stream, km_refs,
                        [functools.partial(column_chunk, c) for c in range(w_ref.shape[1] // col_chunk)])


def _ret_proj(x, norm_g, sh, sc, w, out_dtype, tm, km=None):
    g, r, d = x.shape
    n = w.shape[1]
    assert w.dtype == BF16
    in_specs = [
        pl.BlockSpec((None, tm, d), lambda gi, i: (gi, i, 0)),
        pl.BlockSpec((1, d), lambda gi, i: (0, 0)),
        _mod_spec(sh, tm),
        _mod_spec(sc, tm),
        _weight_spec((d, n), lambda gi, i: (0, 0), True),
    ]
    out_specs = [pl.BlockSpec((None, tm, n), lambda gi, i: (gi, i, 0))]
    out_shape = [jax.ShapeDtypeStruct((g, r, n), out_dtype)]
    scratch = []
    args = [x, norm_g.reshape(1, d), sh, sc, w]
    stream = None
    if km is not None:
        page_table, cache, stream = km
        assert stream.n_steps == g * (r // tm)
        in_specs += [pl.BlockSpec(memory_space=pltpu.SMEM), pl.BlockSpec(memory_space=pl.ANY)]
        km_spec, km_shape = _km_out(stream, cache, r // tm)
        out_specs.append(km_spec)
        out_shape.append(km_shape)
        scratch += _km_scratch(stream, cache)
        args += [page_table, cache]
    outs = pl.pallas_call(
        functools.partial(_ret_proj_kernel, col_chunk=512, stream=stream),
        grid=(g, r // tm),
        in_specs=in_specs,
        out_specs=out_specs,
        out_shape=out_shape,
        scratch_shapes=scratch,
        compiler_params=_params(("arbitrary", "arbitrary"), 58),
        name="ret_proj",
    )(*args)
    return outs if km is not None else outs[0]


def _out_proj_kernel(a_ref, w_ref, x_ref, gt_ref, o_ref, wb_scr):
    @pl.when((pl.program_id(0) == 0) & (pl.program_id(1) == 0))
    def _():
        _cast_weight(w_ref, wb_scr, 512)

    o_ref[...] = x_ref[...] + gt_ref[...] * jnp.dot(a_ref[...], wb_scr[...], preferred_element_type=F32)


def _out_proj(a, w, x, gate, tm):
    g, r, k = a.shape
    d = w.shape[1]
    return pl.pallas_call(
        _out_proj_kernel,
        grid=(g, r // tm),
        in_specs=[
            pl.BlockSpec((None, tm, k), lambda gi, i: (gi, i, 0)),
            _weight_spec((k, d), lambda gi, i: (0, 0), True),
            pl.BlockSpec((None, tm, d), lambda gi, i: (gi, i, 0)),
            _mod_spec(gate, tm),
        ],
        out_specs=pl.BlockSpec((None, tm, d), lambda gi, i: (gi, i, 0)),
        out_shape=jax.ShapeDtypeStruct((g, r, d), F32),
        scratch_shapes=[pltpu.VMEM((k, d), BF16)],
        compiler_params=_params(("arbitrary", "arbitrary"), 48),
        name="out_proj",
    )(a, w, x, gate)


def _pick(index, values):
    out = values[0]
    for k in range(1, len(values)):
        out = jnp.where(index == k, values[k], out)
    return out


def _ret_sample_heads(q_ref, k_ref, v_ref, g_ref, s0_ref, gn_ref, og_ref, s_ref, *, t, hps, part):
    ii = lax.broadcasted_iota(I32, (t, t), 0)
    jj = lax.broadcasted_iota(I32, (t, t), 1)
    diff = (ii - jj).astype(F32)
    pos = lax.broadcasted_iota(I32, (t, 1), 0).astype(F32)
    pad = -t % 8
    for j in range(hps):
        heads = [p * hps + j for p in range(RET_HEADS // hps)]
        lg = _pick(part, [_ret_log_gamma(h) for h in heads])
        chunk_decay = _pick(part, [math.exp(_ret_log_gamma(h) * t) for h in heads])
        decay = jnp.where(diff >= 0, jnp.exp(lg * jnp.maximum(diff, 0.0)), 0.0)
        q_decay = jnp.exp(lg * (pos + 1.0))
        k_decay = jnp.exp(lg * (t - 1.0 - pos))
        q = q_ref[:, j * RET_DK:(j + 1) * RET_DK]
        k = k_ref[:, j * RET_DK:(j + 1) * RET_DK] * (RET_DK ** -0.5)
        v = v_ref[:, j * RET_DV:(j + 1) * RET_DV]
        gate = g_ref[:, j * RET_DV:(j + 1) * RET_DV]
        state = s0_ref[j]
        o = jnp.dot(q * q_decay, state, preferred_element_type=F32)
        for i in range(t):
            s_i = jnp.sum(q * k[i:i + 1, :], axis=-1, keepdims=True) * decay[:, i:i + 1]
            o = o + s_i * v[i:i + 1, :]
        kd = jnp.concatenate([k * k_decay, jnp.zeros((pad, RET_DK), F32)], axis=0)
        vp = jnp.concatenate([v, jnp.zeros((pad, RET_DV), F32)], axis=0)
        kv = lax.dot_general(kd, vp, (((0,), (0,)), ((), ())), preferred_element_type=F32)
        s_ref[j] = chunk_decay * state + kv
        oc = o - jnp.mean(o, axis=-1, keepdims=True)
        y = oc * lax.rsqrt(jnp.mean(oc * oc, axis=-1, keepdims=True) + EPS) * gn_ref[pl.ds(part * hps + j, 1), :]
        og_ref[:, j * RET_DV:(j + 1) * RET_DV] = (y * _silu(gate)).astype(og_ref.dtype)


def _ret_mixer_kernel(p_ref, x_ref, gt_ref, gn_ref, w_ref, qs_ref, ks_ref, vs_ref, gs_ref, s0_ref,
                      y_ref, s_ref, ogs_ref, ss_ref, wb_scr, og_ref, *, chunk, n_chunks, t_s, hps):
    step = pl.program_id(0) * pl.num_programs(1) + pl.program_id(1)

    @pl.when(step == 0)
    def _():
        _cast_weight(w_ref, wb_scr, 512)

    @pl.when(pl.program_id(1) == 0)
    def _():
        s_ref[...] = jnp.zeros_like(s_ref)

    _ret_sample_heads(qs_ref, ks_ref, vs_ref, gs_ref, s0_ref, gn_ref, ogs_ref, ss_ref, t=t_s, hps=hps,
                      part=step % (RET_HEADS // hps))

    ii = lax.broadcasted_iota(I32, (chunk, chunk), 0)
    jj = lax.broadcasted_iota(I32, (chunk, chunk), 1)
    diff = (ii - jj).astype(F32)
    pos = lax.broadcasted_iota(I32, (chunk, 1), 0).astype(F32)
    for h in range(RET_HEADS):
        lg = _ret_log_gamma(h)
        decay = jnp.where(diff >= 0, jnp.exp(lg * jnp.maximum(diff, 0.0)), 0.0)
        q_decay = jnp.exp(lg * (pos + 1.0))
        k_decay = jnp.exp(lg * (chunk - 1.0 - pos))
        chunk_decay = math.exp(lg * chunk)
        for c in range(n_chunks):
            rows = slice(c * chunk, (c + 1) * chunk)
            q = p_ref[rows, h * RET_DK:(h + 1) * RET_DK]
            k = p_ref[rows, RET_QK + h * RET_DK:RET_QK + (h + 1) * RET_DK].astype(F32) * (RET_DK ** -0.5)
            v = p_ref[rows, 2 * RET_QK + h * RET_DV:2 * RET_QK + (h + 1) * RET_DV]
            gate = p_ref[rows, 2 * RET_QK + RET_V + h * RET_DV:2 * RET_QK + RET_V + (h + 1) * RET_DV].astype(F32)
            state = s_ref[h]
            scores = lax.dot_general(q, k.astype(BF16), NT_DIMS, preferred_element_type=F32) * decay
            inner = jnp.dot(scores.astype(BF16), v, preferred_element_type=F32)
            cross = jnp.dot((q.astype(F32) * q_decay).astype(BF16), state.astype(BF16), preferred_element_type=F32)
            kv = lax.dot_general((k * k_decay).astype(BF16), v, (((0,), (0,)), ((), ())), preferred_element_type=F32)
            s_ref[h] = chunk_decay * state + kv
            o = inner + cross
            oc = o - jnp.mean(o, axis=-1, keepdims=True)
            y = oc * lax.rsqrt(jnp.mean(oc * oc, axis=-1, keepdims=True) + EPS) * gn_ref[h:h + 1, :]
            og_ref[rows, h * RET_DV:(h + 1) * RET_DV] = (y * _silu(gate)).astype(og_ref.dtype)

    y_ref[...] = x_ref[...] + gt_ref[...] * jnp.dot(og_ref[...], wb_scr[...], preferred_element_type=F32)


def _ret_mixer(proj, x, gate, gn_g, w_out, proj_s, state0, rows_per_step):
    b, t, n = proj.shape
    d = x.shape[-1]
    bs, ts, _ = proj_s.shape
    assert t % RET_CHUNK == 0 and rows_per_step % RET_CHUNK == 0
    n_t = t // rows_per_step
    n_steps = b * n_t
    assert (bs * RET_HEADS) % n_steps == 0
    hps = bs * RET_HEADS // n_steps
    assert RET_HEADS % hps == 0
    per_sb = RET_HEADS // hps

    def sample_spec(width, first_col):
        first = first_col // (hps * width)
        return pl.BlockSpec((None, ts, hps * width),
                            lambda bi, i: ((bi * n_t + i) // per_sb, 0, first + (bi * n_t + i) % per_sb))

    state_s_spec = pl.BlockSpec((None, hps, RET_DK, RET_DV),
                                lambda bi, i: ((bi * n_t + i) // per_sb, (bi * n_t + i) % per_sb, 0, 0))
    return pl.pallas_call(
        functools.partial(_ret_mixer_kernel, chunk=RET_CHUNK, n_chunks=rows_per_step // RET_CHUNK, t_s=ts, hps=hps),
        grid=(b, n_t),
        in_specs=[
            pl.BlockSpec((None, rows_per_step, n), lambda bi, i: (bi, i, 0)),
            pl.BlockSpec((None, rows_per_step, d), lambda bi, i: (bi, i, 0)),
            _mod_spec(gate, rows_per_step),
            pl.BlockSpec((RET_HEADS, RET_DV), lambda bi, i: (0, 0)),
            _weight_spec((RET_V, d), lambda bi, i: (0, 0), True),
            sample_spec(RET_DK, 0),
            sample_spec(RET_DK, RET_QK),
            sample_spec(RET_DV, 2 * RET_QK),
            sample_spec(RET_DV, 2 * RET_QK + RET_V),
            state_s_spec,
        ],
        out_specs=[
            pl.BlockSpec((None, rows_per_step, d), lambda bi, i: (bi, i, 0)),
            pl.BlockSpec((None, RET_HEADS, RET_DK, RET_DV), lambda bi, i: (bi, 0, 0, 0)),
            sample_spec(RET_DV, 0),
            state_s_spec,
        ],
        out_shape=[
            jax.ShapeDtypeStruct((b, t, d), F32),
            jax.ShapeDtypeStruct((b, RET_HEADS, RET_DK, RET_DV), F32),
            jax.ShapeDtypeStruct((bs, ts, RET_V), F32),
            jax.ShapeDtypeStruct((bs, RET_HEADS, RET_DK, RET_DV), F32),
        ],
        scratch_shapes=[pltpu.VMEM((RET_V, d), BF16), pltpu.VMEM((rows_per_step, RET_V), BF16)],
        compiler_params=_params(("arbitrary", "arbitrary"), 56),
        name="ret_mixer",
    )(proj, x, gate, gn_g, w_out, proj_s, proj_s, proj_s, proj_s, state0)


def _moba_proj_kernel(*refs, stream):
    x_ref, g_ref, sh_ref, sc_ref, w_ref, qg_ref, kg_ref = refs[:7]
    if stream is None:
        q_ref, k_ref, v_ref, gt_ref, wb_scr = refs[7:]
        km_refs = None
    else:
        pt_ref, ck_ref, q_ref, k_ref, v_ref, gt_ref, km_ref, wb_scr, buf, sems = refs[7:]
        km_refs = (pt_ref, ck_ref, km_ref, buf, sems)

    @pl.when((pl.program_id(0) == 0) & (pl.program_id(1) == 0))
    def _():
        _cast_weight(w_ref, wb_scr, 512)

    h = _norm_mod(x_ref[...], g_ref[...], sh_ref[...], sc_ref[...]).astype(BF16)
    w = MOBA_W

    def q_part():
        q = jnp.dot(h, wb_scr[:, 0:w], preferred_element_type=F32)
        for hd in range(MOBA_HEADS):
            cs = slice(hd * MOBA_DH, (hd + 1) * MOBA_DH)
            q_ref[:, cs] = (_head_rms(q[:, cs], qg_ref[...]) * (MOBA_DH ** -0.5)).astype(q_ref.dtype)

    def k_part():
        k = jnp.dot(h, wb_scr[:, w:2 * w], preferred_element_type=F32)
        for hd in range(MOBA_HEADS):
            cs = slice(hd * MOBA_DH, (hd + 1) * MOBA_DH)
            k_ref[:, cs] = _head_rms(k[:, cs], kg_ref[...])

    def v_part():
        v_ref[...] = jnp.dot(h, wb_scr[:, 2 * w:3 * w], preferred_element_type=F32)

    def gate_part():
        gt_ref[...] = jnp.dot(h, wb_scr[:, 3 * w:4 * w], preferred_element_type=F32).astype(gt_ref.dtype)

    _run_with_km_stream(stream, km_refs, [q_part, k_part, v_part, gate_part])


def _moba_proj(x, norm_g, sh, sc, w, q_g, k_g, act_dtype, tm, km=None):
    g, r, d = x.shape
    n = w.shape[1]
    row_spec = pl.BlockSpec((None, tm, MOBA_W), lambda gi, i: (gi, i, 0))
    in_specs = [
        pl.BlockSpec((None, tm, d), lambda gi, i: (gi, i, 0)),
        pl.BlockSpec((1, d), lambda gi, i: (0, 0)),
        _mod_spec(sh, tm),
        _mod_spec(sc, tm),
        _weight_spec((d, n), lambda gi, i: (0, 0), True),
        pl.BlockSpec((1, MOBA_DH), lambda gi, i: (0, 0)),
        pl.BlockSpec((1, MOBA_DH), lambda gi, i: (0, 0)),
    ]
    out_specs = [row_spec, row_spec, row_spec, row_spec]
    out_shape = [
        jax.ShapeDtypeStruct((g, r, MOBA_W), act_dtype),
        jax.ShapeDtypeStruct((g, r, MOBA_W), F32),
        jax.ShapeDtypeStruct((g, r, MOBA_W), F32),
        jax.ShapeDtypeStruct((g, r, MOBA_W), act_dtype),
    ]
    scratch = [pltpu.VMEM((d, n), BF16)]
    args = [x, norm_g.reshape(1, d), sh, sc, w, q_g.reshape(1, MOBA_DH), k_g.reshape(1, MOBA_DH)]
    stream = None
    if km is not None:
        page_table, cache, stream = km
        assert stream.n_steps == g * (r // tm)
        in_specs += [pl.BlockSpec(memory_space=pltpu.SMEM), pl.BlockSpec(memory_space=pl.ANY)]
        km_spec, km_shape = _km_out(stream, cache, r // tm)
        out_specs.append(km_spec)
        out_shape.append(km_shape)
        scratch += _km_scratch(stream, cache)
        args += [page_table, cache]
    return pl.pallas_call(
        functools.partial(_moba_proj_kernel, stream=stream),
        grid=(g, r // tm),
        in_specs=in_specs,
        out_specs=out_specs,
        out_shape=out_shape,
        scratch_shapes=scratch,
        compiler_params=_params(("arbitrary", "arbitrary"), 56),
        name="moba_proj",
    )(*args)


def _alibi_slope(head):
    hv = jnp.full((1, 1), head, I32).astype(F32)
    return jnp.exp2(-8.0 * (hv + 1.0) / MOBA_HEADS)


AUG_PEN0 = 8


def _moba_prompt_head(q_ref, k_ref, v_ref, g_ref, o_ref, ka_scr, vt_scr, qa_scr, km_scr, *, n_blocks, head, side_work):
    blk = MOBA_BLOCK
    dh = MOBA_DH
    t_len = n_blocks * blk
    slope = _alibi_slope(head)
    pen_rows = km_scr.shape[0] // 4

    lane = lax.broadcasted_iota(I32, (blk, dh), 1)
    k_off = lax.broadcasted_iota(I32, (blk, dh), 0).astype(F32)
    km_scr[...] = jnp.zeros_like(km_scr)
    for n in range(n_blocks):
        rows = slice(n * blk, (n + 1) * blk)
        kn = k_ref[rows, :]
        ka_scr[rows, 0:dh] = kn.astype(BF16)
        aug = jnp.where(lane < 2, 1.0,
                        jnp.where(lane == 2, slope * float(blk * n),
                                  jnp.where(lane == 3, slope * k_off,
                                            jnp.where(lane == AUG_PEN0 + n, 1.0, 0.0))))
        ka_scr[rows, dh:2 * dh] = aug.astype(BF16)
        vt_scr[:, rows] = v_ref[rows, :].T.astype(BF16)
        mean = jnp.mean(kn, axis=0, keepdims=True)
        hi = mean.astype(BF16).astype(F32)
        mid = (mean - hi).astype(BF16).astype(F32)
        lo = (mean - hi - mid).astype(BF16).astype(F32)
        km_scr[n:n + 1, :] = hi
        km_scr[pen_rows + n:pen_rows + n + 1, :] = mid
        km_scr[2 * pen_rows + n:2 * pen_rows + n + 1, :] = lo
        qa_scr[0:dh, rows] = q_ref[rows, :].astype(F32).T.astype(BF16)

    parts = jnp.dot(km_scr[...].astype(BF16), qa_scr[0:dh, :], preferred_element_type=F32)
    gs = parts[0:pen_rows] + parts[pen_rows:2 * pen_rows] + parts[2 * pen_rows:3 * pen_rows]
    nid = lax.broadcasted_iota(I32, (pen_rows, t_len), 0)
    q_pos = lax.broadcasted_iota(I32, (pen_rows, t_len), 1)
    own = lax.shift_right_logical(q_pos, int(math.log2(blk)))
    past = nid < own
    attended = nid == own
    gs = jnp.where(past, gs, NEG)
    for _ in range(min(MOBA_TOPK, n_blocks)):
        mx = jnp.max(gs, axis=0, keepdims=True)
        idx = jnp.min(jnp.where(gs == mx, nid, pen_rows), axis=0, keepdims=True)
        pick = nid == idx
        attended = attended | (pick & past)
        gs = jnp.where(pick, BELOW_NEG, gs)
    pen = jnp.where(attended, 0.0, NEG)
    r8 = lax.broadcasted_iota(I32, (8, t_len), 0)
    q8 = lax.broadcasted_iota(I32, (8, t_len), 1)
    own8 = lax.shift_right_logical(q8, int(math.log2(blk)))
    bias = jnp.where(r8 == 0, -slope * (own8 * blk).astype(F32),
                     jnp.where(r8 == 1, -slope * (q8 - own8 * blk).astype(F32), jnp.where(r8 < 4, 1.0, 0.0)))
    extra = jnp.concatenate([bias, pen, jnp.zeros((dh - 8 - pen_rows, t_len), F32)], axis=0)
    qa_scr[dh:2 * dh, :] = extra.astype(BF16)

    def query_block(own):
        n_keys = (own + 1) * blk
        cols = slice(own * blk, n_keys)
        s = jnp.dot(ka_scr[0:n_keys, :], qa_scr[:, cols], preferred_element_type=F32)
        k_idx = lax.broadcasted_iota(I32, (blk, blk), 0)
        q_idx = lax.broadcasted_iota(I32, (blk, blk), 1)
        s_own = jnp.where(q_idx >= k_idx, s[own * blk:n_keys], NEG)
        m = jnp.max(s_own, axis=0, keepdims=True)
        if own > 0:
            s_past = s[0:own * blk]
            m = jnp.maximum(m, jnp.max(s_past, axis=0, keepdims=True))
        p = jnp.exp(s_own - m)
        l = jnp.sum(p, axis=0, keepdims=True)
        p = p.astype(BF16)
        if own > 0:
            p_past = jnp.exp(s_past - m)
            l = l + jnp.sum(p_past, axis=0, keepdims=True)
            p = jnp.concatenate([p_past.astype(BF16), p], axis=0)
        acc = jnp.dot(vt_scr[:, 0:n_keys], p, preferred_element_type=F32)
        o = (acc / l).T
        o_ref[cols, :] = (o * _silu(g_ref[cols, :].astype(F32))).astype(o_ref.dtype)

    for own in range(n_blocks):
        for work in side_work.get(own, ()):
            work()
        query_block(own)


class _SampleAttn(NamedTuple):
    units: int
    t: int
    past_len: int
    n_sel: int


def _sample_slab_copies(ck_ref, cv_ref, kbuf, vbuf, sems, page, head, slot, j):
    return (pltpu.make_async_copy(ck_ref.at[page, :, head, :], kbuf.at[slot, j], sems.at[0, slot]),
            pltpu.make_async_copy(cv_ref.at[page, :, head, :], vbuf.at[slot, j], sems.at[1, slot]))


def _sample_start(cfg, pt_ref, sel_ref, ck_ref, cv_ref, kbuf, vbuf, sems, step, slot):
    for u in range(cfg.units):
        unit = step * cfg.units + u
        sb = unit // MOBA_HEADS
        sh = unit % MOBA_HEADS
        for i in range(cfg.t):
            for s in range(MOBA_TOPK):
                blk_id = sel_ref[sb, (sh * cfg.t + i) * MOBA_TOPK + s]
                for p in range(PAGES_PER_BLOCK):
                    page = pt_ref[sb, blk_id * PAGES_PER_BLOCK + p]
                    j = u * cfg.n_sel + (i * MOBA_TOPK + s) * PAGES_PER_BLOCK + p
                    for c in _sample_slab_copies(ck_ref, cv_ref, kbuf, vbuf, sems, page, sh, slot, j):
                        c.start()


def _sample_wait(cfg, ck_ref, cv_ref, kbuf, vbuf, sems, slot):
    n = cfg.units * cfg.n_sel
    pltpu.make_async_copy(ck_ref.at[pl.ds(0, n), :, 0, :], kbuf.at[slot], sems.at[0, slot]).wait()
    pltpu.make_async_copy(cv_ref.at[pl.ds(0, n), :, 0, :], vbuf.at[slot], sems.at[1, slot]).wait()


def _sample_unit_stages(cfg, sel_ref, q_ref, kn_ref, vn_ref, g_ref, o_ref, kbuf, vbuf, slot, u, sb, sh):
    t = cfg.t
    cols = slice(u * MOBA_DH, (u + 1) * MOBA_DH)
    keys_per_query = cfg.n_sel // t * PAGE_SIZE
    n_keys = cfg.n_sel * PAGE_SIZE
    vals = {}

    def scores():
        slope = _alibi_slope(sh)
        q = q_ref[:, cols]
        kn = kn_ref[:, cols]
        k_all = kbuf[slot, u * cfg.n_sel:(u + 1) * cfg.n_sel].reshape(n_keys, MOBA_DH)
        off = lax.broadcasted_iota(I32, (1, PAGE_SIZE), 1)
        k_pos = jnp.concatenate(
            [sel_ref[sb, (sh * t + i) * MOBA_TOPK + s] * MOBA_BLOCK + p * PAGE_SIZE + off
             for i in range(t) for s in range(MOBA_TOPK) for p in range(PAGES_PER_BLOCK)], axis=1)
        row = lax.broadcasted_iota(I32, (t, n_keys), 0)
        col = lax.broadcasted_iota(I32, (t, n_keys), 1)
        owned = (col >= row * keys_per_query) & (col < (row + 1) * keys_per_query)
        dist = (cfg.past_len + row) - k_pos
        s_sel = lax.dot_general(q, k_all, NT_DIMS, preferred_element_type=F32) - slope * dist.astype(F32)
        vals["s_sel"] = jnp.where(owned & (dist >= 0), s_sel, NEG)
        s_new = jnp.concatenate([jnp.sum(q * kn[j:j + 1, :], axis=-1, keepdims=True) for j in range(t)], axis=1)
        d_new = lax.broadcasted_iota(I32, (t, t), 0) - lax.broadcasted_iota(I32, (t, t), 1)
        vals["s_new"] = jnp.where(d_new >= 0, s_new - slope * d_new.astype(F32), NEG)

    def softmax():
        s_sel, s_new = vals["s_sel"], vals["s_new"]
        m = jnp.maximum(jnp.max(s_sel, axis=1, keepdims=True), jnp.max(s_new, axis=1, keepdims=True))
        vals["p_sel"] = jnp.exp(s_sel - m)
        vals["p_new"] = jnp.exp(s_new - m)
        vals["l"] = jnp.sum(vals["p_sel"], axis=1, keepdims=True) + jnp.sum(vals["p_new"], axis=1, keepdims=True)

    def values():
        vn = vn_ref[:, cols]
        v_all = vbuf[slot, u * cfg.n_sel:(u + 1) * cfg.n_sel].reshape(n_keys, MOBA_DH)
        o = jnp.dot(vals["p_sel"], v_all, preferred_element_type=F32)
        for j in range(t):
            o = o + vals["p_new"][:, j:j + 1] * vn[j:j + 1, :]
        o_ref[:, cols] = (o / vals["l"]) * _silu(g_ref[:, cols])

    return scores, softmax, values


def _moba_attn_kernel(pt_ref, sel_ref, q_ref, k_ref, v_ref, g_ref, qs_ref, kn_ref, vn_ref, gs_ref, ck_ref, cv_ref,
                      o_ref, os_ref, ka_scr, vt_scr, qa_scr, km_scr, kbuf, vbuf, sems, *, n_blocks, cfg):
    head = pl.program_id(1)
    n_steps = pl.num_programs(0) * pl.num_programs(1)
    step = pl.program_id(0) * pl.num_programs(1) + head
    slot = step % 2

    @pl.when(step == 0)
    def _():
        _sample_start(cfg, pt_ref, sel_ref, ck_ref, cv_ref, kbuf, vbuf, sems, step, 0)

    _sample_wait(cfg, ck_ref, cv_ref, kbuf, vbuf, sems, slot)
    _sample_start(cfg, pt_ref, sel_ref, ck_ref, cv_ref, kbuf, vbuf, sems, jnp.where(step + 1 == n_steps, 0, step + 1),
                  1 - slot)

    unit0 = step * cfg.units
    stages = [_sample_unit_stages(cfg, sel_ref, qs_ref, kn_ref, vn_ref, gs_ref, os_ref, kbuf, vbuf, slot, u,
                                  unit0 // MOBA_HEADS, unit0 % MOBA_HEADS + u) for u in range(cfg.units)]
    gap = max(1, n_blocks // 4)
    side_work = {k * gap: [unit[k] for unit in stages] for k in range(3)}
    _moba_prompt_head(q_ref, k_ref, v_ref, g_ref, o_ref, ka_scr, vt_scr, qa_scr, km_scr, n_blocks=n_blocks, head=head,
                      side_work=side_work)

    @pl.when(step == n_steps - 1)
    def _():
        _sample_wait(cfg, ck_ref, cv_ref, kbuf, vbuf, sems, 1 - slot)


def _moba_attn(q, k, v, gate, q_s, k_s, v_s, g_s, cache_k_pages, cache_v_pages, page_table, sel, past_len):
    b, t, w = q.shape
    bs, ts, _ = q_s.shape
    assert t % MOBA_BLOCK == 0
    n_blocks = t // MOBA_BLOCK
    pen_rows = -(-n_blocks // 8) * 8
    assert AUG_PEN0 + pen_rows <= MOBA_DH
    assert MOBA_BLOCK & (MOBA_BLOCK - 1) == 0
    n_steps = b * MOBA_HEADS
    assert (bs * MOBA_HEADS) % n_steps == 0
    units = bs * MOBA_HEADS // n_steps
    assert MOBA_HEADS % units == 0 and n_blocks >= 3
    cfg = _SampleAttn(units=units, t=ts, past_len=past_len, n_sel=ts * MOBA_TOPK * PAGES_PER_BLOCK)
    steps_per_sb = MOBA_HEADS // units
    spec = pl.BlockSpec((None, t, MOBA_DH), lambda bi, h: (bi, 0, h))
    sspec = pl.BlockSpec((None, ts, units * MOBA_DH),
                         lambda bi, h: ((bi * MOBA_HEADS + h) // steps_per_sb, 0, (bi * MOBA_HEADS + h) % steps_per_sb))
    smem = pl.BlockSpec(memory_space=pltpu.SMEM)
    hbm = pl.BlockSpec(memory_space=pl.ANY)
    slab = (2, units * cfg.n_sel, PAGE_SIZE, MOBA_DH)
    return pl.pallas_call(
        functools.partial(_moba_attn_kernel, n_blocks=n_blocks, cfg=cfg),
        grid=(b, MOBA_HEADS),
        in_specs=[smem, smem, spec, spec, spec, spec, sspec, sspec, sspec, sspec, hbm, hbm],
        out_specs=[spec, sspec],
        out_shape=[jax.ShapeDtypeStruct((b, t, w), BF16), jax.ShapeDtypeStruct((bs, ts, w), F32)],
        scratch_shapes=[
            pltpu.VMEM((t, 2 * MOBA_DH), BF16),
            pltpu.VMEM((MOBA_DH, t), BF16),
            pltpu.VMEM((2 * MOBA_DH, t), BF16),
            pltpu.VMEM((4 * pen_rows, MOBA_DH), F32),
            pltpu.VMEM(slab, F32),
            pltpu.VMEM(slab, F32),
            pltpu.SemaphoreType.DMA((2, 2)),
        ],
        compiler_params=_params(("arbitrary", "arbitrary"), 56),
        name="moba_attn",
    )(page_table, sel, q, k, v, gate, q_s, k_s, v_s, g_s, cache_k_pages, cache_v_pages)


def _moba_select_kernel(q_ref, km_ref, sel_ref, *, t, topk):
    n_blocks = km_ref.shape[0]
    rows = sel_ref.shape[0]
    rid = lax.broadcasted_iota(I32, (rows, 128), 0)
    lid = lax.broadcasted_iota(I32, (rows, 128), 1)
    nid = lax.broadcasted_iota(I32, (n_blocks, 1), 0)
    out = jnp.zeros((rows, 128), I32)
    for h in range(MOBA_HEADS):
        cs = slice(h * MOBA_DH, (h + 1) * MOBA_DH)
        km = km_ref[:, h, :]
        for i in range(t):
            gs = jnp.sum(km * q_ref[i:i + 1, cs], axis=-1, keepdims=True)
            for r in range(topk):
                mx = jnp.max(gs, axis=0, keepdims=True)
                idx = jnp.min(jnp.where(gs == mx, nid, n_blocks), axis=0, keepdims=True)
                out = jnp.where((rid == h * t + i) & (lid == r), idx, out)
                gs = jnp.where(nid == idx, BELOW_NEG, gs)
    sel_ref[...] = out


def _moba_select(q, kmean):
    b, t, w = q.shape
    n_blocks = kmean.shape[1]
    rows = MOBA_HEADS * t
    return pl.pallas_call(
        functools.partial(_moba_select_kernel, t=t, topk=MOBA_TOPK),
        grid=(b,),
        in_specs=[
            pl.BlockSpec((None, t, w), lambda bi: (bi, 0, 0)),
            pl.BlockSpec((None, n_blocks, MOBA_HEADS, MOBA_DH), lambda bi: (bi, 0, 0, 0)),
        ],
        out_specs=pl.BlockSpec((None, rows, 128), lambda bi: (bi, 0, 0)),
        out_shape=jax.ShapeDtypeStruct((b, rows, 128), I32),
        compiler_params=_params(("arbitrary",), 32),
        name="moba_select",
    )(q, kmean)


def kernel(x_prompt, x_sample, c_prompt, c_sample, state_ret, cache_k, cache_v, page_table, norm_g, w_ada, b_ada,
           w_ret_in, ret_gn_g, w_ret_out, w_moba_in, moba_q_g, moba_k_g, w_moba_out):
    bp, tp, d = x_prompt.shape
    bs, ts, _ = x_sample.shape
    n_pages = page_table.shape[1]
    past_len = n_pages * PAGE_SIZE
    assert past_len % MOBA_BLOCK == 0 and ts <= MOBA_BLOCK and past_len // MOBA_BLOCK >= MOBA_TOPK
    assert w_ada.shape[0] == 2 and w_ret_in.shape[0] == 1 and w_moba_in.shape[0] == 1

    n_c = bp + bs
    c_all = jnp.concatenate([c_prompt, c_sample, jnp.zeros((-n_c % 8, d), F32)], axis=0)
    mod = _ada(c_all, w_ada, b_ada)

    def group_mod(layer):
        mp = mod[layer, :bp].reshape(bp, 1, 3 * d)
        ms = jnp.repeat(mod[layer, bp:n_c], ts, axis=0).reshape(1, bs * ts, 3 * d)
        return [(m[..., :d], m[..., d:2 * d], m[..., 2 * d:]) for m in (mp, ms)]

    xs = x_sample.reshape(1, bs * ts, d)
    ck = cache_k[0]
    cv = cache_v[0]

    tm_proj = 256
    n_steps = bp * (tp // tm_proj)
    n_blocks_past = bs * n_pages // PAGES_PER_BLOCK
    assert n_pages % PAGES_PER_BLOCK == 0 and n_blocks_past % n_steps == 0 and n_blocks_past // n_steps >= 2
    bps = n_blocks_past // n_steps
    bps_ret = min(bps - 1, -(-bps * 5 // 8))
    stream_ret = _KmStream(0, bps_ret * PAGES_PER_BLOCK, n_steps, n_pages)
    stream_moba = _KmStream(bps_ret * PAGES_PER_BLOCK * n_steps, (bps - bps_ret) * PAGES_PER_BLOCK, n_steps, n_pages)
    w_ret_in_b = w_ret_in[0].astype(BF16)

    (sh_p, sc_p, gt_p), (sh_s, sc_s, gt_s) = group_mod(0)
    proj_p, km_a = _ret_proj(x_prompt, norm_g[0], sh_p, sc_p, w_ret_in_b, BF16, tm=tm_proj,
                             km=(page_table, ck, stream_ret))
    proj_s = _ret_proj(xs, norm_g[0], sh_s, sc_s, w_ret_in_b, F32, tm=bs * ts).reshape(bs, ts, -1)
    y_p, ret_p, og_s, ret_s = _ret_mixer(proj_p, x_prompt, gt_p, ret_gn_g[0], w_ret_out[0], proj_s, state_ret[0],
                                         rows_per_step=512)
    y_s = _out_proj(og_s.reshape(1, bs * ts, RET_V).astype(BF16), w_ret_out[0], xs, gt_s, tm=bs * ts)

    (sh_p, sc_p, gt_p), (sh_s, sc_s, gt_s) = group_mod(1)
    q_p, k_p, v_p, g_p, km_b = _moba_proj(y_p, norm_g[1], sh_p, sc_p, w_moba_in[0], moba_q_g[0], moba_k_g[0], BF16,
                                          tm=tm_proj, km=(page_table, ck, stream_moba))
    q_s, k_s, v_s, g_s = _moba_proj(y_s, norm_g[1], sh_s, sc_s, w_moba_in[0], moba_q_g[0], moba_k_g[0], F32,
                                    tm=bs * ts)
    q_s, k_s, v_s, g_s = (a.reshape(bs, ts, MOBA_W) for a in (q_s, k_s, v_s, g_s))
    kmean = jnp.concatenate([km_a, km_b], axis=0).reshape(bs, past_len // MOBA_BLOCK, MOBA_HEADS, MOBA_DH)
    sel = _moba_select(q_s, kmean)[:, :, :MOBA_TOPK].reshape(bs, MOBA_HEADS * ts * MOBA_TOPK)
    oa_p, oa_s = _moba_attn(q_p, k_p, v_p, g_p, q_s, k_s, v_s, g_s, ck, cv, page_table, sel, past_len)
    y_p = _out_proj(oa_p, w_moba_out[0], y_p, gt_p, tm=1024)
    y_s = _out_proj(oa_s.reshape(1, bs * ts, MOBA_W).astype(BF16), w_moba_out[0], y_s, gt_s, tm=bs * ts)

    hd = (MOBA_HEADS, MOBA_DH)
    return (y_p, y_s.reshape(bs, ts, d), ret_p[None], ret_s[None],
            k_p.reshape(1, bp, tp, *hd), v_p.reshape(1, bp, tp, *hd),
            k_s.reshape(1, bs, ts, *hd), v_s.reshape(1, bs, ts, *hd))
```

```python
import functools
import math
from typing import NamedTuple

import jax
import jax.numpy as jnp
from jax import lax
from jax.experimental import pallas as pl
from jax.experimental.pallas import tpu as pltpu

F32 = jnp.float32
BF16 = jnp.bfloat16
I32 = jnp.int32

RET_HEADS = 4
RET_DK = 256
RET_DV = 512
RET_CHUNK = 256
RET_QK = RET_HEADS * RET_DK
RET_V = RET_HEADS * RET_DV
MOBA_HEADS = 8
MOBA_DH = 128
MOBA_W = MOBA_HEADS * MOBA_DH
MOBA_BLOCK = 256
MOBA_TOPK = 3
PAGE_SIZE = 128
EPS = 1e-6
NEG = -1e30
BELOW_NEG = -3e38

V7X_VMEM_BYTES = 64 * 1024 * 1024
NT_DIMS = (((1,), (1,)), ((), ()))


def _params(semantics, vmem_mb):
    assert vmem_mb * 1024 * 1024 < V7X_VMEM_BYTES
    return pltpu.CompilerParams(dimension_semantics=semantics, vmem_limit_bytes=vmem_mb * 1024 * 1024)


def _silu(x):
    return x * jax.nn.sigmoid(x)


def _norm_mod(x, g, sh, sc):
    ms = jnp.mean(x * x, axis=-1, keepdims=True)
    y = x * lax.rsqrt(ms + EPS) * g
    return y * (1.0 + sc) + sh


def _head_rms(x, g):
    ms = jnp.mean(x * x, axis=-1, keepdims=True)
    return x * lax.rsqrt(ms + EPS) * g


def _ret_log_gamma(h):
    return math.log1p(-(2.0 ** (-5.0 - h)))


def _ada_kernel(c_ref, w_ref, b_ref, o_ref):
    s = _silu(c_ref[...]).astype(BF16)
    o_ref[...] = jnp.dot(s, w_ref[...].astype(BF16), preferred_element_type=F32) + b_ref[...]


def _ada(c_all, w_ada, b_ada):
    depth, d, n = w_ada.shape
    m = c_all.shape[0]
    tn = 512
    return pl.pallas_call(
        _ada_kernel,
        grid=(depth, n // tn),
        in_specs=[
            pl.BlockSpec((m, d), lambda l, j: (0, 0)),
            pl.BlockSpec((None, d, tn), lambda l, j: (l, 0, j)),
            pl.BlockSpec((None, 1, tn), lambda l, j: (l, 0, j)),
        ],
        out_specs=pl.BlockSpec((None, m, tn), lambda l, j: (l, 0, j)),
        out_shape=jax.ShapeDtypeStruct((depth, m, n), F32),
        compiler_params=_params(("arbitrary", "arbitrary"), 32),
        name="ada",
    )(c_all, w_ada, b_ada.reshape(depth, 1, n))


def _mod_spec(mod, tm):
    d = mod.shape[-1]
    if mod.shape[1] == 1:
        return pl.BlockSpec((None, 1, d), lambda g, i, *_: (g, 0, 0))
    return pl.BlockSpec((None, tm, d), lambda g, i, *_: (g, i, 0))


def _weight_spec(shape, index_map, resident):
    if resident:
        return pl.BlockSpec(shape, index_map, pipeline_mode=pl.Buffered(1))
    return pl.BlockSpec(shape, index_map)


def _cast_weight(w_ref, wb_scr, col_chunk):
    for c in range(w_ref.shape[1] // col_chunk):
        cs = slice(c * col_chunk, (c + 1) * col_chunk)
        wb_scr[:, cs] = w_ref[:, cs].astype(BF16)


PAGES_PER_BLOCK = MOBA_BLOCK // PAGE_SIZE


class _KmStream(NamedTuple):
    page0: int
    pages_per_step: int
    n_steps: int
    n_pages: int


def _km_scratch(stream, cache):
    return [pltpu.VMEM((2, stream.pages_per_step, *cache.shape[1:]), F32), pltpu.SemaphoreType.DMA((2,))]


def _km_out(stream, cache, steps_per_group):
    hd = cache.shape[2:]
    blocks_per_step = stream.pages_per_step // PAGES_PER_BLOCK
    spec = pl.BlockSpec((blocks_per_step, *hd), lambda gi, i: (gi * steps_per_group + i, 0, 0))
    return spec, jax.ShapeDtypeStruct((stream.n_steps * blocks_per_step, *hd), F32)


def _km_copy(ck_ref, buf, sems, page, slot, p):
    return pltpu.make_async_copy(ck_ref.at[page], buf.at[slot, p], sems.at[slot])


def _km_start(stream, pt_ref, ck_ref, buf, sems, step, slot):
    for p in range(stream.pages_per_step):
        flat = stream.page0 + step * stream.pages_per_step + p
        page = pt_ref[flat // stream.n_pages, flat % stream.n_pages]
        _km_copy(ck_ref, buf, sems, page, slot, p).start()


def _km_wait(stream, ck_ref, buf, sems, slot):
    pltpu.make_async_copy(ck_ref.at[pl.ds(0, stream.pages_per_step)], buf.at[slot], sems.at[slot]).wait()


def _run_with_km_stream(stream, km_refs, chunks):
    if stream is not None:
        pt_ref, ck_ref, km_ref, buf, sems = km_refs
        step = pl.program_id(0) * pl.num_programs(1) + pl.program_id(1)
        slot = step % 2

        @pl.when(step == 0)
        def _():
            _km_start(stream, pt_ref, ck_ref, buf, sems, step, 0)

        _km_wait(stream, ck_ref, buf, sems, slot)
        _km_start(stream, pt_ref, ck_ref, buf, sems, jnp.where(step + 1 == stream.n_steps, 0, step + 1), 1 - slot)
        for blk in range(stream.pages_per_step // PAGES_PER_BLOCK):
            tot = jnp.sum(buf[slot, blk * PAGES_PER_BLOCK], axis=0)
            for p in range(1, PAGES_PER_BLOCK):
                tot = tot + jnp.sum(buf[slot, blk * PAGES_PER_BLOCK + p], axis=0)
            km_ref[blk] = tot / float(MOBA_BLOCK)

    for chunk in chunks:
        chunk()

    if stream is not None:
        @pl.when(step == stream.n_steps - 1)
        def _():
            _km_wait(stream, ck_ref, buf, sems, 1 - slot)


def _ret_proj_kernel(*refs, col_chunk, stream):
    x_ref, g_ref, sh_ref, sc_ref, w_ref = refs[:5]
    if stream is None:
        (o_ref,) = refs[5:]
        km_refs = None
    else:
        pt_ref, ck_ref, o_ref, km_ref, buf, sems = refs[5:]
        km_refs = (pt_ref, ck_ref, km_ref, buf, sems)

    h = _norm_mod(x_ref[...], g_ref[...], sh_ref[...], sc_ref[...]).astype(BF16)

    def column_chunk(c):
        cs = slice(c * col_chunk, (c + 1) * col_chunk)
        o_ref[:, cs] = jnp.dot(h, w_ref[:, cs], preferred_element_type=F32).astype(o_ref.dtype)

    _run_with_km_stream(stream, km_refs,
                        [functools.partial(column_chunk, c) for c in range(w_ref.shape[1] // col_chunk)])


def _ret_proj(x, norm_g, sh, sc, w, out_dtype, tm, km=None):
    g, r, d = x.shape
    n = w.shape[1]
    assert w.dtype == BF16
    in_specs = [
        pl.BlockSpec((None, tm, d), lambda gi, i: (gi, i, 0)),
        pl.BlockSpec((1, d), lambda gi, i: (0, 0)),
        _mod_spec(sh, tm),
        _mod_spec(sc, tm),
        _weight_spec((d, n), lambda gi, i: (0, 0), True),
    ]
    out_specs = [pl.BlockSpec((None, tm, n), lambda gi, i: (gi, i, 0))]
    out_shape = [jax.ShapeDtypeStruct((g, r, n), out_dtype)]
    scratch = []
    args = [x, norm_g.reshape(1, d), sh, sc, w]
    stream = None
    if km is not None:
        page_table, cache, stream = km
        assert stream.n_steps == g * (r // tm)
        in_specs += [pl.BlockSpec(memory_space=pltpu.SMEM), pl.BlockSpec(memory_space=pl.ANY)]
        km_spec, km_shape = _km_out(stream, cache, r // tm)
        out_specs.append(km_spec)
        out_shape.append(km_shape)
        scratch += _km_scratch(stream, cache)
        args += [page_table, cache]
    outs = pl.pallas_call(
        functools.partial(_ret_proj_kernel, col_chunk=512, stream=stream),
        grid=(g, r // tm),
        in_specs=in_specs,
        out_specs=out_specs,
        out_shape=out_shape,
        scratch_shapes=scratch,
        compiler_params=_params(("arbitrary", "arbitrary"), 58),
        name="ret_proj",
    )(*args)
    return outs if km is not None else outs[0]


def _out_proj_kernel(a_ref, w_ref, x_ref, gt_ref, o_ref, wb_scr):
    @pl.when((pl.program_id(0) == 0) & (pl.program_id(1) == 0))
    def _():
        _cast_weight(w_ref, wb_scr, 512)

    o_ref[...] = x_ref[...] + gt_ref[...] * jnp.dot(a_ref[...], wb_scr[...], preferred_element_type=F32)


def _out_proj(a, w, x, gate, tm):
    g, r, k = a.shape
    d = w.shape[1]
    return pl.pallas_call(
        _out_proj_kernel,
        grid=(g, r // tm),
        in_specs=[
            pl.BlockSpec((None, tm, k), lambda gi, i: (gi, i, 0)),
            _weight_spec((k, d), lambda gi, i: (0, 0), True),
            pl.BlockSpec((None, tm, d), lambda gi, i: (gi, i, 0)),
            _mod_spec(gate, tm),
        ],
        out_specs=pl.BlockSpec((None, tm, d), lambda gi, i: (gi, i, 0)),
        out_shape=jax.ShapeDtypeStruct((g, r, d), F32),
        scratch_shapes=[pltpu.VMEM((k, d), BF16)],
        compiler_params=_params(("arbitrary", "arbitrary"), 48),
        name="out_proj",
    )(a, w, x, gate)


def _pick(index, values):
    out = values[0]
    for k in range(1, len(values)):
        out = jnp.where(index == k, values[k], out)
    return out


def _ret_sample_heads(q_ref, k_ref, v_ref, g_ref, s0_ref, gn_ref, og_ref, s_ref, *, t, hps, part):
    ii = lax.broadcasted_iota(I32, (t, t), 0)
    jj = lax.broadcasted_iota(I32, (t, t), 1)
    diff = (ii - jj).astype(F32)
    pos = lax.broadcasted_iota(I32, (t, 1), 0).astype(F32)
    pad = -t % 8
    for j in range(hps):
        heads = [p * hps + j for p in range(RET_HEADS // hps)]
        lg = _pick(part, [_ret_log_gamma(h) for h in heads])
        chunk_decay = _pick(part, [math.exp(_ret_log_gamma(h) * t) for h in heads])
        decay = jnp.where(diff >= 0, jnp.exp(lg * jnp.maximum(diff, 0.0)), 0.0)
        q_decay = jnp.exp(lg * (pos + 1.0))
        k_decay = jnp.exp(lg * (t - 1.0 - pos))
        q = q_ref[:, j * RET_DK:(j + 1) * RET_DK]
        k = k_ref[:, j * RET_DK:(j + 1) * RET_DK] * (RET_DK ** -0.5)
        v = v_ref[:, j * RET_DV:(j + 1) * RET_DV]
        gate = g_ref[:, j * RET_DV:(j + 1) * RET_DV]
        state = s0_ref[j]
        o = jnp.dot(q * q_decay, state, preferred_element_type=F32)
        for i in range(t):
            s_i = jnp.sum(q * k[i:i + 1, :], axis=-1, keepdims=True) * decay[:, i:i + 1]
            o = o + s_i * v[i:i + 1, :]
        kd = jnp.concatenate([k * k_decay, jnp.zeros((pad, RET_DK), F32)], axis=0)
        vp = jnp.concatenate([v, jnp.zeros((pad, RET_DV), F32)], axis=0)
        kv = lax.dot_general(kd, vp, (((0,), (0,)), ((), ())), preferred_element_type=F32)
        s_ref[j] = chunk_decay * state + kv
        oc = o - jnp.mean(o, axis=-1, keepdims=True)
        y = oc * lax.rsqrt(jnp.mean(oc * oc, axis=-1, keepdims=True) + EPS) * gn_ref[pl.ds(part * hps + j, 1), :]
        og_ref[:, j * RET_DV:(j + 1) * RET_DV] = (y * _silu(gate)).astype(og_ref.dtype)


def _ret_mixer_kernel(p_ref, x_ref, gt_ref, gn_ref, w_ref, qs_ref, ks_ref, vs_ref, gs_ref, s0_ref,
                      y_ref, s_ref, ogs_ref, ss_ref, wb_scr, og_ref, *, chunk, n_chunks, t_s, hps):
    step = pl.program_id(0) * pl.num_programs(1) + pl.program_id(1)

    @pl.when(step == 0)
    def _():
        _cast_weight(w_ref, wb_scr, 512)

    @pl.when(pl.program_id(1) == 0)
    def _():
        s_ref[...] = jnp.zeros_like(s_ref)

    _ret_sample_heads(qs_ref, ks_ref, vs_ref, gs_ref, s0_ref, gn_ref, ogs_ref, ss_ref, t=t_s, hps=hps,
                      part=step % (RET_HEADS // hps))

    ii = lax.broadcasted_iota(I32, (chunk, chunk), 0)
    jj = lax.broadcasted_iota(I32, (chunk, chunk), 1)
    diff = (ii - jj).astype(F32)
    pos = lax.broadcasted_iota(I32, (chunk, 1), 0).astype(F32)
    for h in range(RET_HEADS):
        lg = _ret_log_gamma(h)
        decay = jnp.where(diff >= 0, jnp.exp(lg * jnp.maximum(diff, 0.0)), 0.0)
        q_decay = jnp.exp(lg * (pos + 1.0))
        k_decay = jnp.exp(lg * (chunk - 1.0 - pos))
        chunk_decay = math.exp(lg * chunk)
        for c in range(n_chunks):
            rows = slice(c * chunk, (c + 1) * chunk)
            q = p_ref[rows, h * RET_DK:(h + 1) * RET_DK]
            k = p_ref[rows, RET_QK + h * RET_DK:RET_QK + (h + 1) * RET_DK].astype(F32) * (RET_DK ** -0.5)
            v = p_ref[rows, 2 * RET_QK + h * RET_DV:2 * RET_QK + (h + 1) * RET_DV]
            gate = p_ref[rows, 2 * RET_QK + RET_V + h * RET_DV:2 * RET_QK + RET_V + (h + 1) * RET_DV].astype(F32)
            state = s_ref[h]
            scores = lax.dot_general(q, k.astype(BF16), NT_DIMS, preferred_element_type=F32) * decay
            inner = jnp.dot(scores.astype(BF16), v, preferred_element_type=F32)
            cross = jnp.dot((q.astype(F32) * q_decay).astype(BF16), state.astype(BF16), preferred_element_type=F32)
            kv = lax.dot_general((k * k_decay).astype(BF16), v, (((0,), (0,)), ((), ())), preferred_element_type=F32)
            s_ref[h] = chunk_decay * state + kv
            o = inner + cross
            oc = o - jnp.mean(o, axis=-1, keepdims=True)
            y = oc * lax.rsqrt(jnp.mean(oc * oc, axis=-1, keepdims=True) + EPS) * gn_ref[h:h + 1, :]
            og_ref[rows, h * RET_DV:(h + 1) * RET_DV] = (y * _silu(gate)).astype(og_ref.dtype)

    y_ref[...] = x_ref[...] + gt_ref[...] * jnp.dot(og_ref[...], wb_scr[...], preferred_element_type=F32)


def _ret_mixer(proj, x, gate, gn_g, w_out, proj_s, state0, rows_per_step):
    b, t, n = proj.shape
    d = x.shape[-1]
    bs, ts, _ = proj_s.shape
    assert t % RET_CHUNK == 0 and rows_per_step % RET_CHUNK == 0
    n_t = t // rows_per_step
    n_steps = b * n_t
    assert (bs * RET_HEADS) % n_steps == 0
    hps = bs * RET_HEADS // n_steps
    assert RET_HEADS % hps == 0
    per_sb = RET_HEADS // hps

    def sample_spec(width, first_col):
        first = first_col // (hps * width)
        return pl.BlockSpec((None, ts, hps * width),
                            lambda bi, i: ((bi * n_t + i) // per_sb, 0, first + (bi * n_t + i) % per_sb))

    state_s_spec = pl.BlockSpec((None, hps, RET_DK, RET_DV),
                                lambda bi, i: ((bi * n_t + i) // per_sb, (bi * n_t + i) % per_sb, 0, 0))
    return pl.pallas_call(
        functools.partial(_ret_mixer_kernel, chunk=RET_CHUNK, n_chunks=rows_per_step // RET_CHUNK, t_s=ts, hps=hps),
        grid=(b, n_t),
        in_specs=[
            pl.BlockSpec((None, rows_per_step, n), lambda bi, i: (bi, i, 0)),
            pl.BlockSpec((None, rows_per_step, d), lambda bi, i: (bi, i, 0)),
            _mod_spec(gate, rows_per_step),
            pl.BlockSpec((RET_HEADS, RET_DV), lambda bi, i: (0, 0)),
            _weight_spec((RET_V, d), lambda bi, i: (0, 0), True),
            sample_spec(RET_DK, 0),
            sample_spec(RET_DK, RET_QK),
            sample_spec(RET_DV, 2 * RET_QK),
            sample_spec(RET_DV, 2 * RET_QK + RET_V),
            state_s_spec,
        ],
        out_specs=[
            pl.BlockSpec((None, rows_per_step, d), lambda bi, i: (bi, i, 0)),
            pl.BlockSpec((None, RET_HEADS, RET_DK, RET_DV), lambda bi, i: (bi, 0, 0, 0)),
            sample_spec(RET_DV, 0),
            state_s_spec,
        ],
        out_shape=[
            jax.ShapeDtypeStruct((b, t, d), F32),
            jax.ShapeDtypeStruct((b, RET_HEADS, RET_DK, RET_DV), F32),
            jax.ShapeDtypeStruct((bs, ts, RET_V), F32),
            jax.ShapeDtypeStruct((bs, RET_HEADS, RET_DK, RET_DV), F32),
        ],
        scratch_shapes=[pltpu.VMEM((RET_V, d), BF16), pltpu.VMEM((rows_per_step, RET_V), BF16)],
        compiler_params=_params(("arbitrary", "arbitrary"), 56),
        name="ret_mixer",
    )(proj, x, gate, gn_g, w_out, proj_s, proj_s, proj_s, proj_s, state0)


def _moba_proj_kernel(*refs, stream):
    x_ref, g_ref, sh_ref, sc_ref, w_ref, qg_ref, kg_ref = refs[:7]
    if stream is None:
        q_ref, k_ref, v_ref, gt_ref, wb_scr = refs[7:]
        km_refs = None
    else:
        pt_ref, ck_ref, q_ref, k_ref, v_ref, gt_ref, km_ref, wb_scr, buf, sems = refs[7:]
        km_refs = (pt_ref, ck_ref, km_ref, buf, sems)

    @pl.when((pl.program_id(0) == 0) & (pl.program_id(1) == 0))
    def _():
        _cast_weight(w_ref, wb_scr, 512)

    h = _norm_mod(x_ref[...], g_ref[...], sh_ref[...], sc_ref[...]).astype(BF16)
    w = MOBA_W

    def q_part():
        q = jnp.dot(h, wb_scr[:, 0:w], preferred_element_type=F32)
        for hd in range(MOBA_HEADS):
            cs = slice(hd * MOBA_DH, (hd + 1) * MOBA_DH)
            q_ref[:, cs] = (_head_rms(q[:, cs], qg_ref[...]) * (MOBA_DH ** -0.5)).astype(q_ref.dtype)

    def k_part():
        k = jnp.dot(h, wb_scr[:, w:2 * w], preferred_element_type=F32)
        for hd in range(MOBA_HEADS):
            cs = slice(hd * MOBA_DH, (hd + 1) * MOBA_DH)
            k_ref[:, cs] = _head_rms(k[:, cs], kg_ref[...])

    def v_part():
        v_ref[...] = jnp.dot(h, wb_scr[:, 2 * w:3 * w], preferred_element_type=F32)

    def gate_part():
        gt_ref[...] = jnp.dot(h, wb_scr[:, 3 * w:4 * w], preferred_element_type=F32).astype(gt_ref.dtype)

    _run_with_km_stream(stream, km_refs, [q_part, k_part, v_part, gate_part])


def _moba_proj(x, norm_g, sh, sc, w, q_g, k_g, act_dtype, tm, km=None):
    g, r, d = x.shape
    n = w.shape[1]
    row_spec = pl.BlockSpec((None, tm, MOBA_W), lambda gi, i: (gi, i, 0))
    in_specs = [
        pl.BlockSpec((None, tm, d), lambda gi, i: (gi, i, 0)),
        pl.BlockSpec((1, d), lambda gi, i: (0, 0)),
        _mod_spec(sh, tm),
        _mod_spec(sc, tm),
        _weight_spec((d, n), lambda gi, i: (0, 0), True),
        pl.BlockSpec((1, MOBA_DH), lambda gi, i: (0, 0)),
        pl.BlockSpec((1, MOBA_DH), lambda gi, i: (0, 0)),
    ]
    out_specs = [row_spec, row_spec, row_spec, row_spec]
    out_shape = [
        jax.ShapeDtypeStruct((g, r, MOBA_W), act_dtype),
        jax.ShapeDtypeStruct((g, r, MOBA_W), F32),
        jax.ShapeDtypeStruct((g, r, MOBA_W), F32),
        jax.ShapeDtypeStruct((g, r, MOBA_W), act_dtype),
    ]
    scratch = [pltpu.VMEM((d, n), BF16)]
    args = [x, norm_g.reshape(1, d), sh, sc, w, q_g.reshape(1, MOBA_DH), k_g.reshape(1, MOBA_DH)]
    stream = None
    if km is not None:
        page_table, cache, stream = km
        assert stream.n_steps == g * (r // tm)
        in_specs += [pl.BlockSpec(memory_space=pltpu.SMEM), pl.BlockSpec(memory_space=pl.ANY)]
        km_spec, km_shape = _km_out(stream, cache, r // tm)
        out_specs.append(km_spec)
        out_shape.append(km_shape)
        scratch += _km_scratch(stream, cache)
        args += [page_table, cache]
    return pl.pallas_call(
        functools.partial(_moba_proj_kernel, stream=stream),
        grid=(g, r // tm),
        in_specs=in_specs,
        out_specs=out_specs,
        out_shape=out_shape,
        scratch_shapes=scratch,
        compiler_params=_params(("arbitrary", "arbitrary"), 56),
        name="moba_proj",
    )(*args)


def _alibi_slope(head):
    hv = jnp.full((1, 1), head, I32).astype(F32)
    return jnp.exp2(-8.0 * (hv + 1.0) / MOBA_HEADS)


AUG_PEN0 = 8


def _moba_prompt_head(q_ref, k_ref, v_ref, g_ref, o_ref, ka_scr, vt_scr, qa_scr, km_scr, *, n_blocks, head, side_work):
    blk = MOBA_BLOCK
    dh = MOBA_DH
    t_len = n_blocks * blk
    slope = _alibi_slope(head)
    pen_rows = km_scr.shape[0] // 4

    lane = lax.broadcasted_iota(I32, (blk, dh), 1)
    k_off = lax.broadcasted_iota(I32, (blk, dh), 0).astype(F32)
    km_scr[...] = jnp.zeros_like(km_scr)
    for n in range(n_blocks):
        rows = slice(n * blk, (n + 1) * blk)
        kn = k_ref[rows, :]
        ka_scr[rows, 0:dh] = kn.astype(BF16)
        aug = jnp.where(lane < 2, 1.0,
                        jnp.where(lane == 2, slope * float(blk * n),
                                  jnp.where(lane == 3, slope * k_off,
                                            jnp.where(lane == AUG_PEN0 + n, 1.0, 0.0))))
        ka_scr[rows, dh:2 * dh] = aug.astype(BF16)
        vt_scr[:, rows] = v_ref[rows, :].T.astype(BF16)
        mean = jnp.mean(kn, axis=0, keepdims=True)
        hi = mean.astype(BF16).astype(F32)
        mid = (mean - hi).astype(BF16).astype(F32)
        lo = (mean - hi - mid).astype(BF16).astype(F32)
        km_scr[n:n + 1, :] = hi
        km_scr[pen_rows + n:pen_rows + n + 1, :] = mid
        km_scr[2 * pen_rows + n:2 * pen_rows + n + 1, :] = lo
        qa_scr[0:dh, rows] = q_ref[rows, :].astype(F32).T.astype(BF16)

    parts = jnp.dot(km_scr[...].astype(BF16), qa_scr[0:dh, :], preferred_element_type=F32)
    gs = parts[0:pen_rows] + parts[pen_rows:2 * pen_rows] + parts[2 * pen_rows:3 * pen_rows]
    nid = lax.broadcasted_iota(I32, (pen_rows, t_len), 0)
    q_pos = lax.broadcasted_iota(I32, (pen_rows, t_len), 1)
    own = lax.shift_right_logical(q_pos, int(math.log2(blk)))
    past = nid < own
    attended = nid == own
    gs = jnp.where(past, gs, NEG)
    for _ in range(min(MOBA_TOPK, n_blocks)):
        mx = jnp.max(gs, axis=0, keepdims=True)
        idx = jnp.min(jnp.where(gs == mx, nid, pen_rows), axis=0, keepdims=True)
        pick = nid == idx
        attended = attended | (pick & past)
        gs = jnp.where(pick, BELOW_NEG, gs)
    pen = jnp.where(attended, 0.0, NEG)
    r8 = lax.broadcasted_iota(I32, (8, t_len), 0)
    q8 = lax.broadcasted_iota(I32, (8, t_len), 1)
    own8 = lax.shift_right_logical(q8, int(math.log2(blk)))
    bias = jnp.where(r8 == 0, -slope * (own8 * blk).astype(F32),
                     jnp.where(r8 == 1, -slope * (q8 - own8 * blk).astype(F32), jnp.where(r8 < 4, 1.0, 0.0)))
    extra = jnp.concatenate([bias, pen, jnp.zeros((dh - 8 - pen_rows, t_len), F32)], axis=0)
    qa_scr[dh:2 * dh, :] = extra.astype(BF16)

    def query_block(own):
        n_keys = (own + 1) * blk
        cols = slice(own * blk, n_keys)
        s = jnp.dot(ka_scr[0:n_keys, :], qa_scr[:, cols], preferred_element_type=F32)
        k_idx = lax.broadcasted_iota(I32, (blk, blk), 0)
        q_idx = lax.broadcasted_iota(I32, (blk, blk), 1)
        s_own = jnp.where(q_idx >= k_idx, s[own * blk:n_keys], NEG)
        m = jnp.max(s_own, axis=0, keepdims=True)
        if own > 0:
            s_past = s[0:own * blk]
            m = jnp.maximum(m, jnp.max(s_past, axis=0, keepdims=True))
        p = jnp.exp(s_own - m)
        l = jnp.sum(p, axis=0, keepdims=True)
        p = p.astype(BF16)
        if own > 0:
            p_past = jnp.exp(s_past - m)
            l = l + jnp.sum(p_past, axis=0, keepdims=True)
            p = jnp.concatenate([p_past.astype(BF16), p], axis=0)
        acc = jnp.dot(vt_scr[:, 0:n_keys], p, preferred_element_type=F32)
        o = (acc / l).T
        o_ref[cols, :] = (o * _silu(g_ref[cols, :].astype(F32))).astype(o_ref.dtype)

    for position, own in enumerate(reversed(range(n_blocks))):
        for work in side_work.get(position, ()):
            work()
        query_block(own)


class _SampleAttn(NamedTuple):
    units: int
    t: int
    past_len: int
    n_sel: int


def _sample_slab_copies(ck_ref, cv_ref, kbuf, vbuf, sems, page, head, slot, j):
    return (pltpu.make_async_copy(ck_ref.at[page, :, head, :], kbuf.at[slot, j], sems.at[0, slot]),
            pltpu.make_async_copy(cv_ref.at[page, :, head, :], vbuf.at[slot, j], sems.at[1, slot]))


def _sample_start(cfg, pt_ref, sel_ref, ck_ref, cv_ref, kbuf, vbuf, sems, step, slot):
    for u in range(cfg.units):
        unit = step * cfg.units + u
        sb = unit // MOBA_HEADS
        sh = unit % MOBA_HEADS
        for i in range(cfg.t):
            for s in range(MOBA_TOPK):
                blk_id = sel_ref[sb, (sh * cfg.t + i) * MOBA_TOPK + s]
                for p in range(PAGES_PER_BLOCK):
                    page = pt_ref[sb, blk_id * PAGES_PER_BLOCK + p]
                    j = u * cfg.n_sel + (i * MOBA_TOPK + s) * PAGES_PER_BLOCK + p
                    for c in _sample_slab_copies(ck_ref, cv_ref, kbuf, vbuf, sems, page, sh, slot, j):
                        c.start()


def _sample_wait(cfg, ck_ref, cv_ref, kbuf, vbuf, sems, slot):
    n = cfg.units * cfg.n_sel
    pltpu.make_async_copy(ck_ref.at[pl.ds(0, n), :, 0, :], kbuf.at[slot], sems.at[0, slot]).wait()
    pltpu.make_async_copy(cv_ref.at[pl.ds(0, n), :, 0, :], vbuf.at[slot], sems.at[1, slot]).wait()


def _sample_unit_stages(cfg, sel_ref, q_ref, kn_ref, vn_ref, g_ref, o_ref, kbuf, vbuf, slot, u, sb, sh):
    t = cfg.t
    cols = slice(u * MOBA_DH, (u + 1) * MOBA_DH)
    keys_per_query = cfg.n_sel // t * PAGE_SIZE
    n_keys = cfg.n_sel * PAGE_SIZE
    vals = {}

    def scores():
        slope = _alibi_slope(sh)
        q = q_ref[:, cols]
        kn = kn_ref[:, cols]
        k_all = kbuf[slot, u * cfg.n_sel:(u + 1) * cfg.n_sel].reshape(n_keys, MOBA_DH)
        off = lax.broadcasted_iota(I32, (1, PAGE_SIZE), 1)
        k_pos = jnp.concatenate(
            [sel_ref[sb, (sh * t + i) * MOBA_TOPK + s] * MOBA_BLOCK + p * PAGE_SIZE + off
             for i in range(t) for s in range(MOBA_TOPK) for p in range(PAGES_PER_BLOCK)], axis=1)
        row = lax.broadcasted_iota(I32, (t, n_keys), 0)
        col = lax.broadcasted_iota(I32, (t, n_keys), 1)
        owned = (col >= row * keys_per_query) & (col < (row + 1) * keys_per_query)
        dist = (cfg.past_len + row) - k_pos
        s_sel = lax.dot_general(q, k_all, NT_DIMS, preferred_element_type=F32) - slope * dist.astype(F32)
        vals["s_sel"] = jnp.where(owned & (dist >= 0), s_sel, NEG)
        s_new = jnp.concatenate([jnp.sum(q * kn[j:j + 1, :], axis=-1, keepdims=True) for j in range(t)], axis=1)
        d_new = lax.broadcasted_iota(I32, (t, t), 0) - lax.broadcasted_iota(I32, (t, t), 1)
        vals["s_new"] = jnp.where(d_new >= 0, s_new - slope * d_new.astype(F32), NEG)

    def softmax():
        s_sel, s_new = vals["s_sel"], vals["s_new"]
        m = jnp.maximum(jnp.max(s_sel, axis=1, keepdims=True), jnp.max(s_new, axis=1, keepdims=True))
        vals["p_sel"] = jnp.exp(s_sel - m)
        vals["p_new"] = jnp.exp(s_new - m)
        vals["l"] = jnp.sum(vals["p_sel"], axis=1, keepdims=True) + jnp.sum(vals["p_new"], axis=1, keepdims=True)

    def values():
        vn = vn_ref[:, cols]
        v_all = vbuf[slot, u * cfg.n_sel:(u + 1) * cfg.n_sel].reshape(n_keys, MOBA_DH)
        o = jnp.dot(vals["p_sel"], v_all, preferred_element_type=F32)
        for j in range(t):
            o = o + vals["p_new"][:, j:j + 1] * vn[j:j + 1, :]
        o_ref[:, cols] = (o / vals["l"]) * _silu(g_ref[:, cols])

    return scores, softmax, values


def _moba_attn_kernel(pt_ref, sel_ref, q_ref, k_ref, v_ref, g_ref, qs_ref, kn_ref, vn_ref, gs_ref, ck_ref, cv_ref,
                      o_ref, os_ref, ka_scr, vt_scr, qa_scr, km_scr, kbuf, vbuf, sems, *, n_blocks, cfg):
    head = pl.program_id(1)
    n_steps = pl.num_programs(0) * pl.num_programs(1)
    step = pl.program_id(0) * pl.num_programs(1) + head
    slot = step % 2

    @pl.when(step == 0)
    def _():
        _sample_start(cfg, pt_ref, sel_ref, ck_ref, cv_ref, kbuf, vbuf, sems, step, 0)

    _sample_wait(cfg, ck_ref, cv_ref, kbuf, vbuf, sems, slot)
    _sample_start(cfg, pt_ref, sel_ref, ck_ref, cv_ref, kbuf, vbuf, sems, jnp.where(step + 1 == n_steps, 0, step + 1),
                  1 - slot)

    unit0 = step * cfg.units
    stages = [_sample_unit_stages(cfg, sel_ref, qs_ref, kn_ref, vn_ref, gs_ref, os_ref, kbuf, vbuf, slot, u,
                                  unit0 // MOBA_HEADS, unit0 % MOBA_HEADS + u) for u in range(cfg.units)]
    gap = max(1, n_blocks // 4)
    first = n_blocks - 1 - 2 * gap
    side_work = {first + k * gap: [unit[k] for unit in stages] for k in range(3)}
    _moba_prompt_head(q_ref, k_ref, v_ref, g_ref, o_ref, ka_scr, vt_scr, qa_scr, km_scr, n_blocks=n_blocks, head=head,
                      side_work=side_work)

    @pl.when(step == n_steps - 1)
    def _():
        _sample_wait(cfg, ck_ref, cv_ref, kbuf, vbuf, sems, 1 - slot)


def _moba_attn(q, k, v, gate, q_s, k_s, v_s, g_s, cache_k_pages, cache_v_pages, page_table, sel, past_len):
    b, t, w = q.shape
    bs, ts, _ = q_s.shape
    assert t % MOBA_BLOCK == 0
    n_blocks = t // MOBA_BLOCK
    pen_rows = -(-n_blocks // 8) * 8
    assert AUG_PEN0 + pen_rows <= MOBA_DH
    assert MOBA_BLOCK & (MOBA_BLOCK - 1) == 0
    n_steps = b * MOBA_HEADS
    assert (bs * MOBA_HEADS) % n_steps == 0
    units = bs * MOBA_HEADS // n_steps
    assert MOBA_HEADS % units == 0 and n_blocks >= 3
    cfg = _SampleAttn(units=units, t=ts, past_len=past_len, n_sel=ts * MOBA_TOPK * PAGES_PER_BLOCK)
    steps_per_sb = MOBA_HEADS // units
    spec = pl.BlockSpec((None, t, MOBA_DH), lambda bi, h: (bi, 0, h))
    sspec = pl.BlockSpec((None, ts, units * MOBA_DH),
                         lambda bi, h: ((bi * MOBA_HEADS + h) // steps_per_sb, 0, (bi * MOBA_HEADS + h) % steps_per_sb))
    smem = pl.BlockSpec(memory_space=pltpu.SMEM)
    hbm = pl.BlockSpec(memory_space=pl.ANY)
    slab = (2, units * cfg.n_sel, PAGE_SIZE, MOBA_DH)
    return pl.pallas_call(
        functools.partial(_moba_attn_kernel, n_blocks=n_blocks, cfg=cfg),
        grid=(b, MOBA_HEADS),
        in_specs=[smem, smem, spec, spec, spec, spec, sspec, sspec, sspec, sspec, hbm, hbm],
        out_specs=[spec, sspec],
        out_shape=[jax.ShapeDtypeStruct((b, t, w), BF16), jax.ShapeDtypeStruct((bs, ts, w), F32)],
        scratch_shapes=[
            pltpu.VMEM((t, 2 * MOBA_DH), BF16),
            pltpu.VMEM((MOBA_DH, t), BF16),
            pltpu.VMEM((2 * MOBA_DH, t), BF16),
            pltpu.VMEM((4 * pen_rows, MOBA_DH), F32),
            pltpu.VMEM(slab, F32),
            pltpu.VMEM(slab, F32),
            pltpu.SemaphoreType.DMA((2, 2)),
        ],
        compiler_params=_params(("arbitrary", "arbitrary"), 56),
        name="moba_attn",
    )(page_table, sel, q, k, v, gate, q_s, k_s, v_s, g_s, cache_k_pages, cache_v_pages)


def _moba_select_kernel(q_ref, km_ref, sel_ref, *, t, topk):
    n_blocks = km_ref.shape[0]
    rows = sel_ref.shape[0]
    rid = lax.broadcasted_iota(I32, (rows, 128), 0)
    lid = lax.broadcasted_iota(I32, (rows, 128), 1)
    nid = lax.broadcasted_iota(I32, (n_blocks, 1), 0)
    out = jnp.zeros((rows, 128), I32)
    for h in range(MOBA_HEADS):
        cs = slice(h * MOBA_DH, (h + 1) * MOBA_DH)
        km = km_ref[:, h, :]
        for i in range(t):
            gs = jnp.sum(km * q_ref[i:i + 1, cs], axis=-1, keepdims=True)
            for r in range(topk):
                mx = jnp.max(gs, axis=0, keepdims=True)
                idx = jnp.min(jnp.where(gs == mx, nid, n_blocks), axis=0, keepdims=True)
                out = jnp.where((rid == h * t + i) & (lid == r), idx, out)
                gs = jnp.where(nid == idx, BELOW_NEG, gs)
    sel_ref[...] = out


def _moba_select(q, kmean):
    b, t, w = q.shape
    n_blocks = kmean.shape[1]
    rows = MOBA_HEADS * t
    return pl.pallas_call(
        functools.partial(_moba_select_kernel, t=t, topk=MOBA_TOPK),
        grid=(b,),
        in_specs=[
            pl.BlockSpec((None, t, w), lambda bi: (bi, 0, 0)),
            pl.BlockSpec((None, n_blocks, MOBA_HEADS, MOBA_DH), lambda bi: (bi, 0, 0, 0)),
        ],
        out_specs=pl.BlockSpec((None, rows, 128), lambda bi: (bi, 0, 0)),
        out_shape=jax.ShapeDtypeStruct((b, rows, 128), I32),
        compiler_params=_params(("arbitrary",), 32),
        name="moba_select",
    )(q, kmean)


def kernel(x_prompt, x_sample, c_prompt, c_sample, state_ret, cache_k, cache_v, page_table, norm_g, w_ada, b_ada,
           w_ret_in, ret_gn_g, w_ret_out, w_moba_in, moba_q_g, moba_k_g, w_moba_out):
    bp, tp, d = x_prompt.shape
    bs, ts, _ = x_sample.shape
    n_pages = page_table.shape[1]
    past_len = n_pages * PAGE_SIZE
    assert past_len % MOBA_BLOCK == 0 and ts <= MOBA_BLOCK and past_len // MOBA_BLOCK >= MOBA_TOPK
    assert w_ada.shape[0] == 2 and w_ret_in.shape[0] == 1 and w_moba_in.shape[0] == 1

    n_c = bp + bs
    c_all = jnp.concatenate([c_prompt, c_sample, jnp.zeros((-n_c % 8, d), F32)], axis=0)
    mod = _ada(c_all, w_ada, b_ada)

    def group_mod(layer):
        mp = mod[layer, :bp].reshape(bp, 1, 3 * d)
        ms = jnp.repeat(mod[layer, bp:n_c], ts, axis=0).reshape(1, bs * ts, 3 * d)
        return [(m[..., :d], m[..., d:2 * d], m[..., 2 * d:]) for m in (mp, ms)]

    xs = x_sample.reshape(1, bs * ts, d)
    ck = cache_k[0]
    cv = cache_v[0]

    tm_proj = 256
    n_steps = bp * (tp // tm_proj)
    n_blocks_past = bs * n_pages // PAGES_PER_BLOCK
    assert n_pages % PAGES_PER_BLOCK == 0 and n_blocks_past % n_steps == 0 and n_blocks_past // n_steps >= 2
    bps = n_blocks_past // n_steps
    bps_ret = min(bps - 1, -(-bps * 5 // 8))
    stream_ret = _KmStream(0, bps_ret * PAGES_PER_BLOCK, n_steps, n_pages)
    stream_moba = _KmStream(bps_ret * PAGES_PER_BLOCK * n_steps, (bps - bps_ret) * PAGES_PER_BLOCK, n_steps, n_pages)
    w_ret_in_b = w_ret_in[0].astype(BF16)

    (sh_p, sc_p, gt_p), (sh_s, sc_s, gt_s) = group_mod(0)
    proj_p, km_a = _ret_proj(x_prompt, norm_g[0], sh_p, sc_p, w_ret_in_b, BF16, tm=tm_proj,
                             km=(page_table, ck, stream_ret))
    proj_s = _ret_proj(xs, norm_g[0], sh_s, sc_s, w_ret_in_b, F32, tm=bs * ts).reshape(bs, ts, -1)
    y_p, ret_p, og_s, ret_s = _ret_mixer(proj_p, x_prompt, gt_p, ret_gn_g[0], w_ret_out[0], proj_s, state_ret[0],
                                         rows_per_step=512)
    y_s = _out_proj(og_s.reshape(1, bs * ts, RET_V).astype(BF16), w_ret_out[0], xs, gt_s, tm=bs * ts)

    (sh_p, sc_p, gt_p), (sh_s, sc_s, gt_s) = group_mod(1)
    q_p, k_p, v_p, g_p, km_b = _moba_proj(y_p, norm_g[1], sh_p, sc_p, w_moba_in[0], moba_q_g[0], moba_k_g[0], BF16,
                                          tm=tm_proj, km=(page_table, ck, stream_moba))
    q_s, k_s, v_s, g_s = _moba_proj(y_s, norm_g[1], sh_s, sc_s, w_moba_in[0], moba_q_g[0], moba_k_g[0], F32,
                                    tm=bs * ts)
    q_s, k_s, v_s, g_s = (a.reshape(bs, ts, MOBA_W) for a in (q_s, k_s, v_s, g_s))
    kmean = jnp.concatenate([km_a, km_b], axis=0).reshape(bs, past_len // MOBA_BLOCK, MOBA_HEADS, MOBA_DH)
    sel = _moba_select(q_s, kmean)[:, :, :MOBA_TOPK].reshape(bs, MOBA_HEADS * ts * MOBA_TOPK)
    oa_p, oa_s = _moba_attn(q_p, k_p, v_p, g_p, q_s, k_s, v_s, g_s, ck, cv, page_table, sel, past_len)
    y_p = _out_proj(oa_p, w_moba_out[0], y_p, gt_p, tm=1024)
    y_s = _out_proj(oa_s.reshape(1, bs * ts, MOBA_W).astype(BF16), w_moba_out[0], y_s, gt_s, tm=bs * ts)

    hd = (MOBA_HEADS, MOBA_DH)
    return (y_p, y_s.reshape(bs, ts, d), ret_p[None], ret_s[None],
            k_p.reshape(1, bp, tp, *hd), v_p.reshape(1, bp, tp, *hd),
            k_s.reshape(1, bs, ts, *hd), v_s.reshape(1, bs, ts, *hd))
```

```python
import functools
import math
from typing import NamedTuple

import jax
import jax.numpy as jnp
from jax import lax
from jax.experimental import pallas as pl
from jax.experimental.pallas import tpu as pltpu

F32 = jnp.float32
BF16 = jnp.bfloat16
I32 = jnp.int32

RET_HEADS = 4
RET_DK = 256
RET_DV = 512
RET_CHUNK = 256
RET_QK = RET_HEADS * RET_DK
RET_V = RET_HEADS * RET_DV
MOBA_HEADS = 8
MOBA_DH = 128
MOBA_W = MOBA_HEADS * MOBA_DH
MOBA_BLOCK = 256
MOBA_TOPK = 3
PAGE_SIZE = 128
EPS = 1e-6
NEG = -1e30
BELOW_NEG = -3e38

V7X_VMEM_BYTES = 64 * 1024 * 1024
NT_DIMS = (((1,), (1,)), ((), ()))


def _params(semantics, vmem_mb):
    assert vmem_mb * 1024 * 1024 < V7X_VMEM_BYTES
    return pltpu.CompilerParams(dimension_semantics=semantics, vmem_limit_bytes=vmem_mb * 1024 * 1024)


def _silu(x):
    return x * jax.nn.sigmoid(x)


def _norm_mod(x, g, sh, sc):
    ms = jnp.mean(x * x, axis=-1, keepdims=True)
    y = x * lax.rsqrt(ms + EPS) * g
    return y * (1.0 + sc) + sh


def _head_rms(x, g):
    ms = jnp.mean(x * x, axis=-1, keepdims=True)
    return x * lax.rsqrt(ms + EPS) * g


def _ret_log_gamma(h):
    return math.log1p(-(2.0 ** (-5.0 - h)))


def _ada_kernel(c_ref, w_ref, b_ref, o_ref):
    s = _silu(c_ref[...]).astype(BF16)
    o_ref[...] = jnp.dot(s, w_ref[...].astype(BF16), preferred_element_type=F32) + b_ref[...]


def _ada(c_all, w_ada, b_ada):
    depth, d, n = w_ada.shape
    m = c_all.shape[0]
    tn = 512
    return pl.pallas_call(
        _ada_kernel,
        grid=(depth, n // tn),
        in_specs=[
            pl.BlockSpec((m, d), lambda l, j: (0, 0)),
            pl.BlockSpec((None, d, tn), lambda l, j: (l, 0, j)),
            pl.BlockSpec((None, 1, tn), lambda l, j: (l, 0, j)),
        ],
        out_specs=pl.BlockSpec((None, m, tn), lambda l, j: (l, 0, j)),
        out_shape=jax.ShapeDtypeStruct((depth, m, n), F32),
        compiler_params=_params(("arbitrary", "arbitrary"), 32),
        name="ada",
    )(c_all, w_ada, b_ada.reshape(depth, 1, n))


def _mod_spec(mod, tm):
    d = mod.shape[-1]
    if mod.shape[1] == 1:
        return pl.BlockSpec((None, 1, d), lambda g, i, *_: (g, 0, 0))
    return pl.BlockSpec((None, tm, d), lambda g, i, *_: (g, i, 0))


def _weight_spec(shape, index_map, resident):
    if resident:
        return pl.BlockSpec(shape, index_map, pipeline_mode=pl.Buffered(1))
    return pl.BlockSpec(shape, index_map)


def _cast_weight(w_ref, wb_scr, col_chunk):
    for c in range(w_ref.shape[1] // col_chunk):
        cs = slice(c * col_chunk, (c + 1) * col_chunk)
        wb_scr[:, cs] = w_ref[:, cs].astype(BF16)


PAGES_PER_BLOCK = MOBA_BLOCK // PAGE_SIZE


class _KmStream(NamedTuple):
    page0: int
    pages_per_step: int
    n_steps: int
    n_pages: int


def _km_scratch(stream, cache):
    return [pltpu.VMEM((2, stream.pages_per_step, *cache.shape[1:]), F32), pltpu.SemaphoreType.DMA((2,))]


def _km_out(stream, cache, steps_per_group):
    hd = cache.shape[2:]
    blocks_per_step = stream.pages_per_step // PAGES_PER_BLOCK
    spec = pl.BlockSpec((blocks_per_step, *hd), lambda gi, i: (gi * steps_per_group + i, 0, 0))
    return spec, jax.ShapeDtypeStruct((stream.n_steps * blocks_per_step, *hd), F32)


def _km_copy(ck_ref, buf, sems, page, slot, p):
    return pltpu.make_async_copy(ck_ref.at[page], buf.at[slot, p], sems.at[slot])


def _km_start(stream, pt_ref, ck_ref, buf, sems, step, slot):
    for p in range(stream.pages_per_step):
        flat = stream.page0 + step * stream.pages_per_step + p
        page = pt_ref[flat // stream.n_pages, flat % stream.n_pages]
        _km_copy(ck_ref, buf, sems, page, slot, p).start()


def _km_wait(stream, ck_ref, buf, sems, slot):
    pltpu.make_async_copy(ck_ref.at[pl.ds(0, stream.pages_per_step)], buf.at[slot], sems.at[slot]).wait()


def _run_with_km_stream(stream, km_refs, chunks):
    if stream is not None:
        pt_ref, ck_ref, km_ref, buf, sems = km_refs
        step = pl.program_id(0) * pl.num_programs(1) + pl.program_id(1)
        slot = step % 2

        @pl.when(step == 0)
        def _():
            _km_start(stream, pt_ref, ck_ref, buf, sems, step, 0)

        _km_wait(stream, ck_ref, buf, sems, slot)
        _km_start(stream, pt_ref, ck_ref, buf, sems, jnp.where(step + 1 == stream.n_steps, 0, step + 1), 1 - slot)
        for blk in range(stream.pages_per_step // PAGES_PER_BLOCK):
            tot = jnp.sum(buf[slot, blk * PAGES_PER_BLOCK], axis=0)
            for p in range(1, PAGES_PER_BLOCK):
                tot = tot + jnp.sum(buf[slot, blk * PAGES_PER_BLOCK + p], axis=0)
            km_ref[blk] = tot / float(MOBA_BLOCK)

    for chunk in chunks:
        chunk()

    if stream is not None:
        @pl.when(step == stream.n_steps - 1)
        def _():
            _km_wait(stream, ck_ref, buf, sems, 1 - slot)


def _ret_proj_kernel(*refs, col_chunk, stream):
    x_ref, g_ref, sh_ref, sc_ref, w_ref = refs[:5]
    if stream is None:
        (o_ref,) = refs[5:]
        km_refs = None
    else:
        pt_ref, ck_ref, o_ref, km_ref, buf, sems = refs[5:]
        km_refs = (pt_ref, ck_ref, km_ref, buf, sems)

    h = _norm_mod(x_ref[...], g_ref[...], sh_ref[...], sc_ref[...]).astype(BF16)

    def column_chunk(c):
        cs = slice(c * col_chunk, (c + 1) * col_chunk)
        o_ref[:, cs] = jnp.dot(h, w_ref[:, cs], preferred_element_type=F32).astype(o_ref.dtype)

    _run_with_km_stream(stream, km_refs,
                        [functools.partial(column_chunk, c) for c in range(w_ref.shape[1] // col_chunk)])


def _ret_proj(x, norm_g, sh, sc, w, out_dtype, tm, km=None):
    g, r, d = x.shape
    n = w.shape[1]
    assert w.dtype == BF16
    in_specs = [
        pl.BlockSpec((None, tm, d), lambda gi, i: (gi, i, 0)),
        pl.BlockSpec((1, d), lambda gi, i: (0, 0)),
        _mod_spec(sh, tm),
        _mod_spec(sc, tm),
        _weight_spec((d, n), lambda gi, i: (0, 0), True),
    ]
    out_specs = [pl.BlockSpec((None, tm, n), lambda gi, i: (gi, i, 0))]
    out_shape = [jax.ShapeDtypeStruct((g, r, n), out_dtype)]
    scratch = []
    args = [x, norm_g.reshape(1, d), sh, sc, w]
    stream = None
    if km is not None:
        page_table, cache, stream = km
        assert stream.n_steps == g * (r // tm)
        in_specs += [pl.BlockSpec(memory_space=pltpu.SMEM), pl.BlockSpec(memory_space=pl.ANY)]
        km_spec, km_shape = _km_out(stream, cache, r // tm)
        out_specs.append(km_spec)
        out_shape.append(km_shape)
        scratch += _km_scratch(stream, cache)
        args += [page_table, cache]
    outs = pl.pallas_call(
        functools.partial(_ret_proj_kernel, col_chunk=512, stream=stream),
        grid=(g, r // tm),
        in_specs=in_specs,
        out_specs=out_specs,
        out_shape=out_shape,
        scratch_shapes=scratch,
        compiler_params=_params(("arbitrary", "arbitrary"), 58),
        name="ret_proj",
    )(*args)
    return outs if km is not None else outs[0]


def _out_proj_kernel(a_ref, w_ref, x_ref, gt_ref, o_ref, wb_scr):
    @pl.when((pl.program_id(0) == 0) & (pl.program_id(1) == 0))
    def _():
        _cast_weight(w_ref, wb_scr, 512)

    o_ref[...] = x_ref[...] + gt_ref[...] * jnp.dot(a_ref[...], wb_scr[...], preferred_element_type=F32)


def _out_proj(a, w, x, gate, tm):
    g, r, k = a.shape
    d = w.shape[1]
    return pl.pallas_call(
        _out_proj_kernel,
        grid=(g, r // tm),
        in_specs=[
            pl.BlockSpec((None, tm, k), lambda gi, i: (gi, i, 0)),
            _weight_spec((k, d), lambda gi, i: (0, 0), True),
            pl.BlockSpec((None, tm, d), lambda gi, i: (gi, i, 0)),
            _mod_spec(gate, tm),
        ],
        out_specs=pl.BlockSpec((None, tm, d), lambda gi, i: (gi, i, 0)),
        out_shape=jax.ShapeDtypeStruct((g, r, d), F32),
        scratch_shapes=[pltpu.VMEM((k, d), BF16)],
        compiler_params=_params(("arbitrary", "arbitrary"), 48),
        name="out_proj",
    )(a, w, x, gate)


def _pick(index, values):
    out = values[0]
    for k in range(1, len(values)):
        out = jnp.where(index == k, values[k], out)
    return out


def _ret_sample_heads(q_ref, k_ref, v_ref, g_ref, s0_ref, gn_ref, og_ref, s_ref, *, t, hps, part):
    ii = lax.broadcasted_iota(I32, (t, t), 0)
    jj = lax.broadcasted_iota(I32, (t, t), 1)
    diff = (ii - jj).astype(F32)
    pos = lax.broadcasted_iota(I32, (t, 1), 0).astype(F32)
    pad = -t % 8
    for j in range(hps):
        heads = [p * hps + j for p in range(RET_HEADS // hps)]
        lg = _pick(part, [_ret_log_gamma(h) for h in heads])
        chunk_decay = _pick(part, [math.exp(_ret_log_gamma(h) * t) for h in heads])
        decay = jnp.where(diff >= 0, jnp.exp(lg * jnp.maximum(diff, 0.0)), 0.0)
        q_decay = jnp.exp(lg * (pos + 1.0))
        k_decay = jnp.exp(lg * (t - 1.0 - pos))
        q = q_ref[:, j * RET_DK:(j + 1) * RET_DK]
        k = k_ref[:, j * RET_DK:(j + 1) * RET_DK] * (RET_DK ** -0.5)
        v = v_ref[:, j * RET_DV:(j + 1) * RET_DV]
        gate = g_ref[:, j * RET_DV:(j + 1) * RET_DV]
        state = s0_ref[j]
        o = jnp.dot(q * q_decay, state, preferred_element_type=F32)
        for i in range(t):
            s_i = jnp.sum(q * k[i:i + 1, :], axis=-1, keepdims=True) * decay[:, i:i + 1]
            o = o + s_i * v[i:i + 1, :]
        kd = jnp.concatenate([k * k_decay, jnp.zeros((pad, RET_DK), F32)], axis=0)
        vp = jnp.concatenate([v, jnp.zeros((pad, RET_DV), F32)], axis=0)
        kv = lax.dot_general(kd, vp, (((0,), (0,)), ((), ())), preferred_element_type=F32)
        s_ref[j] = chunk_decay * state + kv
        oc = o - jnp.mean(o, axis=-1, keepdims=True)
        y = oc * lax.rsqrt(jnp.mean(oc * oc, axis=-1, keepdims=True) + EPS) * gn_ref[pl.ds(part * hps + j, 1), :]
        og_ref[:, j * RET_DV:(j + 1) * RET_DV] = (y * _silu(gate)).astype(og_ref.dtype)


def _ret_mixer_kernel(p_ref, x_ref, gt_ref, gn_ref, w_ref, qs_ref, ks_ref, vs_ref, gs_ref, s0_ref,
                      y_ref, s_ref, ogs_ref, ss_ref, wb_scr, og_ref, *, chunk, n_chunks, t_s, hps):
    step = pl.program_id(0) * pl.num_programs(1) + pl.program_id(1)

    @pl.when(step == 0)
    def _():
        _cast_weight(w_ref, wb_scr, 512)

    @pl.when(pl.program_id(1) == 0)
    def _():
        s_ref[...] = jnp.zeros_like(s_ref)

    _ret_sample_heads(qs_ref, ks_ref, vs_ref, gs_ref, s0_ref, gn_ref, ogs_ref, ss_ref, t=t_s, hps=hps,
                      part=step % (RET_HEADS // hps))

    ii = lax.broadcasted_iota(I32, (chunk, chunk), 0)
    jj = lax.broadcasted_iota(I32, (chunk, chunk), 1)
    diff = (ii - jj).astype(F32)
    pos = lax.broadcasted_iota(I32, (chunk, 1), 0).astype(F32)
    for h in range(RET_HEADS):
        lg = _ret_log_gamma(h)
        decay = jnp.where(diff >= 0, jnp.exp(lg * jnp.maximum(diff, 0.0)), 0.0)
        q_decay = jnp.exp(lg * (pos + 1.0))
        k_decay = jnp.exp(lg * (chunk - 1.0 - pos))
        chunk_decay = math.exp(lg * chunk)
        for c in range(n_chunks):
            rows = slice(c * chunk, (c + 1) * chunk)
            q = p_ref[rows, h * RET_DK:(h + 1) * RET_DK]
            k = p_ref[rows, RET_QK + h * RET_DK:RET_QK + (h + 1) * RET_DK].astype(F32) * (RET_DK ** -0.5)
            v = p_ref[rows, 2 * RET_QK + h * RET_DV:2 * RET_QK + (h + 1) * RET_DV]
            gate = p_ref[rows, 2 * RET_QK + RET_V + h * RET_DV:2 * RET_QK + RET_V + (h + 1) * RET_DV].astype(F32)
            state = s_ref[h]
            scores = lax.dot_general(q, k.astype(BF16), NT_DIMS, preferred_element_type=F32) * decay
            inner = jnp.dot(scores.astype(BF16), v, preferred_element_type=F32)
            cross = jnp.dot((q.astype(F32) * q_decay).astype(BF16), state.astype(BF16), preferred_element_type=F32)
            kv = lax.dot_general((k * k_decay).astype(BF16), v, (((0,), (0,)), ((), ())), preferred_element_type=F32)
            s_ref[h] = chunk_decay * state + kv
            o = inner + cross
            oc = o - jnp.mean(o, axis=-1, keepdims=True)
            y = oc * lax.rsqrt(jnp.mean(oc * oc, axis=-1, keepdims=True) + EPS) * gn_ref[h:h + 1, :]
            og_ref[rows, h * RET_DV:(h + 1) * RET_DV] = (y * _silu(gate)).astype(og_ref.dtype)

    y_ref[...] = x_ref[...] + gt_ref[...] * jnp.dot(og_ref[...], wb_scr[...], preferred_element_type=F32)


def _ret_mixer(proj, x, gate, gn_g, w_out, proj_s, state0, rows_per_step):
    b, t, n = proj.shape
    d = x.shape[-1]
    bs, ts, _ = proj_s.shape
    assert t % RET_CHUNK == 0 and rows_per_step % RET_CHUNK == 0
    n_t = t // rows_per_step
    n_steps = b * n_t
    assert (bs * RET_HEADS) % n_steps == 0
    hps = bs * RET_HEADS // n_steps
    assert RET_HEADS % hps == 0
    per_sb = RET_HEADS // hps

    def sample_spec(width, first_col):
        first = first_col // (hps * width)
        return pl.BlockSpec((None, ts, hps * width),
                            lambda bi, i: ((bi * n_t + i) // per_sb, 0, first + (bi * n_t + i) % per_sb))

    state_s_spec = pl.BlockSpec((None, hps, RET_DK, RET_DV),
                                lambda bi, i: ((bi * n_t + i) // per_sb, (bi * n_t + i) % per_sb, 0, 0))
    return pl.pallas_call(
        functools.partial(_ret_mixer_kernel, chunk=RET_CHUNK, n_chunks=rows_per_step // RET_CHUNK, t_s=ts, hps=hps),
        grid=(b, n_t),
        in_specs=[
            pl.BlockSpec((None, rows_per_step, n), lambda bi, i: (bi, i, 0)),
            pl.BlockSpec((None, rows_per_step, d), lambda bi, i: (bi, i, 0)),
            _mod_spec(gate, rows_per_step),
            pl.BlockSpec((RET_HEADS, RET_DV), lambda bi, i: (0, 0)),
            _weight_spec((RET_V, d), lambda bi, i: (0, 0), True),
            sample_spec(RET_DK, 0),
            sample_spec(RET_DK, RET_QK),
            sample_spec(RET_DV, 2 * RET_QK),
            sample_spec(RET_DV, 2 * RET_QK + RET_V),
            state_s_spec,
        ],
        out_specs=[
            pl.BlockSpec((None, rows_per_step, d), lambda bi, i: (bi, i, 0)),
            pl.BlockSpec((None, RET_HEADS, RET_DK, RET_DV), lambda bi, i: (bi, 0, 0, 0)),
            sample_spec(RET_DV, 0),
            state_s_spec,
        ],
        out_shape=[
            jax.ShapeDtypeStruct((b, t, d), F32),
            jax.ShapeDtypeStruct((b, RET_HEADS, RET_DK, RET_DV), F32),
            jax.ShapeDtypeStruct((bs, ts, RET_V), F32),
            jax.ShapeDtypeStruct((bs, RET_HEADS, RET_DK, RET_DV), F32),
        ],
        scratch_shapes=[pltpu.VMEM((RET_V, d), BF16), pltpu.VMEM((rows_per_step, RET_V), BF16)],
        compiler_params=_params(("arbitrary", "arbitrary"), 56),
        name="ret_mixer",
    )(proj, x, gate, gn_g, w_out, proj_s, proj_s, proj_s, proj_s, state0)


def _moba_proj_kernel(*refs, stream):
    x_ref, g_ref, sh_ref, sc_ref, w_ref, qg_ref, kg_ref = refs[:7]
    if stream is None:
        q_ref, k_ref, v_ref, gt_ref, wb_scr = refs[7:]
        km_refs = None
    else:
        pt_ref, ck_ref, q_ref, k_ref, v_ref, gt_ref, km_ref, wb_scr, buf, sems = refs[7:]
        km_refs = (pt_ref, ck_ref, km_ref, buf, sems)

    @pl.when((pl.program_id(0) == 0) & (pl.program_id(1) == 0))
    def _():
        _cast_weight(w_ref, wb_scr, 512)

    h = _norm_mod(x_ref[...], g_ref[...], sh_ref[...], sc_ref[...]).astype(BF16)
    w = MOBA_W

    def q_part():
        q = jnp.dot(h, wb_scr[:, 0:w], preferred_element_type=F32)
        for hd in range(MOBA_HEADS):
            cs = slice(hd * MOBA_DH, (hd + 1) * MOBA_DH)
            q_ref[:, cs] = (_head_rms(q[:, cs], qg_ref[...]) * (MOBA_DH ** -0.5)).astype(q_ref.dtype)

    def k_part():
        k = jnp.dot(h, wb_scr[:, w:2 * w], preferred_element_type=F32)
        for hd in range(MOBA_HEADS):
            cs = slice(hd * MOBA_DH, (hd + 1) * MOBA_DH)
            k_ref[:, cs] = _head_rms(k[:, cs], kg_ref[...])

    def v_part():
        v_ref[...] = jnp.dot(h, wb_scr[:, 2 * w:3 * w], preferred_element_type=F32)

    def gate_part():
        gt_ref[...] = jnp.dot(h, wb_scr[:, 3 * w:4 * w], preferred_element_type=F32).astype(gt_ref.dtype)

    _run_with_km_stream(stream, km_refs, [q_part, k_part, v_part, gate_part])


def _moba_proj(x, norm_g, sh, sc, w, q_g, k_g, act_dtype, tm, km=None):
    g, r, d = x.shape
    n = w.shape[1]
    row_spec = pl.BlockSpec((None, tm, MOBA_W), lambda gi, i: (gi, i, 0))
    in_specs = [
        pl.BlockSpec((None, tm, d), lambda gi, i: (gi, i, 0)),
        pl.BlockSpec((1, d), lambda gi, i: (0, 0)),
        _mod_spec(sh, tm),
        _mod_spec(sc, tm),
        _weight_spec((d, n), lambda gi, i: (0, 0), True),
        pl.BlockSpec((1, MOBA_DH), lambda gi, i: (0, 0)),
        pl.BlockSpec((1, MOBA_DH), lambda gi, i: (0, 0)),
    ]
    out_specs = [row_spec, row_spec, row_spec, row_spec]
    out_shape = [
        jax.ShapeDtypeStruct((g, r, MOBA_W), act_dtype),
        jax.ShapeDtypeStruct((g, r, MOBA_W), F32),
        jax.ShapeDtypeStruct((g, r, MOBA_W), F32),
        jax.ShapeDtypeStruct((g, r, MOBA_W), act_dtype),
    ]
    scratch = [pltpu.VMEM((d, n), BF16)]
    args = [x, norm_g.reshape(1, d), sh, sc, w, q_g.reshape(1, MOBA_DH), k_g.reshape(1, MOBA_DH)]
    stream = None
    if km is not None:
        page_table, cache, stream = km
        assert stream.n_steps == g * (r // tm)
        in_specs += [pl.BlockSpec(memory_space=pltpu.SMEM), pl.BlockSpec(memory_space=pl.ANY)]
        km_spec, km_shape = _km_out(stream, cache, r // tm)
        out_specs.append(km_spec)
        out_shape.append(km_shape)
        scratch += _km_scratch(stream, cache)
        args += [page_table, cache]
    return pl.pallas_call(
        functools.partial(_moba_proj_kernel, stream=stream),
        grid=(g, r // tm),
        in_specs=in_specs,
        out_specs=out_specs,
        out_shape=out_shape,
        scratch_shapes=scratch,
        compiler_params=_params(("arbitrary", "arbitrary"), 56),
        name="moba_proj",
    )(*args)


def _alibi_slope(head):
    hv = jnp.full((1, 1), head, I32).astype(F32)
    return jnp.exp2(-8.0 * (hv + 1.0) / MOBA_HEADS)


AUG_PEN0 = 8


def _moba_prompt_head(q_ref, k_ref, v_ref, g_ref, o_ref, ka_scr, vt_scr, qa_scr, km_scr, *, n_blocks, head, side_work):
    blk = MOBA_BLOCK
    dh = MOBA_DH
    t_len = n_blocks * blk
    slope = _alibi_slope(head)
    pen_rows = km_scr.shape[0] // 4

    lane = lax.broadcasted_iota(I32, (blk, dh), 1)
    k_off = lax.broadcasted_iota(I32, (blk, dh), 0).astype(F32)
    km_scr[...] = jnp.zeros_like(km_scr)
    for n in range(n_blocks):
        rows = slice(n * blk, (n + 1) * blk)
        kn = k_ref[rows, :]
        ka_scr[rows, 0:dh] = kn.astype(BF16)
        aug = jnp.where(lane < 2, 1.0,
                        jnp.where(lane == 2, slope * float(blk * n),
                                  jnp.where(lane == 3, slope * k_off,
                                            jnp.where(lane == AUG_PEN0 + n, 1.0, 0.0))))
        ka_scr[rows, dh:2 * dh] = aug.astype(BF16)
        vt_scr[:, rows] = v_ref[rows, :].T.astype(BF16)
        mean = jnp.mean(kn, axis=0, keepdims=True)
        hi = mean.astype(BF16).astype(F32)
        mid = (mean - hi).astype(BF16).astype(F32)
        lo = (mean - hi - mid).astype(BF16).astype(F32)
        km_scr[n:n + 1, :] = hi
        km_scr[pen_rows + n:pen_rows + n + 1, :] = mid
        km_scr[2 * pen_rows + n:2 * pen_rows + n + 1, :] = lo
        qa_scr[0:dh, rows] = q_ref[rows, :].astype(F32).T.astype(BF16)

    parts = jnp.dot(km_scr[...].astype(BF16), qa_scr[0:dh, :], preferred_element_type=F32)
    gs = parts[0:pen_rows] + parts[pen_rows:2 * pen_rows] + parts[2 * pen_rows:3 * pen_rows]
    nid = lax.broadcasted_iota(I32, (pen_rows, t_len), 0)
    q_pos = lax.broadcasted_iota(I32, (pen_rows, t_len), 1)
    own = lax.shift_right_logical(q_pos, int(math.log2(blk)))
    past = nid < own
    attended = nid == own
    gs = jnp.where(past, gs, NEG)
    for _ in range(min(MOBA_TOPK, n_blocks)):
        mx = jnp.max(gs, axis=0, keepdims=True)
        idx = jnp.min(jnp.where(gs == mx, nid, pen_rows), axis=0, keepdims=True)
        pick = nid == idx
        attended = attended | (pick & past)
        gs = jnp.where(pick, BELOW_NEG, gs)
    pen = jnp.where(attended, 0.0, NEG)
    r8 = lax.broadcasted_iota(I32, (8, t_len), 0)
    q8 = lax.broadcasted_iota(I32, (8, t_len), 1)
    own8 = lax.shift_right_logical(q8, int(math.log2(blk)))
    bias = jnp.where(r8 == 0, -slope * (own8 * blk).astype(F32),
                     jnp.where(r8 == 1, -slope * (q8 - own8 * blk).astype(F32), jnp.where(r8 < 4, 1.0, 0.0)))
    extra = jnp.concatenate([bias, pen, jnp.zeros((dh - 8 - pen_rows, t_len), F32)], axis=0)
    qa_scr[dh:2 * dh, :] = extra.astype(BF16)

    def query_block(own):
        n_keys = (own + 1) * blk
        cols = slice(own * blk, n_keys)
        s = jnp.dot(ka_scr[0:n_keys, :], qa_scr[:, cols], preferred_element_type=F32)
        k_idx = lax.broadcasted_iota(I32, (blk, blk), 0)
        q_idx = lax.broadcasted_iota(I32, (blk, blk), 1)
        s_own = jnp.where(q_idx >= k_idx, s[own * blk:n_keys], NEG)
        m = jnp.max(s_own, axis=0, keepdims=True)
        if own > 0:
            s_past = s[0:own * blk]
            m = jnp.maximum(m, jnp.max(s_past, axis=0, keepdims=True))
        p = jnp.exp(s_own - m)
        l = jnp.sum(p, axis=0, keepdims=True)
        p = p.astype(BF16)
        if own > 0:
            p_past = jnp.exp(s_past - m)
            l = l + jnp.sum(p_past, axis=0, keepdims=True)
            p = jnp.concatenate([p_past.astype(BF16), p], axis=0)
        acc = jnp.dot(vt_scr[:, 0:n_keys], p, preferred_element_type=F32)
        o = (acc / l).T
        o_ref[cols, :] = (o * _silu(g_ref[cols, :].astype(F32))).astype(o_ref.dtype)

    for position, own in enumerate(reversed(range(n_blocks))):
        for work in side_work.get(position, ()):
            work()
        query_block(own)
    for work in side_work.get(n_blocks, ()):
        work()


class _SampleAttn(NamedTuple):
    units: int
    t: int
    past_len: int
    n_sel: int


def _sample_slab_copies(ck_ref, cv_ref, kbuf, vbuf, sems, page, head, slot, j):
    return (pltpu.make_async_copy(ck_ref.at[page, :, head, :], kbuf.at[slot, j], sems.at[0, slot]),
            pltpu.make_async_copy(cv_ref.at[page, :, head, :], vbuf.at[slot, j], sems.at[1, slot]))


def _sample_start(cfg, pt_ref, sel_ref, ck_ref, cv_ref, kbuf, vbuf, sems, step, slot):
    for u in range(cfg.units):
        unit = step * cfg.units + u
        sb = unit // MOBA_HEADS
        sh = unit % MOBA_HEADS
        for i in range(cfg.t):
            for s in range(MOBA_TOPK):
                blk_id = sel_ref[sb, (sh * cfg.t + i) * MOBA_TOPK + s]
                for p in range(PAGES_PER_BLOCK):
                    page = pt_ref[sb, blk_id * PAGES_PER_BLOCK + p]
                    j = u * cfg.n_sel + (i * MOBA_TOPK + s) * PAGES_PER_BLOCK + p
                    for c in _sample_slab_copies(ck_ref, cv_ref, kbuf, vbuf, sems, page, sh, slot, j):
                        c.start()


def _sample_wait(cfg, ck_ref, cv_ref, kbuf, vbuf, sems, slot):
    n = cfg.units * cfg.n_sel
    pltpu.make_async_copy(ck_ref.at[pl.ds(0, n), :, 0, :], kbuf.at[slot], sems.at[0, slot]).wait()
    pltpu.make_async_copy(cv_ref.at[pl.ds(0, n), :, 0, :], vbuf.at[slot], sems.at[1, slot]).wait()


def _sample_unit_stages(cfg, sel_ref, q_ref, kn_ref, vn_ref, g_ref, o_ref, kbuf, vbuf, slot, u, sb, sh):
    t = cfg.t
    cols = slice(u * MOBA_DH, (u + 1) * MOBA_DH)
    keys_per_query = cfg.n_sel // t * PAGE_SIZE
    n_keys = cfg.n_sel * PAGE_SIZE
    vals = {}

    def scores():
        slope = _alibi_slope(sh)
        q = q_ref[:, cols]
        kn = kn_ref[:, cols]
        k_all = kbuf[slot, u * cfg.n_sel:(u + 1) * cfg.n_sel].reshape(n_keys, MOBA_DH)
        off = lax.broadcasted_iota(I32, (1, PAGE_SIZE), 1)
        k_pos = jnp.concatenate(
            [sel_ref[sb, (sh * t + i) * MOBA_TOPK + s] * MOBA_BLOCK + p * PAGE_SIZE + off
             for i in range(t) for s in range(MOBA_TOPK) for p in range(PAGES_PER_BLOCK)], axis=1)
        row = lax.broadcasted_iota(I32, (t, n_keys), 0)
        col = lax.broadcasted_iota(I32, (t, n_keys), 1)
        owned = (col >= row * keys_per_query) & (col < (row + 1) * keys_per_query)
        dist = (cfg.past_len + row) - k_pos
        s_sel = lax.dot_general(q, k_all, NT_DIMS, preferred_element_type=F32) - slope * dist.astype(F32)
        vals["s_sel"] = jnp.where(owned & (dist >= 0), s_sel, NEG)
        s_new = jnp.concatenate([jnp.sum(q * kn[j:j + 1, :], axis=-1, keepdims=True) for j in range(t)], axis=1)
        d_new = lax.broadcasted_iota(I32, (t, t), 0) - lax.broadcasted_iota(I32, (t, t), 1)
        vals["s_new"] = jnp.where(d_new >= 0, s_new - slope * d_new.astype(F32), NEG)

    def softmax():
        s_sel, s_new = vals["s_sel"], vals["s_new"]
        m = jnp.maximum(jnp.max(s_sel, axis=1, keepdims=True), jnp.max(s_new, axis=1, keepdims=True))
        vals["p_sel"] = jnp.exp(s_sel - m)
        vals["p_new"] = jnp.exp(s_new - m)
        vals["l"] = jnp.sum(vals["p_sel"], axis=1, keepdims=True) + jnp.sum(vals["p_new"], axis=1, keepdims=True)

    def values():
        vn = vn_ref[:, cols]
        v_all = vbuf[slot, u * cfg.n_sel:(u + 1) * cfg.n_sel].reshape(n_keys, MOBA_DH)
        o = jnp.dot(vals["p_sel"], v_all, preferred_element_type=F32)
        for j in range(t):
            o = o + vals["p_new"][:, j:j + 1] * vn[j:j + 1, :]
        o_ref[:, cols] = (o / vals["l"]) * _silu(g_ref[:, cols])

    return scores, softmax, values


def _moba_attn_kernel(pt_ref, sel_ref, q_ref, k_ref, v_ref, g_ref, qs_ref, kn_ref, vn_ref, gs_ref, ck_ref, cv_ref,
                      o_ref, os_ref, ka_scr, vt_scr, qa_scr, km_scr, kbuf, vbuf, sems, *, n_blocks, cfg):
    head = pl.program_id(1)
    n_steps = pl.num_programs(0) * pl.num_programs(1)
    step = pl.program_id(0) * pl.num_programs(1) + head
    slot = step % 2

    @pl.when(step == 0)
    def _():
        _sample_start(cfg, pt_ref, sel_ref, ck_ref, cv_ref, kbuf, vbuf, sems, step, 0)

    _sample_wait(cfg, ck_ref, cv_ref, kbuf, vbuf, sems, slot)
    _sample_start(cfg, pt_ref, sel_ref, ck_ref, cv_ref, kbuf, vbuf, sems, jnp.where(step + 1 == n_steps, 0, step + 1),
                  1 - slot)

    unit0 = step * cfg.units
    stages = [_sample_unit_stages(cfg, sel_ref, qs_ref, kn_ref, vn_ref, gs_ref, os_ref, kbuf, vbuf, slot, u,
                                  unit0 // MOBA_HEADS, unit0 % MOBA_HEADS + u) for u in range(cfg.units)]
    positions = (max(0, n_blocks - 5), n_blocks - 2, n_blocks)
    side_work = {position: [unit[k] for unit in stages] for k, position in enumerate(positions)}
    _moba_prompt_head(q_ref, k_ref, v_ref, g_ref, o_ref, ka_scr, vt_scr, qa_scr, km_scr, n_blocks=n_blocks, head=head,
                      side_work=side_work)

    @pl.when(step == n_steps - 1)
    def _():
        _sample_wait(cfg, ck_ref, cv_ref, kbuf, vbuf, sems, 1 - slot)


def _moba_attn(q, k, v, gate, q_s, k_s, v_s, g_s, cache_k_pages, cache_v_pages, page_table, sel, past_len):
    b, t, w = q.shape
    bs, ts, _ = q_s.shape
    assert t % MOBA_BLOCK == 0
    n_blocks = t // MOBA_BLOCK
    pen_rows = -(-n_blocks // 8) * 8
    assert AUG_PEN0 + pen_rows <= MOBA_DH
    assert MOBA_BLOCK & (MOBA_BLOCK - 1) == 0
    n_steps = b * MOBA_HEADS
    assert (bs * MOBA_HEADS) % n_steps == 0
    units = bs * MOBA_HEADS // n_steps
    assert MOBA_HEADS % units == 0 and n_blocks >= 4
    cfg = _SampleAttn(units=units, t=ts, past_len=past_len, n_sel=ts * MOBA_TOPK * PAGES_PER_BLOCK)
    steps_per_sb = MOBA_HEADS // units
    spec = pl.BlockSpec((None, t, MOBA_DH), lambda bi, h: (bi, 0, h))
    sspec = pl.BlockSpec((None, ts, units * MOBA_DH),
                         lambda bi, h: ((bi * MOBA_HEADS + h) // steps_per_sb, 0, (bi * MOBA_HEADS + h) % steps_per_sb))
    smem = pl.BlockSpec(memory_space=pltpu.SMEM)
    hbm = pl.BlockSpec(memory_space=pl.ANY)
    slab = (2, units * cfg.n_sel, PAGE_SIZE, MOBA_DH)
    return pl.pallas_call(
        functools.partial(_moba_attn_kernel, n_blocks=n_blocks, cfg=cfg),
        grid=(b, MOBA_HEADS),
        in_specs=[smem, smem, spec, spec, spec, spec, sspec, sspec, sspec, sspec, hbm, hbm],
        out_specs=[spec, sspec],
        out_shape=[jax.ShapeDtypeStruct((b, t, w), BF16), jax.ShapeDtypeStruct((bs, ts, w), F32)],
        scratch_shapes=[
            pltpu.VMEM((t, 2 * MOBA_DH), BF16),
            pltpu.VMEM((MOBA_DH, t), BF16),
            pltpu.VMEM((2 * MOBA_DH, t), BF16),
            pltpu.VMEM((4 * pen_rows, MOBA_DH), F32),
            pltpu.VMEM(slab, F32),
            pltpu.VMEM(slab, F32),
            pltpu.SemaphoreType.DMA((2, 2)),
        ],
        compiler_params=_params(("arbitrary", "arbitrary"), 56),
        name="moba_attn",
    )(page_table, sel, q, k, v, gate, q_s, k_s, v_s, g_s, cache_k_pages, cache_v_pages)


def _moba_select_kernel(q_ref, km_ref, sel_ref, *, t, topk):
    n_blocks = km_ref.shape[0]
    rows = sel_ref.shape[0]
    rid = lax.broadcasted_iota(I32, (rows, 128), 0)
    lid = lax.broadcasted_iota(I32, (rows, 128), 1)
    nid = lax.broadcasted_iota(I32, (n_blocks, 1), 0)
    out = jnp.zeros((rows, 128), I32)
    for h in range(MOBA_HEADS):
        cs = slice(h * MOBA_DH, (h + 1) * MOBA_DH)
        km = km_ref[:, h, :]
        for i in range(t):
            gs = jnp.sum(km * q_ref[i:i + 1, cs], axis=-1, keepdims=True)
            for r in range(topk):
                mx = jnp.max(gs, axis=0, keepdims=True)
                idx = jnp.min(jnp.where(gs == mx, nid, n_blocks), axis=0, keepdims=True)
                out = jnp.where((rid == h * t + i) & (lid == r), idx, out)
                gs = jnp.where(nid == idx, BELOW_NEG, gs)
    sel_ref[...] = out


def _moba_select(q, kmean):
    b, t, w = q.shape
    n_blocks = kmean.shape[1]
    rows = MOBA_HEADS * t
    return pl.pallas_call(
        functools.partial(_moba_select_kernel, t=t, topk=MOBA_TOPK),
        grid=(b,),
        in_specs=[
            pl.BlockSpec((None, t, w), lambda bi: (bi, 0, 0)),
            pl.BlockSpec((None, n_blocks, MOBA_HEADS, MOBA_DH), lambda bi: (bi, 0, 0, 0)),
        ],
        out_specs=pl.BlockSpec((None, rows, 128), lambda bi: (bi, 0, 0)),
        out_shape=jax.ShapeDtypeStruct((b, rows, 128), I32),
        compiler_params=_params(("arbitrary",), 32),
        name="moba_select",
    )(q, kmean)


def kernel(x_prompt, x_sample, c_prompt, c_sample, state_ret, cache_k, cache_v, page_table, norm_g, w_ada, b_ada,
           w_ret_in, ret_gn_g, w_ret_out, w_moba_in, moba_q_g, moba_k_g, w_moba_out):
    bp, tp, d = x_prompt.shape
    bs, ts, _ = x_sample.shape
    n_pages = page_table.shape[1]
    past_len = n_pages * PAGE_SIZE
    assert past_len % MOBA_BLOCK == 0 and ts <= MOBA_BLOCK and past_len // MOBA_BLOCK >= MOBA_TOPK
    assert w_ada.shape[0] == 2 and w_ret_in.shape[0] == 1 and w_moba_in.shape[0] == 1

    n_c = bp + bs
    c_all = jnp.concatenate([c_prompt, c_sample, jnp.zeros((-n_c % 8, d), F32)], axis=0)
    mod = _ada(c_all, w_ada, b_ada)

    def group_mod(layer):
        mp = mod[layer, :bp].reshape(bp, 1, 3 * d)
        ms = jnp.repeat(mod[layer, bp:n_c], ts, axis=0).reshape(1, bs * ts, 3 * d)
        return [(m[..., :d], m[..., d:2 * d], m[..., 2 * d:]) for m in (mp, ms)]

    xs = x_sample.reshape(1, bs * ts, d)
    ck = cache_k[0]
    cv = cache_v[0]

    tm_proj = 256
    n_steps = bp * (tp // tm_proj)
    n_blocks_past = bs * n_pages // PAGES_PER_BLOCK
    assert n_pages % PAGES_PER_BLOCK == 0 and n_blocks_past % n_steps == 0 and n_blocks_past // n_steps >= 2
    bps = n_blocks_past // n_steps
    bps_ret = min(bps - 1, -(-bps * 5 // 8))
    stream_ret = _KmStream(0, bps_ret * PAGES_PER_BLOCK, n_steps, n_pages)
    stream_moba = _KmStream(bps_ret * PAGES_PER_BLOCK * n_steps, (bps - bps_ret) * PAGES_PER_BLOCK, n_steps, n_pages)
    w_ret_in_b = w_ret_in[0].astype(BF16)

    (sh_p, sc_p, gt_p), (sh_s, sc_s, gt_s) = group_mod(0)
    proj_p, km_a = _ret_proj(x_prompt, norm_g[0], sh_p, sc_p, w_ret_in_b, BF16, tm=tm_proj,
                             km=(page_table, ck, stream_ret))
    proj_s = _ret_proj(xs, norm_g[0], sh_s, sc_s, w_ret_in_b, F32, tm=bs * ts).reshape(bs, ts, -1)
    y_p, ret_p, og_s, ret_s = _ret_mixer(proj_p, x_prompt, gt_p, ret_gn_g[0], w_ret_out[0], proj_s, state_ret[0],
                                         rows_per_step=512)
    y_s = _out_proj(og_s.reshape(1, bs * ts, RET_V).astype(BF16), w_ret_out[0], xs, gt_s, tm=bs * ts)

    (sh_p, sc_p, gt_p), (sh_s, sc_s, gt_s) = group_mod(1)
    q_p, k_p, v_p, g_p, km_b = _moba_proj(y_p, norm_g[1], sh_p, sc_p, w_moba_in[0], moba_q_g[0], moba_k_g[0], BF16,
                                          tm=tm_proj, km=(page_table, ck, stream_moba))
    q_s, k_s, v_s, g_s = _moba_proj(y_s, norm_g[1], sh_s, sc_s, w_moba_in[0], moba_q_g[0], moba_k_g[0], F32,
                                    tm=bs * ts)
    q_s, k_s, v_s, g_s = (a.reshape(bs, ts, MOBA_W) for a in (q_s, k_s, v_s, g_s))
    kmean = jnp.concatenate([km_a, km_b], axis=0).reshape(bs, past_len // MOBA_BLOCK, MOBA_HEADS, MOBA_DH)
    sel = _moba_select(q_s, kmean)[:, :, :MOBA_TOPK].reshape(bs, MOBA_HEADS * ts * MOBA_TOPK)
    oa_p, oa_s = _moba_attn(q_p, k_p, v_p, g_p, q_s, k_s, v_s, g_s, ck, cv, page_table, sel, past_len)
    y_p = _out_proj(oa_p, w_moba_out[0], y_p, gt_p, tm=1024)
    y_s = _out_proj(oa_s.reshape(1, bs * ts, MOBA_W).astype(BF16), w_moba_out[0], y_s, gt_s, tm=bs * ts)

    hd = (MOBA_HEADS, MOBA_DH)
    return (y_p, y_s.reshape(bs, ts, d), ret_p[None], ret_s[None],
            k_p.reshape(1, bp, tp, *hd), v_p.reshape(1, bp, tp, *hd),
            k_s.reshape(1, bs, ts, *hd), v_s.reshape(1, bs, ts, *hd))
```
